```python
import math
import jax, jax.numpy as jnp
from jax import lax
import numpy as np

D_MODEL = 1024
BATCH = 16
SEQ = 2048
DEPTH = 1
DEC_BATCH = 16
DEC_SEQ = 64
PAST_LEN = 4096

CHUNK = 64
Q_BLOCK = 128
MIX_WIDTH = D_MODEL
DIFF_WIDTH = MIX_WIDTH // 2
SB_WIDTH = MIX_WIDTH - DIFF_WIDTH
DIFF_HEADS = 4
DIFF_QK_DIM = 64
DIFF_V_DIM = DIFF_WIDTH // DIFF_HEADS
DIFF_QK_WIDTH = DIFF_HEADS * 2 * DIFF_QK_DIM
SB_HEADS = 8
SB_HEAD_DIM = SB_WIDTH // SB_HEADS
IN_WIDTH = 2 * DIFF_QK_WIDTH + DIFF_WIDTH + 3 * SB_WIDTH
D_FF = -(-8 * D_MODEL // (3 * 256)) * 256
RMS_EPS = 1e-6
NEG_INF = -1e30

kernel_name = 'hymba_diff_stickbreak_stream_step'


def _rms_norm(x, g):
    xf = x.astype(jnp.float32)
    y = xf * lax.rsqrt(jnp.mean(xf * xf, axis=-1, keepdims=True) + RMS_EPS)
    return (y * g.astype(jnp.float32)).astype(x.dtype)


def _alibi_slopes():
    return jnp.exp2(-8.0 / DIFF_HEADS * jnp.arange(1, DIFF_HEADS + 1, dtype=jnp.float32))


def _lambda_init(layer):
    return 0.8 - 0.6 * math.exp(-0.3 * layer)


def _project(n, w_in):
    B, T, _ = n.shape
    qkv = n @ w_in.astype(n.dtype)
    o1 = DIFF_QK_WIDTH
    o2 = o1 + DIFF_QK_WIDTH
    o3 = o2 + DIFF_WIDTH
    o4 = o3 + SB_WIDTH
    o5 = o4 + SB_WIDTH
    qd = qkv[..., :o1].reshape(B, T, DIFF_HEADS, 2 * DIFF_QK_DIM)
    kd = qkv[..., o1:o2].reshape(B, T, DIFF_HEADS, 2 * DIFF_QK_DIM)
    vd = qkv[..., o2:o3].reshape(B, T, DIFF_HEADS, DIFF_V_DIM)
    qs = qkv[..., o3:o4].reshape(B, T, SB_HEADS, SB_HEAD_DIM)
    ks = qkv[..., o4:o5].reshape(B, T, SB_HEADS, SB_HEAD_DIM)
    vs = qkv[..., o5:].reshape(B, T, SB_HEADS, SB_HEAD_DIM)
    return qd, kd, vd, qs, ks, vs


def _diff_attention(q, k, v, q_pos, k_pos, lam, lam_init, subln_g):
    B, Tq = q.shape[0], q.shape[1]
    Tk = k.shape[1]
    qf = q.astype(jnp.float32).reshape(B, Tq, DIFF_HEADS, 2, DIFF_QK_DIM)
    kf = k.astype(jnp.float32).reshape(B, Tk, DIFF_HEADS, 2, DIFF_QK_DIM)
    s = jnp.einsum('bqhmd,bkhmd->bhmqk', qf, kf) * (DIFF_QK_DIM ** -0.5)
    dist = jnp.abs(q_pos[:, None] - k_pos[None, :]).astype(jnp.float32)
    bias = -_alibi_slopes()[:, None, None] * dist
    visible = (k_pos[None, :] // CHUNK) <= (q_pos[:, None] // CHUNK)
    s = jnp.where(visible, s + bias[None, :, None], NEG_INF)
    p = jax.nn.softmax(s, axis=-1)
    a = p[:, :, 0] - lam * p[:, :, 1]
    o = jnp.einsum('bhqk,bkhd->bqhd', a, v.astype(jnp.float32))
    o = o * lax.rsqrt(jnp.mean(o * o, axis=-1, keepdims=True) + RMS_EPS)
    o = o * subln_g.astype(jnp.float32) * (1.0 - lam_init)
    return o.reshape(B, Tq, DIFF_WIDTH)


def _stick_breaking(q, k, v, q_pos, k_pos):
    B, Tq = q.shape[0], q.shape[1]
    z = jnp.einsum('bqhd,bkhd->bhqk', q.astype(jnp.float32), k.astype(jnp.float32)) * (SB_HEAD_DIM ** -0.5)
    causal = k_pos[None, :] < q_pos[:, None]
    log_keep = jnp.where(causal, jax.nn.log_sigmoid(-z), 0.0)
    after = lax.cumsum(log_keep, axis=3, reverse=True) - log_keep
    a = jnp.where(causal, jnp.exp(jax.nn.log_sigmoid(z) + after), 0.0)
    o = jnp.einsum('bhqk,bkhd->bqhd', a, v.astype(jnp.float32))
    return o.reshape(B, Tq, SB_WIDTH)


def _swiglu(h, g, w_gate, w_up, w_down):
    n = _rms_norm(h, g)
    dt = n.dtype
    return (jax.nn.silu(n @ w_gate.astype(dt)) * (n @ w_up.astype(dt))) @ w_down.astype(dt)


def setup_inputs(seed: int = 0) -> dict:
    key = jax.random.key(seed)
    ks = jax.random.split(key, 20)
    f32 = jnp.float32
    nrm = lambda k, shape, scale: jax.random.normal(k, shape, f32) * scale
    return {
        'x_prompt': nrm(ks[0], (BATCH, SEQ, D_MODEL), 1.0),
        'x_sample': nrm(ks[1], (DEC_BATCH, DEC_SEQ, D_MODEL), 1.0),
        'cache_diff_k': nrm(ks[2], (DEPTH, DEC_BATCH, PAST_LEN, DIFF_HEADS, 2 * DIFF_QK_DIM), 1.0),
        'cache_diff_v': nrm(ks[3], (DEPTH, DEC_BATCH, PAST_LEN, DIFF_HEADS, DIFF_V_DIM), 1.0),
        'cache_sb_k': nrm(ks[4], (DEPTH, DEC_BATCH, PAST_LEN, SB_HEADS, SB_HEAD_DIM), 1.0),
        'cache_sb_v': nrm(ks[5], (DEPTH, DEC_BATCH, PAST_LEN, SB_HEADS, SB_HEAD_DIM), 1.0),
        'norm_attn_g': 1.0 + nrm(ks[6], (DEPTH, D_MODEL), 0.02),
        'w_in': nrm(ks[7], (DEPTH, D_MODEL, IN_WIDTH), D_MODEL ** -0.5),
        'lambda_q1': nrm(ks[8], (DEPTH, DIFF_QK_DIM), 0.1),
        'lambda_k1': nrm(ks[9], (DEPTH, DIFF_QK_DIM), 0.1),
        'lambda_q2': nrm(ks[10], (DEPTH, DIFF_QK_DIM), 0.1),
        'lambda_k2': nrm(ks[11], (DEPTH, DIFF_QK_DIM), 0.1),
        'diff_subln_g': 1.0 + nrm(ks[12], (DEPTH, DIFF_V_DIM), 0.02),
        'w_out': nrm(ks[13], (DEPTH, MIX_WIDTH, D_MODEL), MIX_WIDTH ** -0.5),
        'norm_ffn_g': 1.0 + nrm(ks[14], (DEPTH, D_MODEL), 0.02),
        'w_gate': nrm(ks[15], (DEPTH, D_MODEL, D_FF), D_MODEL ** -0.5),
        'w_up': nrm(ks[16], (DEPTH, D_MODEL, D_FF), D_MODEL ** -0.5),
        'w_down': nrm(ks[17], (DEPTH, D_FF, D_MODEL), D_FF ** -0.5),
        'norm_final_g': 1.0 + nrm(ks[18], (D_MODEL,), 0.02),
    }


def reference(x_prompt, x_sample, cache_diff_k, cache_diff_v, cache_sb_k, cache_sb_v,
              norm_attn_g, w_in, lambda_q1, lambda_k1, lambda_q2, lambda_k2, diff_subln_g,
              w_out, norm_ffn_g, w_gate, w_up, w_down, norm_final_g):
    xp = x_prompt
    xs = x_sample
    pos_prompt = jnp.arange(SEQ, dtype=jnp.int32)
    dec_seq = x_sample.shape[1]
    pos_sample_q = PAST_LEN + jnp.arange(dec_seq, dtype=jnp.int32)
    pos_sample_k = jnp.arange(PAST_LEN + dec_seq, dtype=jnp.int32)
    p_dk, p_dv, p_sk, p_sv = [], [], [], []
    s_dk, s_dv, s_sk, s_sv = [], [], [], []
    for l in range(DEPTH):
        lam_init = _lambda_init(l)
        lam = (jnp.exp(jnp.sum(lambda_q1[l].astype(jnp.float32) * lambda_k1[l].astype(jnp.float32)))
               - jnp.exp(jnp.sum(lambda_q2[l].astype(jnp.float32) * lambda_k2[l].astype(jnp.float32)))
               + lam_init)

        n = _rms_norm(xp, norm_attn_g[l])
        qd, kd, vd, qs, ks_, vs = _project(n, w_in[l])

        def prompt_block(b):
            start = b * Q_BLOCK
            qpos = start + jnp.arange(Q_BLOCK, dtype=jnp.int32)
            qd_b = lax.dynamic_slice_in_dim(qd, start, Q_BLOCK, axis=1)
            qs_b = lax.dynamic_slice_in_dim(qs, start, Q_BLOCK, axis=1)
            od = _diff_attention(qd_b, kd, vd, qpos, pos_prompt, lam, lam_init, diff_subln_g[l])
            osb = _stick_breaking(qs_b, ks_, vs, qpos, pos_prompt)
            return jnp.concatenate([od, osb], axis=-1)

        blocks = lax.map(prompt_block, jnp.arange(SEQ // Q_BLOCK, dtype=jnp.int32))
        mixed = jnp.transpose(blocks, (1, 0, 2, 3)).reshape(xp.shape[0], SEQ, MIX_WIDTH).astype(xp.dtype)
        hp = xp + mixed @ w_out[l].astype(xp.dtype)
        xp = hp + _swiglu(hp, norm_ffn_g[l], w_gate[l], w_up[l], w_down[l])
        p_dk.append(kd); p_dv.append(vd); p_sk.append(ks_); p_sv.append(vs)

        n = _rms_norm(xs, norm_attn_g[l])
        qd2, kd2, vd2, qs2, ks2, vs2 = _project(n, w_in[l])
        kd_all = jnp.concatenate([cache_diff_k[l].astype(kd2.dtype), kd2], axis=1)
        vd_all = jnp.concatenate([cache_diff_v[l].astype(vd2.dtype), vd2], axis=1)
        ks_all = jnp.concatenate([cache_sb_k[l].astype(ks2.dtype), ks2], axis=1)
        vs_all = jnp.concatenate([cache_sb_v[l].astype(vs2.dtype), vs2], axis=1)
        od = _diff_attention(qd2, kd_all, vd_all, pos_sample_q, pos_sample_k, lam, lam_init, diff_subln_g[l])
        osb = _stick_breaking(qs2, ks_all, vs_all, pos_sample_q, pos_sample_k)
        mixed_s = jnp.concatenate([od, osb], axis=-1).astype(xs.dtype)
        hs = xs + mixed_s @ w_out[l].astype(xs.dtype)
        xs = hs + _swiglu(hs, norm_ffn_g[l], w_gate[l], w_up[l], w_down[l])
        s_dk.append(kd2); s_dv.append(vd2); s_sk.append(ks2); s_sv.append(vs2)

    y_prompt = _rms_norm(xp, norm_final_g)
    y_sample = _rms_norm(xs, norm_final_g)
    prompt_diff_k = jnp.stack(p_dk)
    prompt_diff_v = jnp.stack(p_dv)
    prompt_sb_k = jnp.stack(p_sk)
    prompt_sb_v = jnp.stack(p_sv)
    sample_diff_k = jnp.stack(s_dk)
    sample_diff_v = jnp.stack(s_dv)
    sample_sb_k = jnp.stack(s_sk)
    sample_sb_v = jnp.stack(s_sv)
    return (y_prompt, y_sample, prompt_diff_k, prompt_diff_v, prompt_sb_k, prompt_sb_v,
            sample_diff_k, sample_diff_v, sample_sb_k, sample_sb_v)
```

```python
import functools
import math

import jax
import jax.numpy as jnp
from jax import lax
from jax.experimental import pallas as pl
from jax.experimental.pallas import tpu as pltpu

D_MODEL = 1024
SEQ = 2048
DEC_SEQ = 64
PAST_LEN = 4096
CHUNK = 64
GROUP_WIDTH = 512
N_GROUPS = 6
DIFF_HEADS = 4
HEAD_LANES = 128
HALF = 64
SB_PAIRS = 4
D_FF = 2816
FF_CHUNK = 256
RMS_EPS = 1e-6
NEG_INF = -1e30
QK_SCALE = 0.125
LAMBDA_INIT = 0.8 - 0.6 * math.exp(-0.3 * 0)

ROW_TILE = 512
TQ = 256
TK = 256
CACHE_BLOCK = 256
VMEM_LIMIT = 56 * 1024 * 1024

F32 = jnp.float32
BF16 = jnp.bfloat16
NT_DIMS = (((1,), (1,)), ((), ()))
TN_DIMS = (((0,), (0,)), ((), ()))


def _rms(x, g):
    return x * lax.rsqrt(jnp.mean(x * x, axis=-1, keepdims=True) + RMS_EPS) * g


def _lambda(lamv_ref):
    lv = lamv_ref[...]
    a = jnp.sum(lv[0:1] * lv[1:2], axis=-1, keepdims=True)
    b = jnp.sum(lv[2:3] * lv[3:4], axis=-1, keepdims=True)
    return jnp.exp(a) - jnp.exp(b) + LAMBDA_INIT


def _log_keep(z):
    return jnp.minimum(-z, 0.0) - jnp.log(1.0 + jnp.exp(-jnp.abs(z)))


def _split_bf16(x):
    hi = x.astype(BF16)
    lo = (x - hi.astype(F32)).astype(BF16)
    return hi, lo


def _inproj_kernel(x_ref, g_ref, w_ref, qd_ref, qs_ref, kd_ref, vd_ref, ks_ref, vs_ref,
                   kd16_ref, vd16_ref, ks16_ref, vs16_ref):
    n = _rms(x_ref[...], g_ref[...]).astype(BF16)

    def proj(c):
        return jnp.dot(n, w_ref[:, c * GROUP_WIDTH:(c + 1) * GROUP_WIDTH], preferred_element_type=F32)

    qd_ref[...] = (proj(0) * QK_SCALE).astype(BF16)
    qs_ref[...] = (proj(3) * QK_SCALE).astype(BF16)
    for c, out_ref, out16_ref in ((1, kd_ref, kd16_ref), (2, vd_ref, vd16_ref),
                                  (4, ks_ref, ks16_ref), (5, vs_ref, vs16_ref)):
        r = proj(c)
        out_ref[...] = r
        out16_ref[...] = r.astype(BF16)


def _inproj(x, g, w16):
    rows = x.shape[0]
    row_spec = lambda width: pl.BlockSpec((ROW_TILE, width), lambda i: (i, 0))
    out_shape = ([jax.ShapeDtypeStruct((rows, GROUP_WIDTH), BF16)] * 2
                 + [jax.ShapeDtypeStruct((rows, GROUP_WIDTH), F32)] * 4
                 + [jax.ShapeDtypeStruct((rows, GROUP_WIDTH), BF16)] * 4)
    return pl.pallas_call(
        _inproj_kernel,
        grid=(rows // ROW_TILE,),
        in_specs=[row_spec(D_MODEL),
                  pl.BlockSpec((1, D_MODEL), lambda i: (0, 0)),
                  pl.BlockSpec((D_MODEL, N_GROUPS * GROUP_WIDTH), lambda i: (0, 0))],
        out_specs=[row_spec(GROUP_WIDTH)] * 10,
        out_shape=out_shape,
        compiler_params=pltpu.CompilerParams(dimension_semantics=("parallel",), vmem_limit_bytes=VMEM_LIMIT),
        name="inproj",
    )(x, g, w16)


def _ffn_kernel(x_ref, od_ref, osb_ref, wout_ref, gffn_ref, wg_ref, wu_ref, wd_ref, gfin_ref, y_ref, act_ref):
    mixed = jnp.concatenate([od_ref[...], osb_ref[...]], axis=-1)
    h = x_ref[...] + jnp.dot(mixed, wout_ref[...], preferred_element_type=F32)
    n = _rms(h, gffn_ref[...]).astype(BF16)
    for c in range(D_FF // FF_CHUNK):
        cols = slice(c * FF_CHUNK, (c + 1) * FF_CHUNK)
        gate = jnp.dot(n, wg_ref[:, cols], preferred_element_type=F32)
        up = jnp.dot(n, wu_ref[:, cols], preferred_element_type=F32)
        act_ref[:, cols] = (gate / (1.0 + jnp.exp(-gate)) * up).astype(BF16)
    y = h + jnp.dot(act_ref[...], wd_ref[...], preferred_element_type=F32)
    y_ref[...] = _rms(y, gfin_ref[...])


def _ffn(x, od, osb, wout16, gffn, wg16, wu16, wd16, gfin):
    rows = x.shape[0]
    row_spec = lambda width: pl.BlockSpec((ROW_TILE, width), lambda i: (i, 0))
    full = lambda a: pl.BlockSpec(a.shape, lambda i: (0, 0))
    return pl.pallas_call(
        _ffn_kernel,
        grid=(rows // ROW_TILE,),
        in_specs=[row_spec(D_MODEL), row_spec(GROUP_WIDTH), row_spec(GROUP_WIDTH),
                  full(wout16), full(gffn), full(wg16), full(wu16), full(wd16), full(gfin)],
        out_specs=row_spec(D_MODEL),
        out_shape=jax.ShapeDtypeStruct((rows, D_MODEL), F32),
        scratch_shapes=[pltpu.VMEM((ROW_TILE, D_FF), BF16)],
        compiler_params=pltpu.CompilerParams(dimension_semantics=("parallel",), vmem_limit_bytes=VMEM_LIMIT),
        name="outproj_ffn",
    )(x, od, osb, wout16, gffn, wg16, wu16, wd16, gfin)


def _subln(o, g):
    o = o * lax.rsqrt(jnp.mean(o * o, axis=-1, keepdims=True) + RMS_EPS)
    return o * g * (1.0 - LAMBDA_INIT)


def _diff_prompt_kernel(slope_ref, lamv_ref, g_ref, q_ref, k_ref, v_ref, o_ref,
                        boff_ref, bdiag_ref, m_ref, l_ref, acc_ref):
    slope = slope_ref[pl.program_id(1)]
    r = lax.broadcasted_iota(jnp.int32, (TQ, TK), 0)
    c = lax.broadcasted_iota(jnp.int32, (TQ, TK), 1)
    d = (c - r).astype(F32)
    boff_ref[...] = slope * d
    bdiag_ref[...] = jnp.where((c // CHUNK) <= (r // CHUNK), -slope * jnp.abs(d), NEG_INF)
    lam = _lambda(lamv_ref)
    lane = lax.broadcasted_iota(jnp.int32, (TQ, HEAD_LANES), 1)

    def flash_update(mi, s, shift, v):
        m_old = m_ref[mi]
        m_new = jnp.maximum(m_old, jnp.max(s, axis=-1, keepdims=True) + shift)
        alpha = jnp.exp(m_old - m_new)
        p = jnp.exp(s + (shift - m_new))
        l_ref[mi] = alpha * l_ref[mi] + jnp.sum(p, axis=-1, keepdims=True)
        acc_ref[mi] = alpha * acc_ref[mi] + jnp.dot(p.astype(BF16), v, preferred_element_type=F32)
        m_ref[mi] = m_new

    def q_block(i, carry):
        q_rows = pl.ds(pl.multiple_of(i * TQ, TQ), TQ)
        q = q_ref[q_rows, :]
        qm = (jnp.where(lane < HALF, q, jnp.zeros_like(q)), jnp.where(lane >= HALF, q, jnp.zeros_like(q)))
        m_ref[...] = jnp.full(m_ref.shape, NEG_INF, F32)
        l_ref[...] = jnp.zeros(l_ref.shape, F32)
        acc_ref[...] = jnp.zeros(acc_ref.shape, F32)

        def kv_block(j, carry):
            k_rows = pl.ds(pl.multiple_of(j * TK, TK), TK)
            k = k_ref[k_rows, :]
            v = v_ref[k_rows, :]
            shift = -slope * (TK * (i - j)).astype(F32)
            for mi in range(2):
                s = lax.dot_general(qm[mi], k, NT_DIMS, preferred_element_type=F32) + boff_ref[...]
                flash_update(mi, s, shift, v)
            return carry

        lax.fori_loop(0, i, kv_block, 0)
        k = k_ref[q_rows, :]
        v = v_ref[q_rows, :]
        for mi in range(2):
            s = lax.dot_general(qm[mi], k, NT_DIMS, preferred_element_type=F32) + bdiag_ref[...]
            flash_update(mi, s, 0.0, v)
        o = acc_ref[0] / l_ref[0] - lam * (acc_ref[1] / l_ref[1])
        o_ref[q_rows, :] = _subln(o, g_ref[...]).astype(BF16)
        return carry

    lax.fori_loop(0, SEQ // TQ, q_block, 0)


def _diff_prompt(slopes, lamv, g, qd, kd16, vd16):
    batch = qd.shape[0] // SEQ
    head_spec = pl.BlockSpec((SEQ, HEAD_LANES), lambda b, h: (b, h))
    return pl.pallas_call(
        _diff_prompt_kernel,
        grid=(batch, DIFF_HEADS),
        in_specs=[pl.BlockSpec(memory_space=pltpu.SMEM),
                  pl.BlockSpec(lamv.shape, lambda b, h: (0, 0)),
                  pl.BlockSpec(g.shape, lambda b, h: (0, 0)),
                  head_spec, head_spec, head_spec],
        out_specs=head_spec,
        out_shape=jax.ShapeDtypeStruct(qd.shape, BF16),
        scratch_shapes=[pltpu.VMEM((TQ, TK), F32), pltpu.VMEM((TQ, TK), F32),
                        pltpu.VMEM((2, TQ, 1), F32), pltpu.VMEM((2, TQ, 1), F32),
                        pltpu.VMEM((2, TQ, HEAD_LANES), F32)],
        compiler_params=pltpu.CompilerParams(dimension_semantics=("parallel", "parallel"),
                                             vmem_limit_bytes=VMEM_LIMIT),
        name="diff_prompt",
    )(slopes, lamv, g, qd, kd16, vd16)


def _sb_prompt_kernel(q_ref, k_ref, v_ref, o_ref, u_ref, carry_ref, acc_ref):
    r = lax.broadcasted_iota(jnp.int32, (TQ, TK), 0)
    c = lax.broadcasted_iota(jnp.int32, (TQ, TK), 1)
    u = jnp.where(r > c, 1.0, 0.0).astype(BF16)
    u_ref[0:TK, :] = u
    u_ref[TK:2 * TK, :] = u
    causal = c < r
    lane = lax.broadcasted_iota(jnp.int32, (TQ, HEAD_LANES), 1)

    def key_block(qm, j, diagonal):
        k_rows = pl.ds(pl.multiple_of(j * TK, TK), TK)
        z = lax.dot_general(qm, k_ref[k_rows, :], NT_DIMS, preferred_element_type=F32)
        lk = _log_keep(z)
        if diagonal:
            lk = jnp.where(causal, lk, 0.0)
        hi, lo = _split_bf16(lk)
        after = jnp.dot(jnp.concatenate([hi, lo], axis=-1), u_ref[...], preferred_element_type=F32)
        a = jnp.exp((z + lk) + after + carry_ref[...])
        if diagonal:
            a = jnp.where(causal, a, 0.0)
        acc_ref[...] += jnp.dot(a.astype(BF16), v_ref[k_rows, :], preferred_element_type=F32)
        carry_ref[...] += jnp.sum(lk, axis=-1, keepdims=True)

    def q_block(i, carry):
        q_rows = pl.ds(pl.multiple_of(i * TQ, TQ), TQ)
        q = q_ref[q_rows, :]
        outs = []
        for first in (True, False):
            qm = jnp.where((lane < HALF) == first, q, jnp.zeros_like(q))
            carry_ref[...] = jnp.zeros(carry_ref.shape, F32)
            acc_ref[...] = jnp.zeros(acc_ref.shape, F32)
            key_block(qm, i, True)

            def below(t, carry, qm=qm):
                key_block(qm, i - 1 - t, False)
                return carry

            lax.fori_loop(0, i, below, 0)
            outs.append(acc_ref[...])
        o_ref[q_rows, :] = jnp.where(lane < HALF, outs[0], outs[1]).astype(BF16)
        return carry

    lax.fori_loop(0, SEQ // TQ, q_block, 0)


def _sb_prompt(qs, ks16, vs16):
    batch = qs.shape[0] // SEQ
    pair_spec = pl.BlockSpec((SEQ, HEAD_LANES), lambda b, p: (b, p))
    return pl.pallas_call(
        _sb_prompt_kernel,
        grid=(batch, SB_PAIRS),
        in_specs=[pair_spec, pair_spec, pair_spec],
        out_specs=pair_spec,
        out_shape=jax.ShapeDtypeStruct(qs.shape, BF16),
        scratch_shapes=[pltpu.VMEM((2 * TK, TK), BF16), pltpu.VMEM((TQ, 1), F32),
                        pltpu.VMEM((TQ, HEAD_LANES), F32)],
        compiler_params=pltpu.CompilerParams(dimension_semantics=("parallel", "parallel"),
                                             vmem_limit_bytes=VMEM_LIMIT),
        name="sb_prompt",
    )(qs, ks16, vs16)


def _stack_queries(q):
    lane = lax.broadcasted_iota(jnp.int32, q.shape, 1)
    zero = jnp.zeros_like(q)
    return jnp.concatenate([jnp.where(lane < HALF, q, zero), jnp.where(lane >= HALF, q, zero)], axis=0)


def _diff_sample_kernel(slope_ref, lamv_ref, g_ref, q_ref, kn_ref, vn_ref, kc_ref, vc_ref, o_ref):
    slope = slope_ref[pl.program_id(1)]
    lam = _lambda(lamv_ref)
    q2 = _stack_queries(q_ref[...])
    kc = kc_ref[...].astype(BF16)
    vc = vc_ref[...].astype(BF16)
    kpos = lax.broadcasted_iota(jnp.int32, (PAST_LEN, HEAD_LANES), 0)
    qpos = PAST_LEN + (lax.broadcasted_iota(jnp.int32, (PAST_LEN, HEAD_LANES), 1) % HALF)
    s_c = lax.dot_general(kc, q2, NT_DIMS, preferred_element_type=F32) - slope * (qpos - kpos).astype(F32)
    kn_pos = lax.broadcasted_iota(jnp.int32, (DEC_SEQ, HEAD_LANES), 0)
    qn_pos = lax.broadcasted_iota(jnp.int32, (DEC_SEQ, HEAD_LANES), 1) % HALF
    s_n = (lax.dot_general(kn_ref[...], q2, NT_DIMS, preferred_element_type=F32)
           - slope * jnp.abs(qn_pos - kn_pos).astype(F32))
    m = jnp.maximum(jnp.max(s_c, axis=0, keepdims=True), jnp.max(s_n, axis=0, keepdims=True))
    p_c = jnp.exp(s_c - m)
    p_n = jnp.exp(s_n - m)
    l = jnp.sum(p_c, axis=0, keepdims=True) + jnp.sum(p_n, axis=0, keepdims=True)
    acc = (lax.dot_general(p_c.astype(BF16), vc, TN_DIMS, preferred_element_type=F32)
           + lax.dot_general(p_n.astype(BF16), vn_ref[...], TN_DIMS, preferred_element_type=F32))
    l_col = jnp.transpose(jnp.broadcast_to(l, (HEAD_LANES, HEAD_LANES)))[:, 0:1]
    o = acc / l_col
    o = o[0:DEC_SEQ] - lam * o[DEC_SEQ:2 * DEC_SEQ]
    o_ref[...] = _subln(o, g_ref[...]).astype(BF16)


def _diff_sample(slopes, lamv, g, qd, kd16, vd16, cache_k, cache_v):
    batch = cache_k.shape[0]
    new_spec = pl.BlockSpec((DEC_SEQ, HEAD_LANES), lambda b, h: (b, h))
    cache_spec = pl.BlockSpec((None, PAST_LEN, HEAD_LANES), lambda b, h: (b, 0, h))
    return pl.pallas_call(
        _diff_sample_kernel,
        grid=(batch, DIFF_HEADS),
        in_specs=[pl.BlockSpec(memory_space=pltpu.SMEM),
                  pl.BlockSpec(lamv.shape, lambda b, h: (0, 0)),
                  pl.BlockSpec(g.shape, lambda b, h: (0, 0)),
                  new_spec, new_spec, new_spec, cache_spec, cache_spec],
        out_specs=new_spec,
        out_shape=jax.ShapeDtypeStruct(qd.shape, BF16),
        compiler_params=pltpu.CompilerParams(dimension_semantics=("parallel", "parallel"),
                                             vmem_limit_bytes=VMEM_LIMIT),
        name="diff_sample",
    )(slopes, lamv, g, qd, kd16, vd16, cache_k, cache_v)


def _sb_sample_kernel(q_ref, kn_ref, vn_ref, kc_ref, vc_ref, o_ref):
    q2 = _stack_queries(q_ref[...])

    def suffix_matrix(n):
        r = lax.broadcasted_iota(jnp.int32, (n, n), 0)
        c = lax.broadcasted_iota(jnp.int32, (n, n), 1)
        return jnp.where(c > r, 1.0, 0.0).astype(BF16)

    def key_block(k, v, ut, carry, acc, visible):
        z = lax.dot_general(k, q2, NT_DIMS, preferred_element_type=F32)
        lk = _log_keep(z)
        if visible is not None:
            lk = jnp.where(visible, lk, 0.0)
        hi, lo = _split_bf16(lk)
        both = jnp.dot(ut, jnp.concatenate([hi, lo], axis=-1), preferred_element_type=F32)
        after = both[:, :HEAD_LANES] + both[:, HEAD_LANES:]
        a = jnp.exp((z + lk) + after + carry)
        if visible is not None:
            a = jnp.where(visible, a, 0.0)
        acc = acc + lax.dot_general(a.astype(BF16), v, TN_DIMS, preferred_element_type=F32)
        return carry + jnp.sum(lk, axis=0, keepdims=True), acc

    key_idx = lax.broadcasted_iota(jnp.int32, (DEC_SEQ, HEAD_LANES), 0)
    query_idx = lax.broadcasted_iota(jnp.int32, (DEC_SEQ, HEAD_LANES), 1) % HALF
    carry = jnp.zeros((1, HEAD_LANES), F32)
    acc = jnp.zeros((HEAD_LANES, HEAD_LANES), F32)
    carry, acc = key_block(kn_ref[...], vn_ref[...], suffix_matrix(DEC_SEQ), carry, acc, key_idx < query_idx)
    ut = suffix_matrix(CACHE_BLOCK)

    def cached(t, state):
        rows = pl.ds(pl.multiple_of((PAST_LEN // CACHE_BLOCK - 1 - t) * CACHE_BLOCK, CACHE_BLOCK), CACHE_BLOCK)
        return key_block(kc_ref[rows, :].astype(BF16), vc_ref[rows, :].astype(BF16), ut, *state, None)

    carry, acc = lax.fori_loop(0, PAST_LEN // CACHE_BLOCK, cached, (carry, acc))
    lane = lax.broadcasted_iota(jnp.int32, (DEC_SEQ, HEAD_LANES), 1)
    o_ref[...] = jnp.where(lane < HALF, acc[0:DEC_SEQ], acc[DEC_SEQ:2 * DEC_SEQ]).astype(BF16)


def _sb_sample(qs, ks16, vs16, cache_k, cache_v):
    batch = cache_k.shape[0]
    new_spec = pl.BlockSpec((DEC_SEQ, HEAD_LANES), lambda b, p: (b, p))
    cache_spec = pl.BlockSpec((None, PAST_LEN, HEAD_LANES), lambda b, p: (b, 0, p))
    return pl.pallas_call(
        _sb_sample_kernel,
        grid=(batch, SB_PAIRS),
        in_specs=[new_spec, new_spec, new_spec, cache_spec, cache_spec],
        out_specs=new_spec,
        out_shape=jax.ShapeDtypeStruct(qs.shape, BF16),
        compiler_params=pltpu.CompilerParams(dimension_semantics=("parallel", "parallel"),
                                             vmem_limit_bytes=VMEM_LIMIT),
        name="sb_sample",
    )(qs, ks16, vs16, cache_k, cache_v)


def kernel(x_prompt, x_sample, cache_diff_k, cache_diff_v, cache_sb_k, cache_sb_v, norm_attn_g, w_in,
           lambda_q1, lambda_k1, lambda_q2, lambda_k2, diff_subln_g, w_out, norm_ffn_g, w_gate, w_up, w_down,
           norm_final_g):
    batch, seq, _ = x_prompt.shape
    dec_batch, dec_seq, _ = x_sample.shape
    assert seq == SEQ and dec_seq == DEC_SEQ and cache_diff_k.shape[2] == PAST_LEN and w_in.shape[0] == 1

    w_in16 = w_in[0].astype(BF16)
    w_out16 = w_out[0].astype(BF16)
    w_gate16 = w_gate[0].astype(BF16)
    w_up16 = w_up[0].astype(BF16)
    w_down16 = w_down[0].astype(BF16)
    g_attn = norm_attn_g[0].reshape(1, D_MODEL)
    g_ffn = norm_ffn_g[0].reshape(1, D_MODEL)
    g_final = norm_final_g.reshape(1, D_MODEL)
    g_subln = diff_subln_g[0].reshape(1, HEAD_LANES)
    lamv = jnp.concatenate([lambda_q1, lambda_k1, lambda_q2, lambda_k2], axis=0).astype(F32)
    slopes = jnp.exp2(-8.0 / DIFF_HEADS * jnp.arange(1, DIFF_HEADS + 1, dtype=F32))

    def ffn(x, od, osb):
        return _ffn(x, od, osb, w_out16, g_ffn, w_gate16, w_up16, w_down16, g_final)

    xp = x_prompt.reshape(batch * SEQ, D_MODEL)
    qd, qs, kd, vd, ks, vs, kd16, vd16, ks16, vs16 = _inproj(xp, g_attn, w_in16)
    od = _diff_prompt(slopes, lamv, g_subln, qd, kd16, vd16)
    osb = _sb_prompt(qs, ks16, vs16)
    y_prompt = ffn(xp, od, osb).reshape(batch, SEQ, D_MODEL)

    xs = x_sample.reshape(dec_batch * DEC_SEQ, D_MODEL)
    qd2, qs2, kd2, vd2, ks2, vs2, kd2_16, vd2_16, ks2_16, vs2_16 = _inproj(xs, g_attn, w_in16)
    cache = lambda a: a[0].reshape(dec_batch, PAST_LEN, GROUP_WIDTH)
    od2 = _diff_sample(slopes, lamv, g_subln, qd2, kd2_16, vd2_16, cache(cache_diff_k), cache(cache_diff_v))
    osb2 = _sb_sample(qs2, ks2_16, vs2_16, cache(cache_sb_k), cache(cache_sb_v))
    y_sample = ffn(xs, od2, osb2).reshape(dec_batch, DEC_SEQ, D_MODEL)

    diff_shape = lambda b, t: (1, b, t, DIFF_HEADS, HEAD_LANES)
    sb_shape = lambda b, t: (1, b, t, 2 * SB_PAIRS, HALF)
    return (y_prompt, y_sample,
            kd.reshape(diff_shape(batch, SEQ)), vd.reshape(diff_shape(batch, SEQ)),
            ks.reshape(sb_shape(batch, SEQ)), vs.reshape(sb_shape(batch, SEQ)),
            kd2.reshape(diff_shape(dec_batch, DEC_SEQ)), vd2.reshape(diff_shape(dec_batch, DEC_SEQ)),
            ks2.reshape(sb_shape(dec_batch, DEC_SEQ)), vs2.reshape(sb_shape(dec_batch, DEC_SEQ)))
```

```python
import functools
import math

import jax
import jax.numpy as jnp
from jax import lax
from jax.experimental import pallas as pl
from jax.experimental.pallas import tpu as pltpu

D_MODEL = 1024
SEQ = 2048
DEC_SEQ = 64
PAST_LEN = 4096
CHUNK = 64
GROUP_WIDTH = 512
N_GROUPS = 6
DIFF_HEADS = 4
HEAD_LANES = 128
HALF = 64
SB_PAIRS = 4
D_FF = 2816
FF_CHUNK = 256
RMS_EPS = 1e-6
NEG_INF = -1e30
QK_SCALE = 0.125
LAMBDA_INIT = 0.8 - 0.6 * math.exp(-0.3 * 0)

ROW_TILE = 512
TQ = 512
TK = 512
CUM_BLOCK = 256
STAT_LANES = 128
CACHE_BLOCK = 256
VMEM_LIMIT = 56 * 1024 * 1024

F32 = jnp.float32
BF16 = jnp.bfloat16
NT_DIMS = (((1,), (1,)), ((), ()))
TN_DIMS = (((0,), (0,)), ((), ()))


def _rms(x, g):
    return x * lax.rsqrt(jnp.mean(x * x, axis=-1, keepdims=True) + RMS_EPS) * g


def _lambda(lamv_ref):
    lv = lamv_ref[...]
    a = jnp.sum(lv[0:1] * lv[1:2], axis=-1, keepdims=True)
    b = jnp.sum(lv[2:3] * lv[3:4], axis=-1, keepdims=True)
    return jnp.exp(a) - jnp.exp(b) + LAMBDA_INIT


def _log_keep(z):
    return jnp.minimum(-z, 0.0) - jnp.log(1.0 + jnp.exp(-jnp.abs(z)))


def _split_bf16(x):
    hi = x.astype(BF16)
    lo = (x - hi.astype(F32)).astype(BF16)
    return hi, lo


def _inproj_kernel(x_ref, g_ref, w_ref, qd_ref, qs_ref, kd_ref, vd_ref, ks_ref, vs_ref,
                   kd16_ref, vd16_ref, ks16_ref, vs16_ref):
    n = _rms(x_ref[...], g_ref[...]).astype(BF16)

    def proj(c):
        return jnp.dot(n, w_ref[:, c * GROUP_WIDTH:(c + 1) * GROUP_WIDTH], preferred_element_type=F32)

    qd_ref[...] = (proj(0) * QK_SCALE).astype(BF16)
    qs_ref[...] = (proj(3) * QK_SCALE).astype(BF16)
    for c, out_ref, out16_ref in ((1, kd_ref, kd16_ref), (2, vd_ref, vd16_ref),
                                  (4, ks_ref, ks16_ref), (5, vs_ref, vs16_ref)):
        r = proj(c)
        out_ref[...] = r
        out16_ref[...] = r.astype(BF16)


def _inproj(x, g, w16):
    rows = x.shape[0]
    row_spec = lambda width: pl.BlockSpec((ROW_TILE, width), lambda i: (i, 0))
    out_shape = ([jax.ShapeDtypeStruct((rows, GROUP_WIDTH), BF16)] * 2
                 + [jax.ShapeDtypeStruct((rows, GROUP_WIDTH), F32)] * 4
                 + [jax.ShapeDtypeStruct((rows, GROUP_WIDTH), BF16)] * 4)
    return pl.pallas_call(
        _inproj_kernel,
        grid=(rows // ROW_TILE,),
        in_specs=[row_spec(D_MODEL),
                  pl.BlockSpec((1, D_MODEL), lambda i: (0, 0)),
                  pl.BlockSpec((D_MODEL, N_GROUPS * GROUP_WIDTH), lambda i: (0, 0))],
        out_specs=[row_spec(GROUP_WIDTH)] * 10,
        out_shape=out_shape,
        compiler_params=pltpu.CompilerParams(dimension_semantics=("parallel",), vmem_limit_bytes=VMEM_LIMIT),
        name="inproj",
    )(x, g, w16)


def _ffn_kernel(x_ref, od_ref, osb_ref, wout_ref, gffn_ref, wg_ref, wu_ref, wd_ref, gfin_ref, y_ref, act_ref):
    mixed = jnp.concatenate([od_ref[...], osb_ref[...]], axis=-1)
    h = x_ref[...] + jnp.dot(mixed, wout_ref[...], preferred_element_type=F32)
    n = _rms(h, gffn_ref[...]).astype(BF16)
    for c in range(D_FF // FF_CHUNK):
        cols = slice(c * FF_CHUNK, (c + 1) * FF_CHUNK)
        gate = jnp.dot(n, wg_ref[:, cols], preferred_element_type=F32)
        up = jnp.dot(n, wu_ref[:, cols], preferred_element_type=F32)
        act_ref[:, cols] = (gate / (1.0 + jnp.exp(-gate)) * up).astype(BF16)
    y = h + jnp.dot(act_ref[...], wd_ref[...], preferred_element_type=F32)
    y_ref[...] = _rms(y, gfin_ref[...])


def _ffn(x, od, osb, wout16, gffn, wg16, wu16, wd16, gfin):
    rows = x.shape[0]
    row_spec = lambda width: pl.BlockSpec((ROW_TILE, width), lambda i: (i, 0))
    full = lambda a: pl.BlockSpec(a.shape, lambda i: (0, 0))
    return pl.pallas_call(
        _ffn_kernel,
        grid=(rows // ROW_TILE,),
        in_specs=[row_spec(D_MODEL), row_spec(GROUP_WIDTH), row_spec(GROUP_WIDTH),
                  full(wout16), full(gffn), full(wg16), full(wu16), full(wd16), full(gfin)],
        out_specs=row_spec(D_MODEL),
        out_shape=jax.ShapeDtypeStruct((rows, D_MODEL), F32),
        scratch_shapes=[pltpu.VMEM((ROW_TILE, D_FF), BF16)],
        compiler_params=pltpu.CompilerParams(dimension_semantics=("parallel",), vmem_limit_bytes=VMEM_LIMIT),
        name="outproj_ffn",
    )(x, od, osb, wout16, gffn, wg16, wu16, wd16, gfin)


def _subln(o, g):
    o = o * lax.rsqrt(jnp.mean(o * o, axis=-1, keepdims=True) + RMS_EPS)
    return o * g * (1.0 - LAMBDA_INIT)


def _lanes(x, width):
    return pltpu.repeat(x, width // STAT_LANES, axis=1) if width != STAT_LANES else x


def _diff_prompt_kernel(slope_ref, lamv_ref, g_ref, q_ref, k_ref, v_ref, o_ref,
                        boff_ref, bdiag_ref, m_ref, l_ref, acc_ref):
    slope = slope_ref[pl.program_id(1)]
    r = lax.broadcasted_iota(jnp.int32, (TQ, TK), 0)
    c = lax.broadcasted_iota(jnp.int32, (TQ, TK), 1)
    d = (c - r).astype(F32)
    boff_ref[...] = slope * d
    bdiag_ref[...] = jnp.where((c // CHUNK) <= (r // CHUNK), -slope * jnp.abs(d), NEG_INF)
    lam = _lambda(lamv_ref)
    lane = lax.broadcasted_iota(jnp.int32, (TQ, HEAD_LANES), 1)

    def flash_update(mi, s, shift, v):
        m_old = m_ref[mi]
        m_new = jnp.maximum(m_old, jnp.max(s, axis=-1, keepdims=True) + shift)
        alpha = jnp.exp(m_old - m_new)
        p = jnp.exp(s + _lanes(shift - m_new, TK))
        l_ref[mi] = alpha * l_ref[mi] + jnp.sum(p, axis=-1, keepdims=True)
        acc_ref[mi] = alpha * acc_ref[mi] + jnp.dot(p.astype(BF16), v, preferred_element_type=F32)
        m_ref[mi] = m_new

    def q_block(i, carry):
        q_rows = pl.ds(pl.multiple_of(i * TQ, TQ), TQ)
        q = q_ref[q_rows, :]
        qm = (jnp.where(lane < HALF, q, jnp.zeros_like(q)), jnp.where(lane >= HALF, q, jnp.zeros_like(q)))
        m_ref[...] = jnp.full(m_ref.shape, NEG_INF, F32)
        l_ref[...] = jnp.zeros(l_ref.shape, F32)
        acc_ref[...] = jnp.zeros(acc_ref.shape, F32)

        def kv_block(j, carry):
            k_rows = pl.ds(pl.multiple_of(j * TK, TK), TK)
            k = k_ref[k_rows, :]
            v = v_ref[k_rows, :]
            shift = -slope * (TK * (i - j)).astype(F32)
            for mi in range(2):
                s = lax.dot_general(qm[mi], k, NT_DIMS, preferred_element_type=F32) + boff_ref[...]
                flash_update(mi, s, shift, v)
            return carry

        lax.fori_loop(0, i, kv_block, 0)
        k = k_ref[q_rows, :]
        v = v_ref[q_rows, :]
        for mi in range(2):
            s = lax.dot_general(qm[mi], k, NT_DIMS, preferred_element_type=F32) + bdiag_ref[...]
            flash_update(mi, s, 0.0, v)
        o = acc_ref[0] / l_ref[0] - lam * (acc_ref[1] / l_ref[1])
        o_ref[q_rows, :] = _subln(o, g_ref[...]).astype(BF16)
        return carry

    lax.fori_loop(0, SEQ // TQ, q_block, 0)


def _diff_prompt(slopes, lamv, g, qd, kd16, vd16):
    batch = qd.shape[0] // SEQ
    head_spec = pl.BlockSpec((SEQ, HEAD_LANES), lambda b, h: (b, h))
    return pl.pallas_call(
        _diff_prompt_kernel,
        grid=(batch, DIFF_HEADS),
        in_specs=[pl.BlockSpec(memory_space=pltpu.SMEM),
                  pl.BlockSpec(lamv.shape, lambda b, h: (0, 0)),
                  pl.BlockSpec(g.shape, lambda b, h: (0, 0)),
                  head_spec, head_spec, head_spec],
        out_specs=head_spec,
        out_shape=jax.ShapeDtypeStruct(qd.shape, BF16),
        scratch_shapes=[pltpu.VMEM((TQ, TK), F32), pltpu.VMEM((TQ, TK), F32),
                        pltpu.VMEM((2, TQ, STAT_LANES), F32), pltpu.VMEM((2, TQ, STAT_LANES), F32),
                        pltpu.VMEM((2, TQ, HEAD_LANES), F32)],
        compiler_params=pltpu.CompilerParams(dimension_semantics=("parallel", "parallel"),
                                             vmem_limit_bytes=VMEM_LIMIT),
        name="diff_prompt",
    )(slopes, lamv, g, qd, kd16, vd16)


def _sb_prompt_kernel(q_ref, k_ref, v_ref, o_ref, u_ref, carry_ref, acc_ref):
    r = lax.broadcasted_iota(jnp.int32, (CUM_BLOCK, CUM_BLOCK), 0)
    c = lax.broadcasted_iota(jnp.int32, (CUM_BLOCK, CUM_BLOCK), 1)
    u = jnp.where(r > c, 1.0, 0.0).astype(BF16)
    u_ref[0:CUM_BLOCK, :] = u
    u_ref[CUM_BLOCK:2 * CUM_BLOCK, :] = u
    causal = lax.broadcasted_iota(jnp.int32, (TQ, TK), 1) < lax.broadcasted_iota(jnp.int32, (TQ, TK), 0)
    lane = lax.broadcasted_iota(jnp.int32, (TQ, HEAD_LANES), 1)

    def key_block(qm, j, diagonal):
        k_rows = pl.ds(pl.multiple_of(j * TK, TK), TK)
        z = lax.dot_general(qm, k_ref[k_rows, :], NT_DIMS, preferred_element_type=F32)
        lk = _log_keep(z)
        if diagonal:
            lk = jnp.where(causal, lk, 0.0)
        hi, lo = _split_bf16(lk)
        carry = carry_ref[...]
        afters = []
        for sub in reversed(range(TK // CUM_BLOCK)):
            cols = slice(sub * CUM_BLOCK, (sub + 1) * CUM_BLOCK)
            within = jnp.dot(jnp.concatenate([hi[:, cols], lo[:, cols]], axis=-1), u_ref[...],
                             preferred_element_type=F32)
            afters.append(within + _lanes(carry, CUM_BLOCK))
            carry = carry + jnp.sum(lk[:, cols], axis=-1, keepdims=True)
        after = jnp.concatenate(afters[::-1], axis=-1)
        a = jnp.exp((z + lk) + after)
        if diagonal:
            a = jnp.where(causal, a, 0.0)
        acc_ref[...] += jnp.dot(a.astype(BF16), v_ref[k_rows, :], preferred_element_type=F32)
        carry_ref[...] = carry

    def q_block(i, carry):
        q_rows = pl.ds(pl.multiple_of(i * TQ, TQ), TQ)
        q = q_ref[q_rows, :]
        outs = []
        for first in (True, False):
            qm = jnp.where((lane < HALF) == first, q, jnp.zeros_like(q))
            carry_ref[...] = jnp.zeros(carry_ref.shape, F32)
            acc_ref[...] = jnp.zeros(acc_ref.shape, F32)
            key_block(qm, i, True)

            def below(t, carry, qm=qm):
                key_block(qm, i - 1 - t, False)
                return carry

            lax.fori_loop(0, i, below, 0)
            outs.append(acc_ref[...])
        o_ref[q_rows, :] = jnp.where(lane < HALF, outs[0], outs[1]).astype(BF16)
        return carry

    lax.fori_loop(0, SEQ // TQ, q_block, 0)


def _sb_prompt(qs, ks16, vs16):
    batch = qs.shape[0] // SEQ
    pair_spec = pl.BlockSpec((SEQ, HEAD_LANES), lambda b, p: (b, p))
    return pl.pallas_call(
        _sb_prompt_kernel,
        grid=(batch, SB_PAIRS),
        in_specs=[pair_spec, pair_spec, pair_spec],
        out_specs=pair_spec,
        out_shape=jax.ShapeDtypeStruct(qs.shape, BF16),
        scratch_shapes=[pltpu.VMEM((2 * CUM_BLOCK, CUM_BLOCK), BF16), pltpu.VMEM((TQ, STAT_LANES), F32),
                        pltpu.VMEM((TQ, HEAD_LANES), F32)],
        compiler_params=pltpu.CompilerParams(dimension_semantics=("parallel", "parallel"),
                                             vmem_limit_bytes=VMEM_LIMIT),
        name="sb_prompt",
    )(qs, ks16, vs16)


def _stack_queries(q):
    lane = lax.broadcasted_iota(jnp.int32, q.shape, 1)
    zero = jnp.zeros_like(q)
    return jnp.concatenate([jnp.where(lane < HALF, q, zero), jnp.where(lane >= HALF, q, zero)], axis=0)


def _diff_sample_kernel(slope_ref, lamv_ref, g_ref, q_ref, kn_ref, vn_ref, kc_ref, vc_ref, o_ref):
    slope = slope_ref[pl.program_id(1)]
    lam = _lambda(lamv_ref)
    q2 = _stack_queries(q_ref[...])
    kc = kc_ref[...].astype(BF16)
    vc = vc_ref[...].astype(BF16)
    kpos = lax.broadcasted_iota(jnp.int32, (PAST_LEN, HEAD_LANES), 0)
    qpos = PAST_LEN + (lax.broadcasted_iota(jnp.int32, (PAST_LEN, HEAD_LANES), 1) % HALF)
    s_c = lax.dot_general(kc, q2, NT_DIMS, preferred_element_type=F32) - slope * (qpos - kpos).astype(F32)
    kn_pos = lax.broadcasted_iota(jnp.int32, (DEC_SEQ, HEAD_LANES), 0)
    qn_pos = lax.broadcasted_iota(jnp.int32, (DEC_SEQ, HEAD_LANES), 1) % HALF
    s_n = (lax.dot_general(kn_ref[...], q2, NT_DIMS, preferred_element_type=F32)
           - slope * jnp.abs(qn_pos - kn_pos).astype(F32))
    m = jnp.maximum(jnp.max(s_c, axis=0, keepdims=True), jnp.max(s_n, axis=0, keepdims=True))
    p_c = jnp.exp(s_c - m)
    p_n = jnp.exp(s_n - m)
    l = jnp.sum(p_c, axis=0, keepdims=True) + jnp.sum(p_n, axis=0, keepdims=True)
    acc = (lax.dot_general(p_c.astype(BF16), vc, TN_DIMS, preferred_element_type=F32)
           + lax.dot_general(p_n.astype(BF16), vn_ref[...], TN_DIMS, preferred_element_type=F32))
    l_col = jnp.transpose(jnp.broadcast_to(l, (HEAD_LANES, HEAD_LANES)))[:, 0:1]
    o = acc / l_col
    o = o[0:DEC_SEQ] - lam * o[DEC_SEQ:2 * DEC_SEQ]
    o_ref[...] = _subln(o, g_ref[...]).astype(BF16)


def _diff_sample(slopes, lamv, g, qd, kd16, vd16, cache_k, cache_v):
    batch = cache_k.shape[0]
    new_spec = pl.BlockSpec((DEC_SEQ, HEAD_LANES), lambda b, h: (b, h))
    cache_spec = pl.BlockSpec((None, PAST_LEN, HEAD_LANES), lambda b, h: (b, 0, h))
    return pl.pallas_call(
        _diff_sample_kernel,
        grid=(batch, DIFF_HEADS),
        in_specs=[pl.BlockSpec(memory_space=pltpu.SMEM),
                  pl.BlockSpec(lamv.shape, lambda b, h: (0, 0)),
                  pl.BlockSpec(g.shape, lambda b, h: (0, 0)),
                  new_spec, new_spec, new_spec, cache_spec, cache_spec],
        out_specs=new_spec,
        out_shape=jax.ShapeDtypeStruct(qd.shape, BF16),
        compiler_params=pltpu.CompilerParams(dimension_semantics=("parallel", "parallel"),
                                             vmem_limit_bytes=VMEM_LIMIT),
        name="diff_sample",
    )(slopes, lamv, g, qd, kd16, vd16, cache_k, cache_v)


def _sb_sample_kernel(q_ref, kn_ref, vn_ref, kc_ref, vc_ref, o_ref):
    q2 = _stack_queries(q_ref[...])

    def suffix_matrix(n):
        r = lax.broadcasted_iota(jnp.int32, (n, n), 0)
        c = lax.broadcasted_iota(jnp.int32, (n, n), 1)
        return jnp.where(c > r, 1.0, 0.0).astype(BF16)

    def within_block(ut, lk):
        hi, lo = _split_bf16(lk)
        both = jnp.dot(ut, jnp.concatenate([hi, lo], axis=-1), preferred_element_type=F32)
        return both[:, :HEAD_LANES] + both[:, HEAD_LANES:]

    key_idx = lax.broadcasted_iota(jnp.int32, (DEC_SEQ, HEAD_LANES), 0)
    query_idx = lax.broadcasted_iota(jnp.int32, (DEC_SEQ, HEAD_LANES), 1) % HALF
    visible = key_idx < query_idx
    z = lax.dot_general(kn_ref[...], q2, NT_DIMS, preferred_element_type=F32)
    lk = jnp.where(visible, _log_keep(z), 0.0)
    a = jnp.where(visible, jnp.exp((z + lk) + within_block(suffix_matrix(DEC_SEQ), lk)), 0.0)
    acc = lax.dot_general(a.astype(BF16), vn_ref[...], TN_DIMS, preferred_element_type=F32)
    carry = jnp.sum(lk, axis=0, keepdims=True)

    z = lax.dot_general(kc_ref[...].astype(BF16), q2, NT_DIMS, preferred_element_type=F32)
    lk = _log_keep(z)
    ut = suffix_matrix(CACHE_BLOCK)
    afters = [None] * (PAST_LEN // CACHE_BLOCK)
    for blk in reversed(range(PAST_LEN // CACHE_BLOCK)):
        rows = slice(blk * CACHE_BLOCK, (blk + 1) * CACHE_BLOCK)
        afters[blk] = within_block(ut, lk[rows]) + carry
        carry = carry + jnp.sum(lk[rows], axis=0, keepdims=True)
    a = jnp.exp((z + lk) + jnp.concatenate(afters, axis=0))
    acc = acc + lax.dot_general(a.astype(BF16), vc_ref[...].astype(BF16), TN_DIMS, preferred_element_type=F32)
    lane = lax.broadcasted_iota(jnp.int32, (DEC_SEQ, HEAD_LANES), 1)
    o_ref[...] = jnp.where(lane < HALF, acc[0:DEC_SEQ], acc[DEC_SEQ:2 * DEC_SEQ]).astype(BF16)


def _sb_sample(qs, ks16, vs16, cache_k, cache_v):
    batch = cache_k.shape[0]
    new_spec = pl.BlockSpec((DEC_SEQ, HEAD_LANES), lambda b, p: (b, p))
    cache_spec = pl.BlockSpec((None, PAST_LEN, HEAD_LANES), lambda b, p: (b, 0, p))
    return pl.pallas_call(
        _sb_sample_kernel,
        grid=(batch, SB_PAIRS),
        in_specs=[new_spec, new_spec, new_spec, cache_spec, cache_spec],
        out_specs=new_spec,
        out_shape=jax.ShapeDtypeStruct(qs.shape, BF16),
        compiler_params=pltpu.CompilerParams(dimension_semantics=("parallel", "parallel"),
                                             vmem_limit_bytes=VMEM_LIMIT),
        name="sb_sample",
    )(qs, ks16, vs16, cache_k, cache_v)


def kernel(x_prompt, x_sample, cache_diff_k, cache_diff_v, cache_sb_k, cache_sb_v, norm_attn_g, w_in,
           lambda_q1, lambda_k1, lambda_q2, lambda_k2, diff_subln_g, w_out, norm_ffn_g, w_gate, w_up, w_down,
           norm_final_g):
    batch, seq, _ = x_prompt.shape
    dec_batch, dec_seq, _ = x_sample.shape
    assert seq == SEQ and dec_seq == DEC_SEQ and cache_diff_k.shape[2] == PAST_LEN and w_in.shape[0] == 1

    w_in16 = w_in[0].astype(BF16)
    w_out16 = w_out[0].astype(BF16)
    w_gate16 = w_gate[0].astype(BF16)
    w_up16 = w_up[0].astype(BF16)
    w_down16 = w_down[0].astype(BF16)
    g_attn = norm_attn_g[0].reshape(1, D_MODEL)
    g_ffn = norm_ffn_g[0].reshape(1, D_MODEL)
    g_final = norm_final_g.reshape(1, D_MODEL)
    g_subln = diff_subln_g[0].reshape(1, HEAD_LANES)
    lamv = jnp.concatenate([lambda_q1, lambda_k1, lambda_q2, lambda_k2], axis=0).astype(F32)
    slopes = jnp.exp2(-8.0 / DIFF_HEADS * jnp.arange(1, DIFF_HEADS + 1, dtype=F32))

    def ffn(x, od, osb):
        return _ffn(x, od, osb, w_out16, g_ffn, w_gate16, w_up16, w_down16, g_final)

    xp = x_prompt.reshape(batch * SEQ, D_MODEL)
    qd, qs, kd, vd, ks, vs, kd16, vd16, ks16, vs16 = _inproj(xp, g_attn, w_in16)
    od = _diff_prompt(slopes, lamv, g_subln, qd, kd16, vd16)
    osb = _sb_prompt(qs, ks16, vs16)
    y_prompt = ffn(xp, od, osb).reshape(batch, SEQ, D_MODEL)

    xs = x_sample.reshape(dec_batch * DEC_SEQ, D_MODEL)
    qd2, qs2, kd2, vd2, ks2, vs2, kd2_16, vd2_16, ks2_16, vs2_16 = _inproj(xs, g_attn, w_in16)
    cache = lambda a: a[0].reshape(dec_batch, PAST_LEN, GROUP_WIDTH)
    od2 = _diff_sample(slopes, lamv, g_subln, qd2, kd2_16, vd2_16, cache(cache_diff_k), cache(cache_diff_v))
    osb2 = _sb_sample(qs2, ks2_16, vs2_16, cache(cache_sb_k), cache(cache_sb_v))
    y_sample = ffn(xs, od2, osb2).reshape(dec_batch, DEC_SEQ, D_MODEL)

    diff_shape = lambda b, t: (1, b, t, DIFF_HEADS, HEAD_LANES)
    sb_shape = lambda b, t: (1, b, t, 2 * SB_PAIRS, HALF)
    return (y_prompt, y_sample,
            kd.reshape(diff_shape(batch, SEQ)), vd.reshape(diff_shape(batch, SEQ)),
            ks.reshape(sb_shape(batch, SEQ)), vs.reshape(sb_shape(batch, SEQ)),
            kd2.reshape(diff_shape(dec_batch, DEC_SEQ)), vd2.reshape(diff_shape(dec_batch, DEC_SEQ)),
            ks2.reshape(sb_shape(dec_batch, DEC_SEQ)), vs2.reshape(sb_shape(dec_batch, DEC_SEQ)))
```

```python
import functools
import math

import jax
import jax.numpy as jnp
from jax import lax
from jax.experimental import pallas as pl
from jax.experimental.pallas import tpu as pltpu

D_MODEL = 1024
SEQ = 2048
DEC_SEQ = 64
PAST_LEN = 4096
CHUNK = 64
GROUP_WIDTH = 512
N_GROUPS = 6
DIFF_HEADS = 4
HEAD_LANES = 128
HALF = 64
SB_PAIRS = 4
D_FF = 2816
FF_CHUNK = 256
RMS_EPS = 1e-6
NEG_INF = -1e30
LOG2E = math.log2(math.e)
QK_SCALE = 0.125 * LOG2E
LAMBDA_INIT = 0.8 - 0.6 * math.exp(-0.3 * 0)

ROW_TILE = 512
TQ = 512
TK = 512
CUM_BLOCK = 256
STAT_LANES = 128
VMEM_LIMIT = 56 * 1024 * 1024

F32 = jnp.float32
BF16 = jnp.bfloat16
NT_DIMS = (((1,), (1,)), ((), ()))
TN_DIMS = (((0,), (0,)), ((), ()))


def _rms(x, g):
    return x * lax.rsqrt(jnp.mean(x * x, axis=-1, keepdims=True) + RMS_EPS) * g


def _lambda(lamv_ref):
    lv = lamv_ref[...]
    a = jnp.sum(lv[0:1] * lv[1:2], axis=-1, keepdims=True)
    b = jnp.sum(lv[2:3] * lv[3:4], axis=-1, keepdims=True)
    return jnp.exp(a) - jnp.exp(b) + LAMBDA_INIT


def _log2_keep(z2):
    nz = -z2
    return jnp.minimum(nz, 0.0) - jnp.log2(1.0 + jnp.exp2(jnp.minimum(z2, nz)))


def _split_bf16(x):
    hi = x.astype(BF16)
    lo = (x - hi.astype(F32)).astype(BF16)
    return hi, lo


def _lanes(x, width):
    return jnp.concatenate([x] * (width // STAT_LANES), axis=1) if width != STAT_LANES else x


def _suffix_matrix(n):
    j = lax.broadcasted_iota(jnp.int32, (n, n), 0)
    s = lax.broadcasted_iota(jnp.int32, (n, n), 1)
    u = jnp.where(j > s, 1.0, 0.0).astype(BF16)
    return jnp.concatenate([u, u], axis=0)


def _inproj_kernel(transpose_sb, x_ref, g_ref, w_ref, wt_ref, qd_ref, qs_ref, kd_ref, vd_ref, ks_ref, vs_ref,
                   kd16_ref, vd16_ref, ks16_ref, vs16_ref):
    n = _rms(x_ref[...], g_ref[...]).astype(BF16)

    def proj(c):
        return jnp.dot(n, w_ref[:, c * GROUP_WIDTH:(c + 1) * GROUP_WIDTH], preferred_element_type=F32)

    qd_ref[...] = (proj(0) * QK_SCALE).astype(BF16)
    qs_ref[...] = (proj(3) * QK_SCALE).astype(BF16)
    for c, out_ref, out16_ref in ((1, kd_ref, kd16_ref), (2, vd_ref, vd16_ref)):
        r = proj(c)
        out16_ref[...] = r.astype(BF16)
        for h in range(DIFF_HEADS):
            out_ref[pl.ds(h, ROW_TILE, stride=DIFF_HEADS), :] = r[:, h * HEAD_LANES:(h + 1) * HEAD_LANES]
    for c, out_ref, out16_ref in ((4, ks_ref, ks16_ref), (5, vs_ref, vs16_ref)):
        if transpose_sb:
            rows = slice((c - 4) * GROUP_WIDTH, (c - 3) * GROUP_WIDTH)
            r = lax.dot_general(wt_ref[rows, :], n, NT_DIMS, preferred_element_type=F32)
        else:
            r = proj(c)
        out_ref[...] = r
        out16_ref[...] = r.astype(BF16)


def _inproj(x, g, w16, wt16, transpose_sb):
    rows = x.shape[0]
    steps = rows // ROW_TILE
    row_spec = lambda width: pl.BlockSpec((ROW_TILE, width), lambda i: (i, 0))
    head_major = pl.BlockSpec((ROW_TILE * DIFF_HEADS, HEAD_LANES), lambda i: (i, 0))
    if transpose_sb:
        blocks_per_seq = SEQ // ROW_TILE
        batch = rows // SEQ
        sb32_spec = pl.BlockSpec((None, GROUP_WIDTH, ROW_TILE), lambda i: (i // blocks_per_seq, 0, i % blocks_per_seq))
        sb16_spec = pl.BlockSpec((None, None, GROUP_WIDTH, ROW_TILE),
                                 lambda i: (i // blocks_per_seq, i % blocks_per_seq, 0, 0))
        sb32_shape = jax.ShapeDtypeStruct((batch, GROUP_WIDTH, SEQ), F32)
        sb16_shape = jax.ShapeDtypeStruct((batch, blocks_per_seq, GROUP_WIDTH, ROW_TILE), BF16)
    else:
        sb32_spec = sb16_spec = row_spec(GROUP_WIDTH)
        sb32_shape = jax.ShapeDtypeStruct((rows, GROUP_WIDTH), F32)
        sb16_shape = jax.ShapeDtypeStruct((rows, GROUP_WIDTH), BF16)
    bf16_rows = jax.ShapeDtypeStruct((rows, GROUP_WIDTH), BF16)
    f32_head_major = jax.ShapeDtypeStruct((rows * DIFF_HEADS, HEAD_LANES), F32)
    return pl.pallas_call(
        functools.partial(_inproj_kernel, transpose_sb),
        grid=(steps,),
        in_specs=[row_spec(D_MODEL),
                  pl.BlockSpec((1, D_MODEL), lambda i: (0, 0)),
                  pl.BlockSpec(w16.shape, lambda i: (0, 0)),
                  pl.BlockSpec(wt16.shape, lambda i: (0, 0))],
        out_specs=[row_spec(GROUP_WIDTH), row_spec(GROUP_WIDTH), head_major, head_major, sb32_spec, sb32_spec,
                   row_spec(GROUP_WIDTH), row_spec(GROUP_WIDTH), sb16_spec, sb16_spec],
        out_shape=[bf16_rows, bf16_rows, f32_head_major, f32_head_major, sb32_shape, sb32_shape,
                   bf16_rows, bf16_rows, sb16_shape, sb16_shape],
        compiler_params=pltpu.CompilerParams(dimension_semantics=("parallel",), vmem_limit_bytes=VMEM_LIMIT),
        name="inproj",
    )(x, g, w16, wt16)


def _ffn_kernel(x_ref, od_ref, osb_ref, wout_ref, gffn_ref, wg_ref, wu_ref, wd_ref, gfin_ref, y_ref, act_ref):
    mixed = jnp.concatenate([od_ref[...], osb_ref[...]], axis=-1)
    h = x_ref[...] + jnp.dot(mixed, wout_ref[...], preferred_element_type=F32)
    n = _rms(h, gffn_ref[...]).astype(BF16)
    for c in range(D_FF // FF_CHUNK):
        cols = slice(c * FF_CHUNK, (c + 1) * FF_CHUNK)
        gate = jnp.dot(n, wg_ref[:, cols], preferred_element_type=F32)
        up = jnp.dot(n, wu_ref[:, cols], preferred_element_type=F32)
        act_ref[:, cols] = (gate / (1.0 + jnp.exp(-gate)) * up).astype(BF16)
    y = h + jnp.dot(act_ref[...], wd_ref[...], preferred_element_type=F32)
    y_ref[...] = _rms(y, gfin_ref[...])


def _ffn(x, od, osb, wout16, gffn, wg16, wu16, wd16, gfin):
    rows = x.shape[0]
    row_spec = lambda width: pl.BlockSpec((ROW_TILE, width), lambda i: (i, 0))
    full = lambda a: pl.BlockSpec(a.shape, lambda i: (0, 0))
    return pl.pallas_call(
        _ffn_kernel,
        grid=(rows // ROW_TILE,),
        in_specs=[row_spec(D_MODEL), row_spec(GROUP_WIDTH), row_spec(GROUP_WIDTH),
                  full(wout16), full(gffn), full(wg16), full(wu16), full(wd16), full(gfin)],
        out_specs=row_spec(D_MODEL),
        out_shape=jax.ShapeDtypeStruct((rows, D_MODEL), F32),
        scratch_shapes=[pltpu.VMEM((ROW_TILE, D_FF), BF16)],
        compiler_params=pltpu.CompilerParams(dimension_semantics=("parallel",), vmem_limit_bytes=VMEM_LIMIT),
        name="outproj_ffn",
    )(x, od, osb, wout16, gffn, wg16, wu16, wd16, gfin)


def _subln(o, g):
    o = o * lax.rsqrt(jnp.mean(o * o, axis=-1, keepdims=True) + RMS_EPS)
    return o * g * (1.0 - LAMBDA_INIT)


def _diff_prompt_kernel(slope_ref, lamv_ref, g_ref, q_ref, k_ref, v_ref, o_ref,
                        boff_ref, bdiag_ref, m_ref, l_ref, acc_ref):
    slope2 = slope_ref[pl.program_id(1)] * LOG2E
    r = lax.broadcasted_iota(jnp.int32, (TQ, TK), 0)
    c = lax.broadcasted_iota(jnp.int32, (TQ, TK), 1)
    d = (c - r).astype(F32)
    boff_ref[...] = slope2 * d
    bdiag_ref[...] = jnp.where((c // CHUNK) <= (r // CHUNK), -slope2 * jnp.abs(d), NEG_INF)
    lam = _lambda(lamv_ref)
    lane = lax.broadcasted_iota(jnp.int32, (TQ, HEAD_LANES), 1)

    def flash_update(mi, s, shift, v):
        m_old = m_ref[mi]
        m_new = jnp.maximum(m_old, jnp.max(s, axis=-1, keepdims=True) + shift)
        alpha = jnp.exp2(m_old - m_new)
        p = jnp.exp2(s + _lanes(shift - m_new, TK))
        l_ref[mi] = alpha * l_ref[mi] + jnp.sum(p, axis=-1, keepdims=True)
        acc_ref[mi] = alpha * acc_ref[mi] + jnp.dot(p.astype(BF16), v, preferred_element_type=F32)
        m_ref[mi] = m_new

    def q_block(i, carry):
        q_rows = pl.ds(pl.multiple_of(i * TQ, TQ), TQ)
        q = q_ref[q_rows, :]
        qm = (jnp.where(lane < HALF, q, jnp.zeros_like(q)), jnp.where(lane >= HALF, q, jnp.zeros_like(q)))
        m_ref[...] = jnp.full(m_ref.shape, NEG_INF, F32)
        l_ref[...] = jnp.zeros(l_ref.shape, F32)
        acc_ref[...] = jnp.zeros(acc_ref.shape, F32)

        def kv_block(j, carry):
            k_rows = pl.ds(pl.multiple_of(j * TK, TK), TK)
            k = k_ref[k_rows, :]
            v = v_ref[k_rows, :]
            shift = -slope2 * lax.convert_element_type(TK * (i - j), F32)
            for mi in range(2):
                s = lax.dot_general(qm[mi], k, NT_DIMS, preferred_element_type=F32) + boff_ref[...]
                flash_update(mi, s, shift, v)
            return carry

        lax.fori_loop(0, i, kv_block, 0)
        k = k_ref[q_rows, :]
        v = v_ref[q_rows, :]
        for mi in range(2):
            s = lax.dot_general(qm[mi], k, NT_DIMS, preferred_element_type=F32) + bdiag_ref[...]
            flash_update(mi, s, 0.0, v)
        o = acc_ref[0] / l_ref[0] - lam * (acc_ref[1] / l_ref[1])
        o_ref[q_rows, :] = _subln(o, g_ref[...]).astype(BF16)
        return carry

    lax.fori_loop(0, SEQ // TQ, q_block, 0)


def _diff_prompt(slopes, lamv, g, qd, kd16, vd16):
    batch = qd.shape[0] // SEQ
    head_spec = pl.BlockSpec((SEQ, HEAD_LANES), lambda b, h: (b, h))
    return pl.pallas_call(
        _diff_prompt_kernel,
        grid=(batch, DIFF_HEADS),
        in_specs=[pl.BlockSpec(memory_space=pltpu.SMEM),
                  pl.BlockSpec(lamv.shape, lambda b, h: (0, 0)),
                  pl.BlockSpec(g.shape, lambda b, h: (0, 0)),
                  head_spec, head_spec, head_spec],
        out_specs=head_spec,
        out_shape=jax.ShapeDtypeStruct(qd.shape, BF16),
        scratch_shapes=[pltpu.VMEM((TQ, TK), F32), pltpu.VMEM((TQ, TK), F32),
                        pltpu.VMEM((2, TQ, STAT_LANES), F32), pltpu.VMEM((2, TQ, STAT_LANES), F32),
                        pltpu.VMEM((2, TQ, HEAD_LANES), F32)],
        compiler_params=pltpu.CompilerParams(dimension_semantics=("parallel", "parallel"),
                                             vmem_limit_bytes=VMEM_LIMIT),
        name="diff_prompt",
    )(slopes, lamv, g, qd, kd16, vd16)


def _sb_prompt_kernel(q_ref, kt_ref, vt_ref, o_ref, u_ref, carry_ref, acc_ref):
    u_ref[...] = _suffix_matrix(CUM_BLOCK)
    causal = lax.broadcasted_iota(jnp.int32, (TQ, TK), 1) < lax.broadcasted_iota(jnp.int32, (TQ, TK), 0)
    lane = lax.broadcasted_iota(jnp.int32, (TQ, HEAD_LANES), 1)

    def key_block(qm, j, diagonal):
        z = jnp.dot(qm, kt_ref[j], preferred_element_type=F32)
        lk = _log2_keep(z)
        if diagonal:
            lk = jnp.where(causal, lk, 0.0)
        hi, lo = _split_bf16(lk)
        carry = carry_ref[...]
        afters = []
        for sub in reversed(range(TK // CUM_BLOCK)):
            cols = slice(sub * CUM_BLOCK, (sub + 1) * CUM_BLOCK)
            within = jnp.dot(jnp.concatenate([hi[:, cols], lo[:, cols]], axis=-1), u_ref[...],
                             preferred_element_type=F32)
            afters.append(within + _lanes(carry, CUM_BLOCK))
            carry = carry + jnp.sum(lk[:, cols], axis=-1, keepdims=True)
        after = jnp.concatenate(afters[::-1], axis=-1)
        a = jnp.exp2((z + lk) + after)
        if diagonal:
            a = jnp.where(causal, a, 0.0)
        acc_ref[...] += lax.dot_general(a.astype(BF16), vt_ref[j], NT_DIMS, preferred_element_type=F32)
        carry_ref[...] = carry

    def q_block(i, carry):
        q_rows = pl.ds(pl.multiple_of(i * TQ, TQ), TQ)
        q = q_ref[q_rows, :]
        outs = []
        for first in (True, False):
            qm = jnp.where((lane < HALF) == first, q, jnp.zeros_like(q))
            carry_ref[...] = jnp.zeros(carry_ref.shape, F32)
            acc_ref[...] = jnp.zeros(acc_ref.shape, F32)
            key_block(qm, i, True)

            def below(t, carry, qm=qm):
                key_block(qm, i - 1 - t, False)
                return carry

            lax.fori_loop(0, i, below, 0)
            outs.append(acc_ref[...])
        o_ref[q_rows, :] = jnp.where(lane < HALF, outs[0], outs[1]).astype(BF16)
        return carry

    lax.fori_loop(0, SEQ // TQ, q_block, 0)


def _sb_prompt(qs, kst16, vst16):
    batch = qs.shape[0] // SEQ
    pair_spec = pl.BlockSpec((SEQ, HEAD_LANES), lambda b, p: (b, p))
    kv_spec = pl.BlockSpec((None, SEQ // TK, HEAD_LANES, TK), lambda b, p: (b, 0, p, 0))
    return pl.pallas_call(
        _sb_prompt_kernel,
        grid=(batch, SB_PAIRS),
        in_specs=[pair_spec, kv_spec, kv_spec],
        out_specs=pair_spec,
        out_shape=jax.ShapeDtypeStruct(qs.shape, BF16),
        scratch_shapes=[pltpu.VMEM((2 * CUM_BLOCK, CUM_BLOCK), BF16), pltpu.VMEM((TQ, STAT_LANES), F32),
                        pltpu.VMEM((TQ, HEAD_LANES), F32)],
        compiler_params=pltpu.CompilerParams(dimension_semantics=("parallel", "parallel"),
                                             vmem_limit_bytes=VMEM_LIMIT),
        name="sb_prompt",
    )(qs, kst16, vst16)


def _stack_queries(q):
    lane = lax.broadcasted_iota(jnp.int32, q.shape, 1)
    zero = jnp.zeros_like(q)
    return jnp.concatenate([jnp.where(lane < HALF, q, zero), jnp.where(lane >= HALF, q, zero)], axis=0)


def _diff_sample_kernel(slope_ref, lamv_ref, g_ref, q_ref, kn_ref, vn_ref, kc_ref, vc_ref, o_ref):
    lam = _lambda(lamv_ref)
    kpos = lax.broadcasted_iota(jnp.int32, (PAST_LEN, HEAD_LANES), 0)
    qpos = PAST_LEN + (lax.broadcasted_iota(jnp.int32, (PAST_LEN, HEAD_LANES), 1) % HALF)
    dist_c = (qpos - kpos).astype(F32)
    kn_pos = lax.broadcasted_iota(jnp.int32, (DEC_SEQ, HEAD_LANES), 0)
    qn_pos = lax.broadcasted_iota(jnp.int32, (DEC_SEQ, HEAD_LANES), 1) % HALF
    dist_n = jnp.abs(qn_pos - kn_pos).astype(F32)
    for h in range(DIFF_HEADS):
        slope2 = slope_ref[h] * LOG2E
        cols = slice(h * HEAD_LANES, (h + 1) * HEAD_LANES)
        q2 = _stack_queries(q_ref[:, cols])
        kc = kc_ref[pl.ds(h, PAST_LEN, stride=DIFF_HEADS), :].astype(BF16)
        vc = vc_ref[pl.ds(h, PAST_LEN, stride=DIFF_HEADS), :].astype(BF16)
        s_c = lax.dot_general(kc, q2, NT_DIMS, preferred_element_type=F32) - slope2 * dist_c
        s_n = lax.dot_general(kn_ref[:, cols], q2, NT_DIMS, preferred_element_type=F32) - slope2 * dist_n
        m = jnp.maximum(jnp.max(s_c, axis=0, keepdims=True), jnp.max(s_n, axis=0, keepdims=True))
        p_c = jnp.exp2(s_c - m)
        p_n = jnp.exp2(s_n - m)
        l = jnp.sum(p_c, axis=0, keepdims=True) + jnp.sum(p_n, axis=0, keepdims=True)
        acc = (lax.dot_general(p_c.astype(BF16), vc, TN_DIMS, preferred_element_type=F32)
               + lax.dot_general(p_n.astype(BF16), vn_ref[:, cols], TN_DIMS, preferred_element_type=F32))
        l_col = jnp.transpose(jnp.broadcast_to(l, (HEAD_LANES, HEAD_LANES)))[:, 0:1]
        o = acc / l_col
        o = o[0:DEC_SEQ] - lam * o[DEC_SEQ:2 * DEC_SEQ]
        o_ref[:, cols] = _subln(o, g_ref[...]).astype(BF16)


def _diff_sample(slopes, lamv, g, qd, kd16, vd16, cache_k, cache_v):
    batch = cache_k.shape[0]
    new_spec = pl.BlockSpec((DEC_SEQ, GROUP_WIDTH), lambda b: (b, 0))
    cache_spec = pl.BlockSpec((None, PAST_LEN * DIFF_HEADS, HEAD_LANES), lambda b: (b, 0, 0))
    return pl.pallas_call(
        _diff_sample_kernel,
        grid=(batch,),
        in_specs=[pl.BlockSpec(memory_space=pltpu.SMEM),
                  pl.BlockSpec(lamv.shape, lambda b: (0, 0)),
                  pl.BlockSpec(g.shape, lambda b: (0, 0)),
                  new_spec, new_spec, new_spec, cache_spec, cache_spec],
        out_specs=new_spec,
        out_shape=jax.ShapeDtypeStruct(qd.shape, BF16),
        compiler_params=pltpu.CompilerParams(dimension_semantics=("parallel",), vmem_limit_bytes=VMEM_LIMIT),
        name="diff_sample",
    )(slopes, lamv, g, qd, kd16, vd16, cache_k, cache_v)


def _sb_sample_kernel(q_ref, kn_ref, vn_ref, kct_ref, vct_ref, o_ref):
    q2 = _stack_queries(q_ref[...])
    n_blocks = PAST_LEN // CUM_BLOCK

    key_idx = lax.broadcasted_iota(jnp.int32, (HEAD_LANES, DEC_SEQ), 1)
    query_idx = lax.broadcasted_iota(jnp.int32, (HEAD_LANES, DEC_SEQ), 0) % HALF
    visible = key_idx < query_idx
    z = lax.dot_general(q2, kn_ref[...], NT_DIMS, preferred_element_type=F32)
    lk = jnp.where(visible, _log2_keep(z), 0.0)
    hi, lo = _split_bf16(lk)
    within = jnp.dot(jnp.concatenate([hi, lo], axis=-1), _suffix_matrix(DEC_SEQ), preferred_element_type=F32)
    a = jnp.where(visible, jnp.exp2((z + lk) + within), 0.0)
    acc = jnp.dot(a.astype(BF16), vn_ref[...], preferred_element_type=F32)
    carry = jnp.sum(lk, axis=-1, keepdims=True)

    z = jnp.dot(q2, kct_ref[...].astype(BF16), preferred_element_type=F32)
    lk = _log2_keep(z)
    blocks = [lk[:, b * CUM_BLOCK:(b + 1) * CUM_BLOCK] for b in range(n_blocks)]
    hi, lo = _split_bf16(jnp.concatenate(blocks, axis=0))
    within = jnp.dot(jnp.concatenate([hi, lo], axis=-1), _suffix_matrix(CUM_BLOCK), preferred_element_type=F32)
    afters = [None] * n_blocks
    for b in reversed(range(n_blocks)):
        afters[b] = within[b * HEAD_LANES:(b + 1) * HEAD_LANES] + carry
        carry = carry + jnp.sum(blocks[b], axis=-1, keepdims=True)
    a = jnp.exp2((z + lk) + jnp.concatenate(afters, axis=-1))
    acc = acc + lax.dot_general(a.astype(BF16), vct_ref[...].astype(BF16), NT_DIMS, preferred_element_type=F32)
    lane = lax.broadcasted_iota(jnp.int32, (DEC_SEQ, HEAD_LANES), 1)
    o_ref[...] = jnp.where(lane < HALF, acc[0:DEC_SEQ], acc[DEC_SEQ:2 * DEC_SEQ]).astype(BF16)


def _sb_sample(qs, ks16, vs16, cache_kt, cache_vt):
    batch = cache_kt.shape[0]
    new_spec = pl.BlockSpec((DEC_SEQ, HEAD_LANES), lambda b, p: (b, p))
    cache_spec = pl.BlockSpec((None, HEAD_LANES, PAST_LEN), lambda b, p: (b, p, 0))
    return pl.pallas_call(
        _sb_sample_kernel,
        grid=(batch, SB_PAIRS),
        in_specs=[new_spec, new_spec, new_spec, cache_spec, cache_spec],
        out_specs=new_spec,
        out_shape=jax.ShapeDtypeStruct(qs.shape, BF16),
        compiler_params=pltpu.CompilerParams(dimension_semantics=("parallel", "parallel"),
                                             vmem_limit_bytes=VMEM_LIMIT),
        name="sb_sample",
    )(qs, ks16, vs16, cache_kt, cache_vt)


def kernel(x_prompt, x_sample, cache_diff_k, cache_diff_v, cache_sb_k, cache_sb_v, norm_attn_g, w_in,
           lambda_q1, lambda_k1, lambda_q2, lambda_k2, diff_subln_g, w_out, norm_ffn_g, w_gate, w_up, w_down,
           norm_final_g):
    batch, seq, _ = x_prompt.shape
    dec_batch, dec_seq, _ = x_sample.shape
    assert seq == SEQ and dec_seq == DEC_SEQ and cache_diff_k.shape[2] == PAST_LEN and w_in.shape[0] == 1

    w_in16 = w_in[0].astype(BF16)
    w_sb_t16 = jnp.transpose(w_in16[:, 4 * GROUP_WIDTH:])
    w_out16 = w_out[0].astype(BF16)
    w_gate16 = w_gate[0].astype(BF16)
    w_up16 = w_up[0].astype(BF16)
    w_down16 = w_down[0].astype(BF16)
    g_attn = norm_attn_g[0].reshape(1, D_MODEL)
    g_ffn = norm_ffn_g[0].reshape(1, D_MODEL)
    g_final = norm_final_g.reshape(1, D_MODEL)
    g_subln = diff_subln_g[0].reshape(1, HEAD_LANES)
    lamv = jnp.concatenate([lambda_q1, lambda_k1, lambda_q2, lambda_k2], axis=0).astype(F32)
    slopes = jnp.exp2(-8.0 / DIFF_HEADS * jnp.arange(1, DIFF_HEADS + 1, dtype=F32))

    def ffn(x, od, osb):
        return _ffn(x, od, osb, w_out16, g_ffn, w_gate16, w_up16, w_down16, g_final)

    xp = x_prompt.reshape(batch * SEQ, D_MODEL)
    qd, qs, kd, vd, kst, vst, kd16, vd16, kst16, vst16 = _inproj(xp, g_attn, w_in16, w_sb_t16, True)
    od = _diff_prompt(slopes, lamv, g_subln, qd, kd16, vd16)
    osb = _sb_prompt(qs, kst16, vst16)
    y_prompt = ffn(xp, od, osb).reshape(batch, SEQ, D_MODEL)

    xs = x_sample.reshape(dec_batch * DEC_SEQ, D_MODEL)
    qd2, qs2, kd2, vd2, ks2, vs2, kd2_16, vd2_16, ks2_16, vs2_16 = _inproj(xs, g_attn, w_in16, w_sb_t16, False)
    head_major = lambda a: a[0].reshape(dec_batch, PAST_LEN * DIFF_HEADS, HEAD_LANES)
    od2 = _diff_sample(slopes, lamv, g_subln, qd2, kd2_16, vd2_16, head_major(cache_diff_k), head_major(cache_diff_v))
    keys_minor = lambda a: jnp.transpose(a[0], (0, 2, 3, 1)).reshape(dec_batch, GROUP_WIDTH, PAST_LEN)
    osb2 = _sb_sample(qs2, ks2_16, vs2_16, keys_minor(cache_sb_k), keys_minor(cache_sb_v))
    y_sample = ffn(xs, od2, osb2).reshape(dec_batch, DEC_SEQ, D_MODEL)

    diff_shape = lambda b, t: (1, b, t, DIFF_HEADS, HEAD_LANES)
    sb_shape = lambda b, t: (1, b, t, 2 * SB_PAIRS, HALF)
    from_keys_minor = lambda a: jnp.transpose(a.reshape(batch, 2 * SB_PAIRS, HALF, SEQ), (0, 3, 1, 2))[None]
    return (y_prompt, y_sample,
            kd.reshape(diff_shape(batch, SEQ)), vd.reshape(diff_shape(batch, SEQ)),
            from_keys_minor(kst), from_keys_minor(vst),
            kd2.reshape(diff_shape(dec_batch, DEC_SEQ)), vd2.reshape(diff_shape(dec_batch, DEC_SEQ)),
            ks2.reshape(sb_shape(dec_batch, DEC_SEQ)), vs2.reshape(sb_shape(dec_batch, DEC_SEQ)))
```

```python
import functools
import math

import jax
import jax.numpy as jnp
from jax import lax
from jax.experimental import pallas as pl
from jax.experimental.pallas import tpu as pltpu

D_MODEL = 1024
SEQ = 2048
DEC_SEQ = 64
PAST_LEN = 4096
CHUNK = 64
GROUP_WIDTH = 512
N_GROUPS = 6
DIFF_HEADS = 4
HEAD_LANES = 128
HALF = 64
SB_PAIRS = 4
D_FF = 2816
FF_CHUNK = 256
RMS_EPS = 1e-6
NEG_INF = -1e30
LOG2E = math.log2(math.e)
QK_SCALE = 0.125 * LOG2E
LAMBDA_INIT = 0.8 - 0.6 * math.exp(-0.3 * 0)

ROW_TILE = 512
TQ = 256
TK = TQ
CUM_BLOCK = 256
SB_BLOCK = CUM_BLOCK
EXIT_LOG2 = -160.0
STAT_LANES = 128
VMEM_LIMIT = 56 * 1024 * 1024

F32 = jnp.float32
BF16 = jnp.bfloat16
NT_DIMS = (((1,), (1,)), ((), ()))
TN_DIMS = (((0,), (0,)), ((), ()))


def _rms(x, g):
    return x * lax.rsqrt(jnp.mean(x * x, axis=-1, keepdims=True) + RMS_EPS) * g


def _lambda(lamv_ref):
    lv = lamv_ref[...]
    a = jnp.sum(lv[0:1] * lv[1:2], axis=-1, keepdims=True)
    b = jnp.sum(lv[2:3] * lv[3:4], axis=-1, keepdims=True)
    return jnp.exp(a) - jnp.exp(b) + LAMBDA_INIT


def _log2_keep(z2):
    nz = -z2
    return jnp.minimum(nz, 0.0) - jnp.log2(1.0 + jnp.exp2(jnp.minimum(z2, nz)))


def _split_bf16(x):
    hi = x.astype(BF16)
    lo = (x - hi.astype(F32)).astype(BF16)
    return hi, lo


def _lanes(x, width):
    if width <= STAT_LANES:
        return x[:, :width]
    return jnp.concatenate([x] * (width // STAT_LANES), axis=1)


def _suffix_matrix(n):
    j = lax.broadcasted_iota(jnp.int32, (n, n), 0)
    s = lax.broadcasted_iota(jnp.int32, (n, n), 1)
    u = jnp.where(j > s, 1.0, 0.0).astype(BF16)
    return jnp.concatenate([u, u], axis=0)


def _stick_block(z, suffix, visible):
    lk = _log2_keep(z)
    if visible is not None:
        lk = jnp.where(visible, lk, 0.0)
    hi, lo = _split_bf16(lk)
    within = jnp.dot(jnp.concatenate([hi, lo], axis=-1), suffix, preferred_element_type=F32)
    return (z + lk) + within, jnp.sum(lk, axis=-1, keepdims=True)


def _stick_weights(log2w, carry, visible):
    a = jnp.exp2(log2w + _lanes(carry, log2w.shape[-1]))
    return a if visible is None else jnp.where(visible, a, 0.0)


def _inproj_kernel(transpose_sb, x_ref, g_ref, w_ref, wt_ref, qd_ref, qs_ref, kd_ref, vd_ref, ks_ref, vs_ref,
                   kd16_ref, vd16_ref, ks16_ref, vs16_ref):
    n = _rms(x_ref[...], g_ref[...]).astype(BF16)

    def proj(c):
        return jnp.dot(n, w_ref[:, c * GROUP_WIDTH:(c + 1) * GROUP_WIDTH], preferred_element_type=F32)

    qd_ref[...] = (proj(0) * QK_SCALE).astype(BF16)
    qs_ref[...] = (proj(3) * QK_SCALE).astype(BF16)
    for c, out_ref, out16_ref in ((1, kd_ref, kd16_ref), (2, vd_ref, vd16_ref)):
        r = proj(c)
        out16_ref[...] = r.astype(BF16)
        for h in range(DIFF_HEADS):
            out_ref[pl.ds(h, ROW_TILE, stride=DIFF_HEADS), :] = r[:, h * HEAD_LANES:(h + 1) * HEAD_LANES]
    for c, out_ref, out16_ref in ((4, ks_ref, ks16_ref), (5, vs_ref, vs16_ref)):
        if transpose_sb:
            rows = slice((c - 4) * GROUP_WIDTH, (c - 3) * GROUP_WIDTH)
            r = lax.dot_general(wt_ref[rows, :], n, NT_DIMS, preferred_element_type=F32)
            out_ref[...] = r
            for t in range(ROW_TILE // SB_BLOCK):
                out16_ref[t] = r[:, t * SB_BLOCK:(t + 1) * SB_BLOCK].astype(BF16)
        else:
            r = proj(c)
            out_ref[...] = r
            out16_ref[...] = r.astype(BF16)


def _inproj(x, g, w16, wt16, transpose_sb):
    rows = x.shape[0]
    steps = rows // ROW_TILE
    row_spec = lambda width: pl.BlockSpec((ROW_TILE, width), lambda i: (i, 0))
    head_major = pl.BlockSpec((ROW_TILE * DIFF_HEADS, HEAD_LANES), lambda i: (i, 0))
    if transpose_sb:
        blocks_per_seq = SEQ // ROW_TILE
        batch = rows // SEQ
        sb32_spec = pl.BlockSpec((None, GROUP_WIDTH, ROW_TILE), lambda i: (i // blocks_per_seq, 0, i % blocks_per_seq))
        sb16_spec = pl.BlockSpec((None, ROW_TILE // SB_BLOCK, GROUP_WIDTH, SB_BLOCK),
                                 lambda i: (i // blocks_per_seq, i % blocks_per_seq, 0, 0))
        sb32_shape = jax.ShapeDtypeStruct((batch, GROUP_WIDTH, SEQ), F32)
        sb16_shape = jax.ShapeDtypeStruct((batch, SEQ // SB_BLOCK, GROUP_WIDTH, SB_BLOCK), BF16)
    else:
        sb32_spec = sb16_spec = row_spec(GROUP_WIDTH)
        sb32_shape = jax.ShapeDtypeStruct((rows, GROUP_WIDTH), F32)
        sb16_shape = jax.ShapeDtypeStruct((rows, GROUP_WIDTH), BF16)
    bf16_rows = jax.ShapeDtypeStruct((rows, GROUP_WIDTH), BF16)
    f32_head_major = jax.ShapeDtypeStruct((rows * DIFF_HEADS, HEAD_LANES), F32)
    return pl.pallas_call(
        functools.partial(_inproj_kernel, transpose_sb),
        grid=(steps,),
        in_specs=[row_spec(D_MODEL),
                  pl.BlockSpec((1, D_MODEL), lambda i: (0, 0)),
                  pl.BlockSpec(w16.shape, lambda i: (0, 0)),
                  pl.BlockSpec(wt16.shape, lambda i: (0, 0))],
        out_specs=[row_spec(GROUP_WIDTH), row_spec(GROUP_WIDTH), head_major, head_major, sb32_spec, sb32_spec,
                   row_spec(GROUP_WIDTH), row_spec(GROUP_WIDTH), sb16_spec, sb16_spec],
        out_shape=[bf16_rows, bf16_rows, f32_head_major, f32_head_major, sb32_shape, sb32_shape,
                   bf16_rows, bf16_rows, sb16_shape, sb16_shape],
        compiler_params=pltpu.CompilerParams(dimension_semantics=("parallel",), vmem_limit_bytes=VMEM_LIMIT),
        name="inproj",
    )(x, g, w16, wt16)


def _ffn_kernel(x_ref, od_ref, osb_ref, wout_ref, gffn_ref, wg_ref, wu_ref, wd_ref, gfin_ref, y_ref, act_ref):
    mixed = jnp.concatenate([od_ref[...], osb_ref[...]], axis=-1)
    h = x_ref[...] + jnp.dot(mixed, wout_ref[...], preferred_element_type=F32)
    n = _rms(h, gffn_ref[...]).astype(BF16)
    for c in range(D_FF // FF_CHUNK):
        cols = slice(c * FF_CHUNK, (c + 1) * FF_CHUNK)
        gate = jnp.dot(n, wg_ref[:, cols], preferred_element_type=F32)
        up = jnp.dot(n, wu_ref[:, cols], preferred_element_type=F32)
        act_ref[:, cols] = (gate / (1.0 + jnp.exp(-gate)) * up).astype(BF16)
    y = h + jnp.dot(act_ref[...], wd_ref[...], preferred_element_type=F32)
    y_ref[...] = _rms(y, gfin_ref[...])


def _ffn(x, od, osb, wout16, gffn, wg16, wu16, wd16, gfin):
    rows = x.shape[0]
    row_spec = lambda width: pl.BlockSpec((ROW_TILE, width), lambda i: (i, 0))
    full = lambda a: pl.BlockSpec(a.shape, lambda i: (0, 0))
    return pl.pallas_call(
        _ffn_kernel,
        grid=(rows // ROW_TILE,),
        in_specs=[row_spec(D_MODEL), row_spec(GROUP_WIDTH), row_spec(GROUP_WIDTH),
                  full(wout16), full(gffn), full(wg16), full(wu16), full(wd16), full(gfin)],
        out_specs=row_spec(D_MODEL),
        out_shape=jax.ShapeDtypeStruct((rows, D_MODEL), F32),
        scratch_shapes=[pltpu.VMEM((ROW_TILE, D_FF), BF16)],
        compiler_params=pltpu.CompilerParams(dimension_semantics=("parallel",), vmem_limit_bytes=VMEM_LIMIT),
        name="outproj_ffn",
    )(x, od, osb, wout16, gffn, wg16, wu16, wd16, gfin)


def _subln(o, g):
    o = o * lax.rsqrt(jnp.mean(o * o, axis=-1, keepdims=True) + RMS_EPS)
    return o * g * (1.0 - LAMBDA_INIT)


def _diff_prompt_kernel(slope_ref, lamv_ref, g_ref, q_ref, k_ref, v_ref, o_ref, bias_ref):
    n_blocks = SEQ // TK
    slope2 = slope_ref[pl.program_id(1)] * LOG2E
    r = lax.broadcasted_iota(jnp.int32, (TQ, TK), 0)
    c = lax.broadcasted_iota(jnp.int32, (TQ, TK), 1)
    d = (c - r).astype(F32)
    bias_ref[0] = jnp.where((c // CHUNK) <= (r // CHUNK), -slope2 * jnp.abs(d), NEG_INF)
    for delta in range(1, n_blocks):
        bias_ref[delta] = slope2 * (d - float(TK * delta))
    lam = _lambda(lamv_ref)
    lane = lax.broadcasted_iota(jnp.int32, (TQ, HEAD_LANES), 1)
    kt = jnp.transpose(k_ref[...])

    def scores(i):
        q = q_ref[i * TQ:(i + 1) * TQ, :]
        qms = [jnp.where((lane < HALF) == first, q, jnp.zeros_like(q)) for first in (True, False)]
        raw = [jnp.dot(qm, kt[:, :TK * (i + 1)], preferred_element_type=F32) for qm in qms]
        return [jnp.concatenate([s[:, j * TK:(j + 1) * TK] + bias_ref[i - j] for j in range(i + 1)], axis=-1)
                for s in raw]

    def attend(i, s):
        e = [jnp.exp2(sm - jnp.max(sm, axis=-1, keepdims=True)) for sm in s]
        l = [jnp.sum(em, axis=-1, keepdims=True) for em in e]
        a = (e[0] - (lam * l[0] / l[1]) * e[1]).astype(BF16)
        o = jnp.dot(a, v_ref[0:TK * (i + 1), :], preferred_element_type=F32) / l[0]
        o_ref[i * TQ:(i + 1) * TQ, :] = _subln(o, g_ref[...]).astype(BF16)

    n_q = SEQ // TQ
    s_next = scores(0)
    for i in range(n_q):
        s_now = s_next
        if i + 1 < n_q:
            s_next = scores(i + 1)
        attend(i, s_now)


def _diff_prompt(slopes, lamv, g, qd, kd16, vd16):
    batch = qd.shape[0] // SEQ
    head_spec = pl.BlockSpec((SEQ, HEAD_LANES), lambda b, h: (b, h))
    return pl.pallas_call(
        _diff_prompt_kernel,
        grid=(batch, DIFF_HEADS),
        in_specs=[pl.BlockSpec(memory_space=pltpu.SMEM),
                  pl.BlockSpec(lamv.shape, lambda b, h: (0, 0)),
                  pl.BlockSpec(g.shape, lambda b, h: (0, 0)),
                  head_spec, head_spec, head_spec],
        out_specs=head_spec,
        out_shape=jax.ShapeDtypeStruct(qd.shape, BF16),
        scratch_shapes=[pltpu.VMEM((SEQ // TK, TQ, TK), F32)],
        compiler_params=pltpu.CompilerParams(dimension_semantics=("parallel", "parallel"),
                                             vmem_limit_bytes=VMEM_LIMIT),
        name="diff_prompt",
    )(slopes, lamv, g, qd, kd16, vd16)


def _sb_prompt_kernel(q_ref, kt_ref, vt_ref, o_ref, u_ref, carry_ref, acc_ref):
    u_ref[...] = _suffix_matrix(SB_BLOCK)
    causal = (lax.broadcasted_iota(jnp.int32, (SB_BLOCK, SB_BLOCK), 1)
              < lax.broadcasted_iota(jnp.int32, (SB_BLOCK, SB_BLOCK), 0))
    lane = lax.broadcasted_iota(jnp.int32, (SB_BLOCK, HEAD_LANES), 1)

    def begin(q_rows):
        q = q_ref[q_rows, :]
        carry_ref[...] = jnp.zeros(carry_ref.shape, F32)
        acc_ref[...] = jnp.zeros(acc_ref.shape, F32)
        return jnp.where(lane < HALF, q, jnp.zeros_like(q)), jnp.where(lane >= HALF, q, jnp.zeros_like(q))

    def add_blocks(qms, first, count, diagonal):
        pairs = [(h, t) for h in range(2) for t in reversed(range(count))]
        mask = {p: causal if diagonal and p[1] == count - 1 else None for p in pairs}
        z = {p: jnp.dot(qms[p[0]], kt_ref[first + p[1]], preferred_element_type=F32) for p in pairs}
        part = {p: _stick_block(z[p], u_ref[...], mask[p]) for p in pairs}
        carry = [carry_ref[0], carry_ref[1]]
        weights = {}
        for h, t in pairs:
            weights[h, t] = _stick_weights(part[h, t][0], carry[h], mask[h, t]).astype(BF16)
            carry[h] = carry[h] + part[h, t][1]
        for h in range(2):
            acc = acc_ref[h]
            for t in range(count):
                acc = acc + lax.dot_general(weights[h, t], vt_ref[first + t], NT_DIMS, preferred_element_type=F32)
            acc_ref[h] = acc
            carry_ref[h] = carry[h]

    def finish(q_rows):
        o_ref[q_rows, :] = jnp.where(lane < HALF, acc_ref[0], acc_ref[1]).astype(BF16)

    first_rows = pl.ds(0, SB_BLOCK)
    add_blocks(begin(first_rows), 0, 1, True)
    finish(first_rows)

    def q_block(i, carry):
        q_rows = pl.ds(pl.multiple_of(i * SB_BLOCK, SB_BLOCK), SB_BLOCK)
        qms = begin(q_rows)
        add_blocks(qms, i - 1, 2, True)

        def unfinished(state):
            j, top = state
            return jnp.logical_and(j >= 0, top > EXIT_LOG2)

        def earlier(state):
            j, _ = state
            add_blocks(qms, j, 1, False)
            return j - 1, jnp.max(carry_ref[...])

        lax.while_loop(unfinished, earlier, (i - 2, jnp.max(carry_ref[...])))
        finish(q_rows)
        return carry

    lax.fori_loop(1, SEQ // SB_BLOCK, q_block, 0)


def _sb_prompt(qs, kst16, vst16):
    batch = qs.shape[0] // SEQ
    pair_spec = pl.BlockSpec((SEQ, HEAD_LANES), lambda b, p: (b, p))
    kv_spec = pl.BlockSpec((None, SEQ // SB_BLOCK, HEAD_LANES, SB_BLOCK), lambda b, p: (b, 0, p, 0))
    return pl.pallas_call(
        _sb_prompt_kernel,
        grid=(batch, SB_PAIRS),
        in_specs=[pair_spec, kv_spec, kv_spec],
        out_specs=pair_spec,
        out_shape=jax.ShapeDtypeStruct(qs.shape, BF16),
        scratch_shapes=[pltpu.VMEM((2 * SB_BLOCK, SB_BLOCK), BF16), pltpu.VMEM((2, SB_BLOCK, STAT_LANES), F32),
                        pltpu.VMEM((2, SB_BLOCK, HEAD_LANES), F32)],
        compiler_params=pltpu.CompilerParams(dimension_semantics=("parallel", "parallel"),
                                             vmem_limit_bytes=VMEM_LIMIT),
        name="sb_prompt",
    )(qs, kst16, vst16)


def _stack_queries(q):
    lane = lax.broadcasted_iota(jnp.int32, q.shape, 1)
    zero = jnp.zeros_like(q)
    return jnp.concatenate([jnp.where(lane < HALF, q, zero), jnp.where(lane >= HALF, q, zero)], axis=0)


def _diff_sample_kernel(slope_ref, lamv_ref, g_ref, q_ref, kn_ref, vn_ref, kc_ref, vc_ref, o_ref):
    lam = _lambda(lamv_ref)
    kpos = lax.broadcasted_iota(jnp.int32, (PAST_LEN, HEAD_LANES), 0)
    qpos = PAST_LEN + (lax.broadcasted_iota(jnp.int32, (PAST_LEN, HEAD_LANES), 1) % HALF)
    dist_c = (qpos - kpos).astype(F32)
    kn_pos = lax.broadcasted_iota(jnp.int32, (DEC_SEQ, HEAD_LANES), 0)
    qn_pos = lax.broadcasted_iota(jnp.int32, (DEC_SEQ, HEAD_LANES), 1) % HALF
    dist_n = jnp.abs(qn_pos - kn_pos).astype(F32)
    for h in range(DIFF_HEADS):
        slope2 = slope_ref[h] * LOG2E
        cols = slice(h * HEAD_LANES, (h + 1) * HEAD_LANES)
        q2 = _stack_queries(q_ref[:, cols])
        kc = kc_ref[pl.ds(h, PAST_LEN, stride=DIFF_HEADS), :].astype(BF16)
        vc = vc_ref[pl.ds(h, PAST_LEN, stride=DIFF_HEADS), :].astype(BF16)
        s_c = lax.dot_general(kc, q2, NT_DIMS, preferred_element_type=F32) - slope2 * dist_c
        s_n = lax.dot_general(kn_ref[:, cols], q2, NT_DIMS, preferred_element_type=F32) - slope2 * dist_n
        m = jnp.maximum(jnp.max(s_c, axis=0, keepdims=True), jnp.max(s_n, axis=0, keepdims=True))
        p_c = jnp.exp2(s_c - m)
        p_n = jnp.exp2(s_n - m)
        l = jnp.sum(p_c, axis=0, keepdims=True) + jnp.sum(p_n, axis=0, keepdims=True)
        acc = (lax.dot_general(p_c.astype(BF16), vc, TN_DIMS, preferred_element_type=F32)
               + lax.dot_general(p_n.astype(BF16), vn_ref[:, cols], TN_DIMS, preferred_element_type=F32))
        l_col = jnp.transpose(jnp.broadcast_to(l, (HEAD_LANES, HEAD_LANES)))[:, 0:1]
        o = acc / l_col
        o = o[0:DEC_SEQ] - lam * o[DEC_SEQ:2 * DEC_SEQ]
        o_ref[:, cols] = _subln(o, g_ref[...]).astype(BF16)


def _diff_sample(slopes, lamv, g, qd, kd16, vd16, cache_k, cache_v):
    batch = cache_k.shape[0]
    new_spec = pl.BlockSpec((DEC_SEQ, GROUP_WIDTH), lambda b: (b, 0))
    cache_spec = pl.BlockSpec((None, PAST_LEN * DIFF_HEADS, HEAD_LANES), lambda b: (b, 0, 0))
    return pl.pallas_call(
        _diff_sample_kernel,
        grid=(batch,),
        in_specs=[pl.BlockSpec(memory_space=pltpu.SMEM),
                  pl.BlockSpec(lamv.shape, lambda b: (0, 0)),
                  pl.BlockSpec(g.shape, lambda b: (0, 0)),
                  new_spec, new_spec, new_spec, cache_spec, cache_spec],
        out_specs=new_spec,
        out_shape=jax.ShapeDtypeStruct(qd.shape, BF16),
        compiler_params=pltpu.CompilerParams(dimension_semantics=("parallel",), vmem_limit_bytes=VMEM_LIMIT),
        name="diff_sample",
    )(slopes, lamv, g, qd, kd16, vd16, cache_k, cache_v)


def _sb_sample_kernel(q_ref, kn_ref, vn_ref, kct_ref, vct_ref, o_ref):
    q2 = _stack_queries(q_ref[...])
    n_blocks = PAST_LEN // CUM_BLOCK

    key_idx = lax.broadcasted_iota(jnp.int32, (HEAD_LANES, DEC_SEQ), 1)
    query_idx = lax.broadcasted_iota(jnp.int32, (HEAD_LANES, DEC_SEQ), 0) % HALF
    visible = key_idx < query_idx
    z = lax.dot_general(q2, kn_ref[...], NT_DIMS, preferred_element_type=F32)
    log2w, carry = _stick_block(z, _suffix_matrix(DEC_SEQ), visible)
    a = jnp.where(visible, jnp.exp2(log2w), 0.0)
    acc = jnp.dot(a.astype(BF16), vn_ref[...], preferred_element_type=F32)

    z = jnp.dot(q2, kct_ref[...].astype(BF16), preferred_element_type=F32)
    lk = _log2_keep(z)
    blocks = [lk[:, b * CUM_BLOCK:(b + 1) * CUM_BLOCK] for b in range(n_blocks)]
    hi, lo = _split_bf16(jnp.concatenate(blocks, axis=0))
    within = jnp.dot(jnp.concatenate([hi, lo], axis=-1), _suffix_matrix(CUM_BLOCK), preferred_element_type=F32)
    afters = [None] * n_blocks
    for b in reversed(range(n_blocks)):
        afters[b] = within[b * HEAD_LANES:(b + 1) * HEAD_LANES] + carry
        carry = carry + jnp.sum(blocks[b], axis=-1, keepdims=True)
    a = jnp.exp2((z + lk) + jnp.concatenate(afters, axis=-1))
    acc = acc + lax.dot_general(a.astype(BF16), vct_ref[...].astype(BF16), NT_DIMS, preferred_element_type=F32)
    lane = lax.broadcasted_iota(jnp.int32, (DEC_SEQ, HEAD_LANES), 1)
    o_ref[...] = jnp.where(lane < HALF, acc[0:DEC_SEQ], acc[DEC_SEQ:2 * DEC_SEQ]).astype(BF16)


def _sb_sample(qs, ks16, vs16, cache_kt, cache_vt):
    batch = cache_kt.shape[0]
    new_spec = pl.BlockSpec((DEC_SEQ, HEAD_LANES), lambda b, p: (b, p))
    cache_spec = pl.BlockSpec((None, HEAD_LANES, PAST_LEN), lambda b, p: (b, p, 0))
    return pl.pallas_call(
        _sb_sample_kernel,
        grid=(batch, SB_PAIRS),
        in_specs=[new_spec, new_spec, new_spec, cache_spec, cache_spec],
        out_specs=new_spec,
        out_shape=jax.ShapeDtypeStruct(qs.shape, BF16),
        compiler_params=pltpu.CompilerParams(dimension_semantics=("parallel", "parallel"),
                                             vmem_limit_bytes=VMEM_LIMIT),
        name="sb_sample",
    )(qs, ks16, vs16, cache_kt, cache_vt)


def kernel(x_prompt, x_sample, cache_diff_k, cache_diff_v, cache_sb_k, cache_sb_v, norm_attn_g, w_in,
           lambda_q1, lambda_k1, lambda_q2, lambda_k2, diff_subln_g, w_out, norm_ffn_g, w_gate, w_up, w_down,
           norm_final_g):
    batch, seq, _ = x_prompt.shape
    dec_batch, dec_seq, _ = x_sample.shape
    assert seq == SEQ and dec_seq == DEC_SEQ and cache_diff_k.shape[2] == PAST_LEN and w_in.shape[0] == 1

    w_in16 = w_in[0].astype(BF16)
    w_sb_t16 = jnp.transpose(w_in16[:, 4 * GROUP_WIDTH:])
    w_out16 = w_out[0].astype(BF16)
    w_gate16 = w_gate[0].astype(BF16)
    w_up16 = w_up[0].astype(BF16)
    w_down16 = w_down[0].astype(BF16)
    g_attn = norm_attn_g[0].reshape(1, D_MODEL)
    g_ffn = norm_ffn_g[0].reshape(1, D_MODEL)
    g_final = norm_final_g.reshape(1, D_MODEL)
    g_subln = diff_subln_g[0].reshape(1, HEAD_LANES)
    lamv = jnp.concatenate([lambda_q1, lambda_k1, lambda_q2, lambda_k2], axis=0).astype(F32)
    slopes = jnp.exp2(-8.0 / DIFF_HEADS * jnp.arange(1, DIFF_HEADS + 1, dtype=F32))

    def ffn(x, od, osb):
        return _ffn(x, od, osb, w_out16, g_ffn, w_gate16, w_up16, w_down16, g_final)

    xp = x_prompt.reshape(batch * SEQ, D_MODEL)
    qd, qs, kd, vd, kst, vst, kd16, vd16, kst16, vst16 = _inproj(xp, g_attn, w_in16, w_sb_t16, True)
    od = _diff_prompt(slopes, lamv, g_subln, qd, kd16, vd16)
    osb = _sb_prompt(qs, kst16, vst16)
    y_prompt = ffn(xp, od, osb).reshape(batch, SEQ, D_MODEL)

    xs = x_sample.reshape(dec_batch * DEC_SEQ, D_MODEL)
    qd2, qs2, kd2, vd2, ks2, vs2, kd2_16, vd2_16, ks2_16, vs2_16 = _inproj(xs, g_attn, w_in16, w_sb_t16, False)
    head_major = lambda a: a[0].reshape(dec_batch, PAST_LEN * DIFF_HEADS, HEAD_LANES)
    od2 = _diff_sample(slopes, lamv, g_subln, qd2, kd2_16, vd2_16, head_major(cache_diff_k), head_major(cache_diff_v))
    keys_minor = lambda a: jnp.transpose(a[0], (0, 2, 3, 1)).reshape(dec_batch, GROUP_WIDTH, PAST_LEN)
    osb2 = _sb_sample(qs2, ks2_16, vs2_16, keys_minor(cache_sb_k), keys_minor(cache_sb_v))
    y_sample = ffn(xs, od2, osb2).reshape(dec_batch, DEC_SEQ, D_MODEL)

    diff_shape = lambda b, t: (1, b, t, DIFF_HEADS, HEAD_LANES)
    sb_shape = lambda b, t: (1, b, t, 2 * SB_PAIRS, HALF)
    from_keys_minor = lambda a: jnp.transpose(a.reshape(batch, 2 * SB_PAIRS, HALF, SEQ), (0, 3, 1, 2))[None]
    return (y_prompt, y_sample,
            kd.reshape(diff_shape(batch, SEQ)), vd.reshape(diff_shape(batch, SEQ)),
            from_keys_minor(kst), from_keys_minor(vst),
            kd2.reshape(diff_shape(dec_batch, DEC_SEQ)), vd2.reshape(diff_shape(dec_batch, DEC_SEQ)),
            ks2.reshape(sb_shape(dec_batch, DEC_SEQ)), vs2.reshape(sb_shape(dec_batch, DEC_SEQ)))
```

```python
import functools
import math

import jax
import jax.numpy as jnp
from jax import lax
from jax.experimental import pallas as pl
from jax.experimental.pallas import tpu as pltpu

D_MODEL = 1024
SEQ = 2048
DEC_SEQ = 64
PAST_LEN = 4096
CHUNK = 64
GROUP_WIDTH = 512
N_GROUPS = 6
DIFF_HEADS = 4
HEAD_LANES = 128
HALF = 64
SB_PAIRS = 4
D_FF = 2816
FF_CHUNK = 256
RMS_EPS = 1e-6
NEG_INF = -1e30
LOG2E = math.log2(math.e)
QK_SCALE = 0.125 * LOG2E
LAMBDA_INIT = 0.8 - 0.6 * math.exp(-0.3 * 0)

ROW_TILE = 512
TQ = 256
TK = TQ
CUM_BLOCK = 256
SB_BLOCK = CUM_BLOCK
EXIT_LOG2 = -160.0
STAT_LANES = 128
VMEM_LIMIT = 56 * 1024 * 1024

F32 = jnp.float32
BF16 = jnp.bfloat16
NT_DIMS = (((1,), (1,)), ((), ()))
TN_DIMS = (((0,), (0,)), ((), ()))


def _rms(x, g):
    return x * lax.rsqrt(jnp.mean(x * x, axis=-1, keepdims=True) + RMS_EPS) * g


def _lambda(lamv_ref):
    lv = lamv_ref[...]
    a = jnp.sum(lv[0:1] * lv[1:2], axis=-1, keepdims=True)
    b = jnp.sum(lv[2:3] * lv[3:4], axis=-1, keepdims=True)
    return jnp.exp(a) - jnp.exp(b) + LAMBDA_INIT


def _log2_keep(z2):
    nz = -z2
    return jnp.minimum(nz, 0.0) - jnp.log2(1.0 + jnp.exp2(jnp.minimum(z2, nz)))


def _lanes(x, width):
    if width <= STAT_LANES:
        return x[:, :width]
    return jnp.concatenate([x] * (width // STAT_LANES), axis=1)


def _suffix_matrix(n):
    j = lax.broadcasted_iota(jnp.int32, (n, n), 0)
    s = lax.broadcasted_iota(jnp.int32, (n, n), 1)
    return jnp.where(j > s, 1.0, 0.0).astype(BF16)


def _stick_block(z, suffix, visible):
    lk = _log2_keep(z)
    if visible is not None:
        lk = jnp.where(visible, lk, 0.0)
    within = jnp.dot(lk.astype(BF16), suffix, preferred_element_type=F32)
    return (z + lk) + within, jnp.sum(lk, axis=-1, keepdims=True)


def _stick_weights(log2w, carry, visible):
    a = jnp.exp2(log2w + _lanes(carry, log2w.shape[-1]))
    return a if visible is None else jnp.where(visible, a, 0.0)


def _inproj_kernel(transpose_sb, x_ref, g_ref, w_ref, wt_ref, qd_ref, qs_ref, kd_ref, vd_ref, ks_ref, vs_ref,
                   kd16_ref, vd16_ref, ks16_ref, vs16_ref):
    n = _rms(x_ref[...], g_ref[...]).astype(BF16)

    def proj(c):
        return jnp.dot(n, w_ref[:, c * GROUP_WIDTH:(c + 1) * GROUP_WIDTH], preferred_element_type=F32)

    qd_ref[...] = (proj(0) * QK_SCALE).astype(BF16)
    qs_ref[...] = (proj(3) * QK_SCALE).astype(BF16)
    for c, out_ref, out16_ref in ((1, kd_ref, kd16_ref), (2, vd_ref, vd16_ref)):
        r = proj(c)
        out16_ref[...] = r.astype(BF16)
        for h in range(DIFF_HEADS):
            out_ref[pl.ds(h, ROW_TILE, stride=DIFF_HEADS), :] = r[:, h * HEAD_LANES:(h + 1) * HEAD_LANES]
    for c, out_ref, out16_ref in ((4, ks_ref, ks16_ref), (5, vs_ref, vs16_ref)):
        if transpose_sb:
            rows = slice((c - 4) * GROUP_WIDTH, (c - 3) * GROUP_WIDTH)
            r = lax.dot_general(wt_ref[rows, :], n, NT_DIMS, preferred_element_type=F32)
            out_ref[...] = r
            for t in range(ROW_TILE // SB_BLOCK):
                out16_ref[t] = r[:, t * SB_BLOCK:(t + 1) * SB_BLOCK].astype(BF16)
        else:
            r = proj(c)
            out_ref[...] = r
            out16_ref[...] = r.astype(BF16)


def _inproj(x, g, w16, wt16, transpose_sb):
    rows = x.shape[0]
    steps = rows // ROW_TILE
    row_spec = lambda width: pl.BlockSpec((ROW_TILE, width), lambda i: (i, 0))
    head_major = pl.BlockSpec((ROW_TILE * DIFF_HEADS, HEAD_LANES), lambda i: (i, 0))
    if transpose_sb:
        blocks_per_seq = SEQ // ROW_TILE
        batch = rows // SEQ
        sb32_spec = pl.BlockSpec((None, GROUP_WIDTH, ROW_TILE), lambda i: (i // blocks_per_seq, 0, i % blocks_per_seq))
        sb16_spec = pl.BlockSpec((None, ROW_TILE // SB_BLOCK, GROUP_WIDTH, SB_BLOCK),
                                 lambda i: (i // blocks_per_seq, i % blocks_per_seq, 0, 0))
        sb32_shape = jax.ShapeDtypeStruct((batch, GROUP_WIDTH, SEQ), F32)
        sb16_shape = jax.ShapeDtypeStruct((batch, SEQ // SB_BLOCK, GROUP_WIDTH, SB_BLOCK), BF16)
    else:
        sb32_spec = sb16_spec = row_spec(GROUP_WIDTH)
        sb32_shape = jax.ShapeDtypeStruct((rows, GROUP_WIDTH), F32)
        sb16_shape = jax.ShapeDtypeStruct((rows, GROUP_WIDTH), BF16)
    bf16_rows = jax.ShapeDtypeStruct((rows, GROUP_WIDTH), BF16)
    f32_head_major = jax.ShapeDtypeStruct((rows * DIFF_HEADS, HEAD_LANES), F32)
    return pl.pallas_call(
        functools.partial(_inproj_kernel, transpose_sb),
        grid=(steps,),
        in_specs=[row_spec(D_MODEL),
                  pl.BlockSpec((1, D_MODEL), lambda i: (0, 0)),
                  pl.BlockSpec(w16.shape, lambda i: (0, 0)),
                  pl.BlockSpec(wt16.shape, lambda i: (0, 0))],
        out_specs=[row_spec(GROUP_WIDTH), row_spec(GROUP_WIDTH), head_major, head_major, sb32_spec, sb32_spec,
                   row_spec(GROUP_WIDTH), row_spec(GROUP_WIDTH), sb16_spec, sb16_spec],
        out_shape=[bf16_rows, bf16_rows, f32_head_major, f32_head_major, sb32_shape, sb32_shape,
                   bf16_rows, bf16_rows, sb16_shape, sb16_shape],
        compiler_params=pltpu.CompilerParams(dimension_semantics=("parallel",), vmem_limit_bytes=VMEM_LIMIT),
        name="inproj",
    )(x, g, w16, wt16)


def _ffn_kernel(x_ref, od_ref, osb_ref, wout_ref, gffn_ref, wg_ref, wu_ref, wd_ref, gfin_ref, y_ref, act_ref):
    mixed = jnp.concatenate([od_ref[...], osb_ref[...]], axis=-1)
    h = x_ref[...] + jnp.dot(mixed, wout_ref[...], preferred_element_type=F32)
    n = _rms(h, gffn_ref[...]).astype(BF16)
    for c in range(D_FF // FF_CHUNK):
        cols = slice(c * FF_CHUNK, (c + 1) * FF_CHUNK)
        gate = jnp.dot(n, wg_ref[:, cols], preferred_element_type=F32)
        up = jnp.dot(n, wu_ref[:, cols], preferred_element_type=F32)
        act_ref[:, cols] = (gate / (1.0 + jnp.exp(-gate)) * up).astype(BF16)
    y = h + jnp.dot(act_ref[...], wd_ref[...], preferred_element_type=F32)
    y_ref[...] = _rms(y, gfin_ref[...])


def _ffn(x, od, osb, wout16, gffn, wg16, wu16, wd16, gfin):
    rows = x.shape[0]
    row_spec = lambda width: pl.BlockSpec((ROW_TILE, width), lambda i: (i, 0))
    full = lambda a: pl.BlockSpec(a.shape, lambda i: (0, 0))
    return pl.pallas_call(
        _ffn_kernel,
        grid=(rows // ROW_TILE,),
        in_specs=[row_spec(D_MODEL), row_spec(GROUP_WIDTH), row_spec(GROUP_WIDTH),
                  full(wout16), full(gffn), full(wg16), full(wu16), full(wd16), full(gfin)],
        out_specs=row_spec(D_MODEL),
        out_shape=jax.ShapeDtypeStruct((rows, D_MODEL), F32),
        scratch_shapes=[pltpu.VMEM((ROW_TILE, D_FF), BF16)],
        compiler_params=pltpu.CompilerParams(dimension_semantics=("parallel",), vmem_limit_bytes=VMEM_LIMIT),
        name="outproj_ffn",
    )(x, od, osb, wout16, gffn, wg16, wu16, wd16, gfin)


def _subln(o, g):
    o = o * lax.rsqrt(jnp.mean(o * o, axis=-1, keepdims=True) + RMS_EPS)
    return o * g * (1.0 - LAMBDA_INIT)


def _diff_prompt_kernel(slope_ref, lamv_ref, g_ref, q_ref, k_ref, v_ref, o_ref, bias_ref):
    n_blocks = SEQ // TK
    slope2 = slope_ref[pl.program_id(1)] * LOG2E
    r = lax.broadcasted_iota(jnp.int32, (TQ, TK), 0)
    c = lax.broadcasted_iota(jnp.int32, (TQ, TK), 1)
    d = (c - r).astype(F32)
    bias_ref[0] = jnp.where((c // CHUNK) <= (r // CHUNK), -slope2 * jnp.abs(d), NEG_INF)
    for delta in range(1, n_blocks):
        bias_ref[delta] = slope2 * (d - float(TK * delta))
    lam = _lambda(lamv_ref)
    lane = lax.broadcasted_iota(jnp.int32, (TQ, HEAD_LANES), 1)
    kt = jnp.transpose(k_ref[...])

    def scores(i):
        q = q_ref[i * TQ:(i + 1) * TQ, :]
        qms = [jnp.where((lane < HALF) == first, q, jnp.zeros_like(q)) for first in (True, False)]
        raw = [jnp.dot(qm, kt[:, :TK * (i + 1)], preferred_element_type=F32) for qm in qms]
        return [jnp.concatenate([s[:, j * TK:(j + 1) * TK] + bias_ref[i - j] for j in range(i + 1)], axis=-1)
                for s in raw]

    def attend(i, s):
        e = [jnp.exp2(sm - jnp.max(sm, axis=-1, keepdims=True)) for sm in s]
        l = [jnp.sum(em, axis=-1, keepdims=True) for em in e]
        a = (e[0] - (lam * l[0] / l[1]) * e[1]).astype(BF16)
        o = jnp.dot(a, v_ref[0:TK * (i + 1), :], preferred_element_type=F32) / l[0]
        o_ref[i * TQ:(i + 1) * TQ, :] = _subln(o, g_ref[...]).astype(BF16)

    n_q = SEQ // TQ
    s_next = scores(0)
    for i in range(n_q):
        s_now = s_next
        if i + 1 < n_q:
            s_next = scores(i + 1)
        attend(i, s_now)


def _diff_prompt(slopes, lamv, g, qd, kd16, vd16):
    batch = qd.shape[0] // SEQ
    head_spec = pl.BlockSpec((SEQ, HEAD_LANES), lambda b, h: (b, h))
    return pl.pallas_call(
        _diff_prompt_kernel,
        grid=(batch, DIFF_HEADS),
        in_specs=[pl.BlockSpec(memory_space=pltpu.SMEM),
                  pl.BlockSpec(lamv.shape, lambda b, h: (0, 0)),
                  pl.BlockSpec(g.shape, lambda b, h: (0, 0)),
                  head_spec, head_spec, head_spec],
        out_specs=head_spec,
        out_shape=jax.ShapeDtypeStruct(qd.shape, BF16),
        scratch_shapes=[pltpu.VMEM((SEQ // TK, TQ, TK), F32)],
        compiler_params=pltpu.CompilerParams(dimension_semantics=("parallel", "parallel"),
                                             vmem_limit_bytes=VMEM_LIMIT),
        name="diff_prompt",
    )(slopes, lamv, g, qd, kd16, vd16)


def _sb_prompt_kernel(q_ref, kt_ref, vt_ref, o_ref, u_ref, carry_ref, acc_ref):
    u_ref[...] = _suffix_matrix(SB_BLOCK)
    causal = (lax.broadcasted_iota(jnp.int32, (SB_BLOCK, SB_BLOCK), 1)
              < lax.broadcasted_iota(jnp.int32, (SB_BLOCK, SB_BLOCK), 0))
    lane = lax.broadcasted_iota(jnp.int32, (SB_BLOCK, HEAD_LANES), 1)

    def rows_of(i):
        return pl.ds(pl.multiple_of(i * SB_BLOCK, SB_BLOCK), SB_BLOCK)

    def begin(slot, i):
        q = q_ref[rows_of(i), :]
        carry_ref[slot] = jnp.zeros(carry_ref.shape[1:], F32)
        acc_ref[slot] = jnp.zeros(acc_ref.shape[1:], F32)
        return jnp.where(lane < HALF, q, jnp.zeros_like(q)), jnp.where(lane >= HALF, q, jnp.zeros_like(q))

    def add_blocks(items, diagonal):
        chains = [(n, h, t) for n, item in enumerate(items) for h in range(2) for t in reversed(range(item[3]))]
        mask = {c: causal if diagonal and c[2] == items[c[0]][3] - 1 else None for c in chains}
        z = {(n, h, t): jnp.dot(items[n][1][h], kt_ref[items[n][2] + t], preferred_element_type=F32)
             for n, h, t in chains}
        part = {c: _stick_block(z[c], u_ref[...], mask[c]) for c in chains}
        carry = {(n, h): carry_ref[item[0], h] for n, item in enumerate(items) for h in range(2)}
        weights = {}
        for n, h, t in chains:
            weights[n, h, t] = _stick_weights(part[n, h, t][0], carry[n, h], mask[n, h, t]).astype(BF16)
            carry[n, h] = carry[n, h] + part[n, h, t][1]
        for n, (slot, _, first, count) in enumerate(items):
            for h in range(2):
                acc = acc_ref[slot, h]
                for t in range(count):
                    acc = acc + lax.dot_general(weights[n, h, t], vt_ref[first + t], NT_DIMS,
                                                preferred_element_type=F32)
                acc_ref[slot, h] = acc
                carry_ref[slot, h] = carry[n, h]

    def finish(slot, qms, i, j):
        def unfinished(state):
            j, top = state
            return jnp.logical_and(j >= 0, top > EXIT_LOG2)

        def earlier(state):
            j, _ = state
            add_blocks([(slot, qms, j, 1)], False)
            return j - 1, jnp.max(carry_ref[slot])

        lax.while_loop(unfinished, earlier, (j, jnp.max(carry_ref[slot])))
        o_ref[rows_of(i), :] = jnp.where(lane < HALF, acc_ref[slot, 0], acc_ref[slot, 1]).astype(BF16)

    def query_pair(i_a, count_a, i_b):
        qms_a, qms_b = begin(0, i_a), begin(1, i_b)
        first_a = i_a - (count_a - 1)
        add_blocks([(0, qms_a, first_a, count_a), (1, qms_b, i_b - 1, 2)], True)
        finish(0, qms_a, i_a, first_a - 1)
        finish(1, qms_b, i_b, i_b - 2)

    n_q = SEQ // SB_BLOCK
    query_pair(0, 1, n_q - 1)

    def middle(m, carry):
        query_pair(2 * m + 1, 2, 2 * m + 2)
        return carry

    lax.fori_loop(0, (n_q - 2) // 2, middle, 0)


def _sb_prompt(qs, kst16, vst16):
    batch = qs.shape[0] // SEQ
    pair_spec = pl.BlockSpec((SEQ, HEAD_LANES), lambda b, p: (b, p))
    kv_spec = pl.BlockSpec((None, SEQ // SB_BLOCK, HEAD_LANES, SB_BLOCK), lambda b, p: (b, 0, p, 0))
    return pl.pallas_call(
        _sb_prompt_kernel,
        grid=(batch, SB_PAIRS),
        in_specs=[pair_spec, kv_spec, kv_spec],
        out_specs=pair_spec,
        out_shape=jax.ShapeDtypeStruct(qs.shape, BF16),
        scratch_shapes=[pltpu.VMEM((SB_BLOCK, SB_BLOCK), BF16),pltpu.VMEM((2, 2, SB_BLOCK, STAT_LANES), F32),
                        pltpu.VMEM((2, 2, SB_BLOCK, HEAD_LANES), F32)],
        compiler_params=pltpu.CompilerParams(dimension_semantics=("parallel", "parallel"),
                                             vmem_limit_bytes=VMEM_LIMIT),
        name="sb_prompt",
    )(qs, kst16, vst16)


def _stack_queries(q):
    lane = lax.broadcasted_iota(jnp.int32, q.shape, 1)
    zero = jnp.zeros_like(q)
    return jnp.concatenate([jnp.where(lane < HALF, q, zero), jnp.where(lane >= HALF, q, zero)], axis=0)


def _diff_sample_kernel(slope_ref, lamv_ref, g_ref, q_ref, kn_ref, vn_ref, kc_ref, vc_ref, o_ref):
    lam = _lambda(lamv_ref)
    kpos = lax.broadcasted_iota(jnp.int32, (PAST_LEN, HEAD_LANES), 0)
    qpos = PAST_LEN + (lax.broadcasted_iota(jnp.int32, (PAST_LEN, HEAD_LANES), 1) % HALF)
    dist_c = (qpos - kpos).astype(F32)
    kn_pos = lax.broadcasted_iota(jnp.int32, (DEC_SEQ, HEAD_LANES), 0)
    qn_pos = lax.broadcasted_iota(jnp.int32, (DEC_SEQ, HEAD_LANES), 1) % HALF
    dist_n = jnp.abs(qn_pos - kn_pos).astype(F32)
    cols = [slice(h * HEAD_LANES, (h + 1) * HEAD_LANES) for h in range(DIFF_HEADS)]
    head_rows = lambda ref, h: ref[pl.ds(h, PAST_LEN, stride=DIFF_HEADS), :].astype(BF16)
    for heads in ((0, 1), (2, 3)):
        s_c, s_n = {}, {}
        for h in heads:
            slope2 = slope_ref[h] * LOG2E
            q2 = _stack_queries(q_ref[:, cols[h]])
            s_c[h] = lax.dot_general(head_rows(kc_ref, h), q2, NT_DIMS, preferred_element_type=F32) - slope2 * dist_c
            s_n[h] = lax.dot_general(kn_ref[:, cols[h]], q2, NT_DIMS, preferred_element_type=F32) - slope2 * dist_n
        p_c, p_n, l = {}, {}, {}
        for h in heads:
            m = jnp.maximum(jnp.max(s_c[h], axis=0, keepdims=True), jnp.max(s_n[h], axis=0, keepdims=True))
            p_c[h] = jnp.exp2(s_c[h] - m)
            p_n[h] = jnp.exp2(s_n[h] - m)
            l[h] = jnp.sum(p_c[h], axis=0, keepdims=True) + jnp.sum(p_n[h], axis=0, keepdims=True)
        for h in heads:
            acc = (lax.dot_general(p_c[h].astype(BF16), head_rows(vc_ref, h), TN_DIMS, preferred_element_type=F32)
                   + lax.dot_general(p_n[h].astype(BF16), vn_ref[:, cols[h]], TN_DIMS, preferred_element_type=F32))
            l_col = jnp.transpose(jnp.broadcast_to(l[h], (HEAD_LANES, HEAD_LANES)))[:, 0:1]
            o = acc / l_col
            o = o[0:DEC_SEQ] - lam * o[DEC_SEQ:2 * DEC_SEQ]
            o_ref[:, cols[h]] = _subln(o, g_ref[...]).astype(BF16)


def _diff_sample(slopes, lamv, g, qd, kd16, vd16, cache_k, cache_v):
    batch = cache_k.shape[0]
    new_spec = pl.BlockSpec((DEC_SEQ, GROUP_WIDTH), lambda b: (b, 0))
    cache_spec = pl.BlockSpec((None, PAST_LEN * DIFF_HEADS, HEAD_LANES), lambda b: (b, 0, 0))
    return pl.pallas_call(
        _diff_sample_kernel,
        grid=(batch,),
        in_specs=[pl.BlockSpec(memory_space=pltpu.SMEM),
                  pl.BlockSpec(lamv.shape, lambda b: (0, 0)),
                  pl.BlockSpec(g.shape, lambda b: (0, 0)),
                  new_spec, new_spec, new_spec, cache_spec, cache_spec],
        out_specs=new_spec,
        out_shape=jax.ShapeDtypeStruct(qd.shape, BF16),
        compiler_params=pltpu.CompilerParams(dimension_semantics=("parallel",), vmem_limit_bytes=VMEM_LIMIT),
        name="diff_sample",
    )(slopes, lamv, g, qd, kd16, vd16, cache_k, cache_v)


def _sb_sample_kernel(q_ref, kn_ref, vn_ref, kct_ref, vct_ref, o_ref):
    q2 = _stack_queries(q_ref[...])
    n_blocks = PAST_LEN // CUM_BLOCK

    key_idx = lax.broadcasted_iota(jnp.int32, (HEAD_LANES, DEC_SEQ), 1)
    query_idx = lax.broadcasted_iota(jnp.int32, (HEAD_LANES, DEC_SEQ), 0) % HALF
    visible = key_idx < query_idx
    z = lax.dot_general(q2, kn_ref[...], NT_DIMS, preferred_element_type=F32)
    log2w, carry = _stick_block(z, _suffix_matrix(DEC_SEQ), visible)
    a = jnp.where(visible, jnp.exp2(log2w), 0.0)
    acc = jnp.dot(a.astype(BF16), vn_ref[...], preferred_element_type=F32)

    z = jnp.dot(q2, kct_ref[...].astype(BF16), preferred_element_type=F32)
    lk = _log2_keep(z)
    blocks = [lk[:, b * CUM_BLOCK:(b + 1) * CUM_BLOCK] for b in range(n_blocks)]
    stacked = jnp.concatenate(blocks, axis=0).astype(BF16)
    within = jnp.dot(stacked, _suffix_matrix(CUM_BLOCK), preferred_element_type=F32)
    afters = [None] * n_blocks
    for b in reversed(range(n_blocks)):
        afters[b] = within[b * HEAD_LANES:(b + 1) * HEAD_LANES] + carry
        carry = carry + jnp.sum(blocks[b], axis=-1, keepdims=True)
    a = jnp.exp2((z + lk) + jnp.concatenate(afters, axis=-1))
    acc = acc + lax.dot_general(a.astype(BF16), vct_ref[...].astype(BF16), NT_DIMS, preferred_element_type=F32)
    lane = lax.broadcasted_iota(jnp.int32, (DEC_SEQ, HEAD_LANES), 1)
    o_ref[...] = jnp.where(lane < HALF, acc[0:DEC_SEQ], acc[DEC_SEQ:2 * DEC_SEQ]).astype(BF16)


def _sb_sample(qs, ks16, vs16, cache_kt, cache_vt):
    batch = cache_kt.shape[0]
    new_spec = pl.BlockSpec((DEC_SEQ, HEAD_LANES), lambda b, p: (b, p))
    cache_spec = pl.BlockSpec((None, HEAD_LANES, PAST_LEN), lambda b, p: (b, p, 0))
    return pl.pallas_call(
        _sb_sample_kernel,
        grid=(batch, SB_PAIRS),
        in_specs=[new_spec, new_spec, new_spec, cache_spec, cache_spec],
        out_specs=new_spec,
        out_shape=jax.ShapeDtypeStruct(qs.shape, BF16),
        compiler_params=pltpu.CompilerParams(dimension_semantics=("parallel", "parallel"),
                                             vmem_limit_bytes=VMEM_LIMIT),
        name="sb_sample",
    )(qs, ks16, vs16, cache_kt, cache_vt)


def kernel(x_prompt, x_sample, cache_diff_k, cache_diff_v, cache_sb_k, cache_sb_v, norm_attn_g, w_in,
           lambda_q1, lambda_k1, lambda_q2, lambda_k2, diff_subln_g, w_out, norm_ffn_g, w_gate, w_up, w_down,
           norm_final_g):
    batch, seq, _ = x_prompt.shape
    dec_batch, dec_seq, _ = x_sample.shape
    assert seq == SEQ and dec_seq == DEC_SEQ and cache_diff_k.shape[2] == PAST_LEN and w_in.shape[0] == 1

    w_in16 = w_in[0].astype(BF16)
    w_sb_t16 = jnp.transpose(w_in16[:, 4 * GROUP_WIDTH:])
    w_out16 = w_out[0].astype(BF16)
    w_gate16 = w_gate[0].astype(BF16)
    w_up16 = w_up[0].astype(BF16)
    w_down16 = w_down[0].astype(BF16)
    g_attn = norm_attn_g[0].reshape(1, D_MODEL)
    g_ffn = norm_ffn_g[0].reshape(1, D_MODEL)
    g_final = norm_final_g.reshape(1, D_MODEL)
    g_subln = diff_subln_g[0].reshape(1, HEAD_LANES)
    lamv = jnp.concatenate([lambda_q1, lambda_k1, lambda_q2, lambda_k2], axis=0).astype(F32)
    slopes = jnp.exp2(-8.0 / DIFF_HEADS * jnp.arange(1, DIFF_HEADS + 1, dtype=F32))

    def ffn(x, od, osb):
        return _ffn(x, od, osb, w_out16, g_ffn, w_gate16, w_up16, w_down16, g_final)

    xp = x_prompt.reshape(batch * SEQ, D_MODEL)
    qd, qs, kd, vd, kst, vst, kd16, vd16, kst16, vst16 = _inproj(xp, g_attn, w_in16, w_sb_t16, True)
    od = _diff_prompt(slopes, lamv, g_subln, qd, kd16, vd16)
    osb = _sb_prompt(qs, kst16, vst16)
    y_prompt = ffn(xp, od, osb).reshape(batch, SEQ, D_MODEL)

    xs = x_sample.reshape(dec_batch * DEC_SEQ, D_MODEL)
    qd2, qs2, kd2, vd2, ks2, vs2, kd2_16, vd2_16, ks2_16, vs2_16 = _inproj(xs, g_attn, w_in16, w_sb_t16, False)
    head_major = lambda a: a[0].reshape(dec_batch, PAST_LEN * DIFF_HEADS, HEAD_LANES)
    od2 = _diff_sample(slopes, lamv, g_subln, qd2, kd2_16, vd2_16, head_major(cache_diff_k), head_major(cache_diff_v))
    keys_minor = lambda a: jnp.transpose(a[0], (0, 2, 3, 1)).reshape(dec_batch, GROUP_WIDTH, PAST_LEN)
    osb2 = _sb_sample(qs2, ks2_16, vs2_16, keys_minor(cache_sb_k), keys_minor(cache_sb_v))
    y_sample = ffn(xs, od2, osb2).reshape(dec_batch, DEC_SEQ, D_MODEL)

    diff_shape = lambda b, t: (1, b, t, DIFF_HEADS, HEAD_LANES)
    sb_shape = lambda b, t: (1, b, t, 2 * SB_PAIRS, HALF)
    from_keys_minor = lambda a: jnp.transpose(a.reshape(batch, 2 * SB_PAIRS, HALF, SEQ), (0, 3, 1, 2))[None]
    return (y_prompt, y_sample,
            kd.reshape(diff_shape(batch, SEQ)), vd.reshape(diff_shape(batch, SEQ)),
            from_keys_minor(kst), from_keys_minor(vst),
            kd2.reshape(diff_shape(dec_batch, DEC_SEQ)), vd2.reshape(diff_shape(dec_batch, DEC_SEQ)),
            ks2.reshape(sb_shape(dec_batch, DEC_SEQ)), vs2.reshape(sb_shape(dec_batch, DEC_SEQ)))
```

```python
import functools
import math

import jax
import jax.numpy as jnp
from jax import lax
from jax.experimental import pallas as pl
from jax.experimental.pallas import tpu as pltpu

D_MODEL = 1024
SEQ = 2048
DEC_SEQ = 64
PAST_LEN = 4096
CHUNK = 64
GROUP_WIDTH = 512
N_GROUPS = 6
DIFF_HEADS = 4
HEAD_LANES = 128
HALF = 64
SB_PAIRS = 4
D_FF = 2816
FF_CHUNK = 256
RMS_EPS = 1e-6
NEG_INF = -1e30
LOG2E = math.log2(math.e)
QK_SCALE = 0.125 * LOG2E
LAMBDA_INIT = 0.8 - 0.6 * math.exp(-0.3 * 0)

ROW_TILE = 512
TQ = 256
TK = TQ
CUM_BLOCK = 256
SB_BLOCK = CUM_BLOCK
SAMPLE_SWEEP = 512
EXIT_LOG2 = -160.0
STAT_LANES = 128
VMEM_LIMIT = 56 * 1024 * 1024

F32 = jnp.float32
BF16 = jnp.bfloat16
BF16_ROWS = 16
NT_DIMS = (((1,), (1,)), ((), ()))
TN_DIMS = (((0,), (0,)), ((), ()))


def _rms(x, g):
    return x * lax.rsqrt(jnp.mean(x * x, axis=-1, keepdims=True) + RMS_EPS) * g


def _lambda(lamv_ref):
    lv = lamv_ref[...]
    a = jnp.sum(lv[0:1] * lv[1:2], axis=-1, keepdims=True)
    b = jnp.sum(lv[2:3] * lv[3:4], axis=-1, keepdims=True)
    return jnp.exp(a) - jnp.exp(b) + LAMBDA_INIT


def _log2_keep(z2):
    nz = -z2
    return jnp.minimum(nz, 0.0) - jnp.log(1.0 + jnp.exp2(jnp.minimum(z2, nz))) * LOG2E


def _lanes(x, width):
    if width <= STAT_LANES:
        return x[:, :width]
    return jnp.concatenate([x] * (width // STAT_LANES), axis=1)


def _suffix_matrix(n):
    j = lax.broadcasted_iota(jnp.int32, (n, n), 0)
    s = lax.broadcasted_iota(jnp.int32, (n, n), 1)
    return jnp.where(j > s, 1.0, 0.0).astype(BF16)


def _stick_block(z, suffix, visible):
    lk = _log2_keep(z)
    if visible is not None:
        lk = jnp.where(visible, lk, 0.0)
    within = jnp.dot(lk.astype(BF16), suffix, preferred_element_type=F32)
    return (z + lk) + within, jnp.sum(lk, axis=-1, keepdims=True)


def _stick_weights(log2w, carry, visible):
    a = jnp.exp2(log2w + _lanes(carry, log2w.shape[-1]))
    return a if visible is None else jnp.where(visible, a, 0.0)


def _inproj_kernel(transpose_sb, x_ref, g_ref, w_ref, wt_ref, qd_ref, qs_ref, kd_ref, vd_ref, ks_ref, vs_ref,
                   kd16_ref, vd16_ref, ks16_ref, vs16_ref):
    n = _rms(x_ref[...], g_ref[...]).astype(BF16)

    def proj(c):
        return jnp.dot(n, w_ref[:, c * GROUP_WIDTH:(c + 1) * GROUP_WIDTH], preferred_element_type=F32)

    qd_ref[...] = (proj(0) * QK_SCALE).astype(BF16)
    qs_ref[...] = (proj(3) * QK_SCALE).astype(BF16)
    for c, out_ref, out16_ref in ((1, kd_ref, kd16_ref), (2, vd_ref, vd16_ref)):
        r = proj(c)
        out16_ref[...] = r.astype(BF16)
        for h in range(DIFF_HEADS):
            out_ref[pl.ds(h, ROW_TILE, stride=DIFF_HEADS), :] = r[:, h * HEAD_LANES:(h + 1) * HEAD_LANES]
    for c, out_ref, out16_ref in ((4, ks_ref, ks16_ref), (5, vs_ref, vs16_ref)):
        if transpose_sb:
            rows = slice((c - 4) * GROUP_WIDTH, (c - 3) * GROUP_WIDTH)
            r = lax.dot_general(wt_ref[rows, :], n, NT_DIMS, preferred_element_type=F32)
            out_ref[...] = r
            for t in range(ROW_TILE // SB_BLOCK):
                out16_ref[t] = r[:, t * SB_BLOCK:(t + 1) * SB_BLOCK].astype(BF16)
        else:
            r = proj(c)
            out_ref[...] = r
            out16_ref[...] = r.astype(BF16)


def _inproj(x, g, w16, wt16, transpose_sb):
    rows = x.shape[0]
    steps = rows // ROW_TILE
    row_spec = lambda width: pl.BlockSpec((ROW_TILE, width), lambda i: (i, 0))
    head_major = pl.BlockSpec((ROW_TILE * DIFF_HEADS, HEAD_LANES), lambda i: (i, 0))
    if transpose_sb:
        blocks_per_seq = SEQ // ROW_TILE
        batch = rows // SEQ
        sb32_spec = pl.BlockSpec((None, GROUP_WIDTH, ROW_TILE), lambda i: (i // blocks_per_seq, 0, i % blocks_per_seq))
        sb16_spec = pl.BlockSpec((None, ROW_TILE // SB_BLOCK, GROUP_WIDTH, SB_BLOCK),
                                 lambda i: (i // blocks_per_seq, i % blocks_per_seq, 0, 0))
        sb32_shape = jax.ShapeDtypeStruct((batch, GROUP_WIDTH, SEQ), F32)
        sb16_shape = jax.ShapeDtypeStruct((batch, SEQ // SB_BLOCK, GROUP_WIDTH, SB_BLOCK), BF16)
    else:
        sb32_spec = sb16_spec = row_spec(GROUP_WIDTH)
        sb32_shape = jax.ShapeDtypeStruct((rows, GROUP_WIDTH), F32)
        sb16_shape = jax.ShapeDtypeStruct((rows, GROUP_WIDTH), BF16)
    bf16_rows = jax.ShapeDtypeStruct((rows, GROUP_WIDTH), BF16)
    f32_head_major = jax.ShapeDtypeStruct((rows * DIFF_HEADS, HEAD_LANES), F32)
    return pl.pallas_call(
        functools.partial(_inproj_kernel, transpose_sb),
        grid=(steps,),
        in_specs=[row_spec(D_MODEL),
                  pl.BlockSpec((1, D_MODEL), lambda i: (0, 0)),
                  pl.BlockSpec(w16.shape, lambda i: (0, 0)),
                  pl.BlockSpec(wt16.shape, lambda i: (0, 0))],
        out_specs=[row_spec(GROUP_WIDTH), row_spec(GROUP_WIDTH), head_major, head_major, sb32_spec, sb32_spec,
                   row_spec(GROUP_WIDTH), row_spec(GROUP_WIDTH), sb16_spec, sb16_spec],
        out_shape=[bf16_rows, bf16_rows, f32_head_major, f32_head_major, sb32_shape, sb32_shape,
                   bf16_rows, bf16_rows, sb16_shape, sb16_shape],
        compiler_params=pltpu.CompilerParams(dimension_semantics=("parallel",), vmem_limit_bytes=VMEM_LIMIT),
        name="inproj",
    )(x, g, w16, wt16)


def _ffn_kernel(x_ref, od_ref, osb_ref, wout_ref, gffn_ref, wg_ref, wu_ref, wd_ref, gfin_ref, y_ref, act_ref):
    mixed = jnp.concatenate([od_ref[...], osb_ref[...]], axis=-1)
    h = x_ref[...] + jnp.dot(mixed, wout_ref[...], preferred_element_type=F32)
    n = _rms(h, gffn_ref[...]).astype(BF16)
    for c in range(D_FF // FF_CHUNK):
        cols = slice(c * FF_CHUNK, (c + 1) * FF_CHUNK)
        gate = jnp.dot(n, wg_ref[:, cols], preferred_element_type=F32)
        up = jnp.dot(n, wu_ref[:, cols], preferred_element_type=F32)
        act_ref[:, cols] = (gate / (1.0 + jnp.exp(-gate)) * up).astype(BF16)
    y = h + jnp.dot(act_ref[...], wd_ref[...], preferred_element_type=F32)
    y_ref[...] = _rms(y, gfin_ref[...])


def _ffn(x, od, osb, wout16, gffn, wg16, wu16, wd16, gfin):
    rows = x.shape[0]
    row_spec = lambda width: pl.BlockSpec((ROW_TILE, width), lambda i: (i, 0))
    full = lambda a: pl.BlockSpec(a.shape, lambda i: (0, 0))
    return pl.pallas_call(
        _ffn_kernel,
        grid=(rows // ROW_TILE,),
        in_specs=[row_spec(D_MODEL), row_spec(GROUP_WIDTH), row_spec(GROUP_WIDTH),
                  full(wout16), full(gffn), full(wg16), full(wu16), full(wd16), full(gfin)],
        out_specs=row_spec(D_MODEL),
        out_shape=jax.ShapeDtypeStruct((rows, D_MODEL), F32),
        scratch_shapes=[pltpu.VMEM((ROW_TILE, D_FF), BF16)],
        compiler_params=pltpu.CompilerParams(dimension_semantics=("parallel",), vmem_limit_bytes=VMEM_LIMIT),
        name="outproj_ffn",
    )(x, od, osb, wout16, gffn, wg16, wu16, wd16, gfin)


def _subln(o, g):
    o = o * lax.rsqrt(jnp.mean(o * o, axis=-1, keepdims=True) + RMS_EPS)
    return o * g * (1.0 - LAMBDA_INIT)


def _diff_prompt_kernel(slope_ref, lamv_ref, g_ref, q_ref, k_ref, v_ref, o_ref, bias_ref):
    slope2 = slope_ref[pl.program_id(1)] * LOG2E
    r = lax.broadcasted_iota(jnp.int32, (TQ, TK), 0)
    c = lax.broadcasted_iota(jnp.int32, (TQ, TK), 1)
    bias_ref[...] = jnp.where((c // CHUNK) <= (r // CHUNK),
                              jnp.where(c > r, (-2.0 * slope2) * (c - r).astype(F32), 0.0), NEG_INF)
    lam = _lambda(lamv_ref)
    lane = lax.broadcasted_iota(jnp.int32, (TQ, HEAD_LANES), 1)
    key_bias = slope2 * lax.broadcasted_iota(jnp.int32, (BF16_ROWS, SEQ), 1).astype(F32)
    row = lax.broadcasted_iota(jnp.int32, (BF16_ROWS, SEQ), 0)
    hi = key_bias.astype(BF16).astype(F32)
    mid = (key_bias - hi).astype(BF16).astype(F32)
    lo = key_bias - hi - mid
    split = jnp.where(row == 0, hi, jnp.where(row == 1, mid, jnp.where(row == 2, lo, 0.0))).astype(BF16)
    kt = jnp.concatenate([jnp.transpose(k_ref[...]), split,
                          jnp.zeros((HEAD_LANES - BF16_ROWS, SEQ), BF16)], axis=0)
    ones = jnp.where(lane < 3, 1.0, 0.0).astype(BF16)

    def scores(i):
        q = q_ref[i * TQ:(i + 1) * TQ, :]
        qms = [jnp.concatenate([jnp.where((lane < HALF) == first, q, jnp.zeros_like(q)), ones], axis=-1)
               for first in (True, False)]
        raw = [jnp.dot(qm, kt[:, :TK * (i + 1)], preferred_element_type=F32) for qm in qms]
        return [jnp.concatenate([s[:, :TK * i], s[:, TK * i:] + bias_ref[...]], axis=-1) if i else
                s + bias_ref[...] for s in raw]

    def attend(i, s):
        e = [jnp.exp2(sm - jnp.max(sm, axis=-1, keepdims=True)) for sm in s]
        l = [jnp.sum(em, axis=-1, keepdims=True) for em in e]
        a = (e[0] - (lam * l[0] / l[1]) * e[1]).astype(BF16)
        o = jnp.dot(a, v_ref[0:TK * (i + 1), :], preferred_element_type=F32) / l[0]
        o_ref[i * TQ:(i + 1) * TQ, :] = _subln(o, g_ref[...]).astype(BF16)

    n_q = SEQ // TQ
    s_next = scores(0)
    for i in range(n_q):
        s_now = s_next
        if i + 1 < n_q:
            s_next = scores(i + 1)
        attend(i, s_now)


def _diff_prompt(slopes, lamv, g, qd, kd16, vd16):
    batch = qd.shape[0] // SEQ
    head_spec = pl.BlockSpec((SEQ, HEAD_LANES), lambda b, h: (b, h))
    return pl.pallas_call(
        _diff_prompt_kernel,
        grid=(batch, DIFF_HEADS),
        in_specs=[pl.BlockSpec(memory_space=pltpu.SMEM),
                  pl.BlockSpec(lamv.shape, lambda b, h: (0, 0)),
                  pl.BlockSpec(g.shape, lambda b, h: (0, 0)),
                  head_spec, head_spec, head_spec],
        out_specs=head_spec,
        out_shape=jax.ShapeDtypeStruct(qd.shape, BF16),
        scratch_shapes=[pltpu.VMEM((TQ, TK), F32)],
        compiler_params=pltpu.CompilerParams(dimension_semantics=("parallel", "parallel"),
                                             vmem_limit_bytes=VMEM_LIMIT),
        name="diff_prompt",
    )(slopes, lamv, g, qd, kd16, vd16)


def _sb_prompt_kernel(q_ref, kt_ref, vt_ref, o_ref, u_ref, carry_ref, acc_ref):
    u_ref[...] = _suffix_matrix(SB_BLOCK)
    causal = (lax.broadcasted_iota(jnp.int32, (SB_BLOCK, SB_BLOCK), 1)
              < lax.broadcasted_iota(jnp.int32, (SB_BLOCK, SB_BLOCK), 0))
    lane = lax.broadcasted_iota(jnp.int32, (SB_BLOCK, HEAD_LANES), 1)

    def rows_of(i):
        return pl.ds(pl.multiple_of(i * SB_BLOCK, SB_BLOCK), SB_BLOCK)

    def begin(slot, i):
        q = q_ref[rows_of(i), :]
        carry_ref[slot] = jnp.zeros(carry_ref.shape[1:], F32)
        acc_ref[slot] = jnp.zeros(acc_ref.shape[1:], F32)
        return jnp.where(lane < HALF, q, jnp.zeros_like(q)), jnp.where(lane >= HALF, q, jnp.zeros_like(q))

    def add_blocks(items, diagonal):
        chains = [(n, h, t) for n, item in enumerate(items) for h in range(2) for t in reversed(range(item[3]))]
        mask = {c: causal if diagonal and c[2] == items[c[0]][3] - 1 else None for c in chains}
        z = {(n, h, t): jnp.dot(items[n][1][h], kt_ref[items[n][2] + t], preferred_element_type=F32)
             for n, h, t in chains}
        part = {c: _stick_block(z[c], u_ref[...], mask[c]) for c in chains}
        carry = {(n, h): carry_ref[item[0], h] for n, item in enumerate(items) for h in range(2)}
        weights = {}
        for n, h, t in chains:
            weights[n, h, t] = _stick_weights(part[n, h, t][0], carry[n, h], mask[n, h, t]).astype(BF16)
            carry[n, h] = carry[n, h] + part[n, h, t][1]
        for n, (slot, _, first, count) in enumerate(items):
            for h in range(2):
                acc = acc_ref[slot, h]
                for t in range(count):
                    acc = acc + lax.dot_general(weights[n, h, t], vt_ref[first + t], NT_DIMS,
                                                preferred_element_type=F32)
                acc_ref[slot, h] = acc
                carry_ref[slot, h] = carry[n, h]

    def finish(slot, qms, i, j):
        def unfinished(state):
            j, top = state
            return jnp.logical_and(j >= 0, top > EXIT_LOG2)

        def earlier(state):
            j, _ = state
            add_blocks([(slot, qms, j, 1)], False)
            return j - 1, jnp.max(carry_ref[slot])

        lax.while_loop(unfinished, earlier, (j, jnp.max(carry_ref[slot])))
        o_ref[rows_of(i), :] = jnp.where(lane < HALF, acc_ref[slot, 0], acc_ref[slot, 1]).astype(BF16)

    def query_pair(i_a, count_a, i_b):
        qms_a, qms_b = begin(0, i_a), begin(1, i_b)
        first_a = i_a - (count_a - 1)
        add_blocks([(0, qms_a, first_a, count_a), (1, qms_b, i_b - 1, 2)], True)
        finish(0, qms_a, i_a, first_a - 1)
        finish(1, qms_b, i_b, i_b - 2)

    n_q = SEQ // SB_BLOCK
    query_pair(0, 1, n_q - 1)

    def middle(m, carry):
        query_pair(2 * m + 1, 2, 2 * m + 2)
        return carry

    lax.fori_loop(0, (n_q - 2) // 2, middle, 0)


def _sb_prompt(qs, kst16, vst16):
    batch = qs.shape[0] // SEQ
    pair_spec = pl.BlockSpec((SEQ, HEAD_LANES), lambda b, p: (b, p))
    kv_spec = pl.BlockSpec((None, SEQ // SB_BLOCK, HEAD_LANES, SB_BLOCK), lambda b, p: (b, 0, p, 0))
    return pl.pallas_call(
        _sb_prompt_kernel,
        grid=(batch, SB_PAIRS),
        in_specs=[pair_spec, kv_spec, kv_spec],
        out_specs=pair_spec,
        out_shape=jax.ShapeDtypeStruct(qs.shape, BF16),
        scratch_shapes=[pltpu.VMEM((SB_BLOCK, SB_BLOCK), BF16),pltpu.VMEM((2, 2, SB_BLOCK, STAT_LANES), F32),
                        pltpu.VMEM((2, 2, SB_BLOCK, HEAD_LANES), F32)],
        compiler_params=pltpu.CompilerParams(dimension_semantics=("parallel", "parallel"),
                                             vmem_limit_bytes=VMEM_LIMIT),
        name="sb_prompt",
    )(qs, kst16, vst16)


def _stack_queries(q):
    lane = lax.broadcasted_iota(jnp.int32, q.shape, 1)
    zero = jnp.zeros_like(q)
    return jnp.concatenate([jnp.where(lane < HALF, q, zero), jnp.where(lane >= HALF, q, zero)], axis=0)


def _diff_sample_kernel(slope_ref, lamv_ref, g_ref, q_ref, kn_ref, vn_ref, kc_ref, vc_ref, o_ref):
    lam = _lambda(lamv_ref)
    kpos = lax.broadcasted_iota(jnp.int32, (PAST_LEN, HEAD_LANES), 0)
    qpos = PAST_LEN + (lax.broadcasted_iota(jnp.int32, (PAST_LEN, HEAD_LANES), 1) % HALF)
    dist_c = (qpos - kpos).astype(F32)
    kn_pos = lax.broadcasted_iota(jnp.int32, (DEC_SEQ, HEAD_LANES), 0)
    qn_pos = lax.broadcasted_iota(jnp.int32, (DEC_SEQ, HEAD_LANES), 1) % HALF
    dist_n = jnp.abs(qn_pos - kn_pos).astype(F32)
    cols = [slice(h * HEAD_LANES, (h + 1) * HEAD_LANES) for h in range(DIFF_HEADS)]
    head_rows = lambda ref, h: ref[pl.ds(h, PAST_LEN, stride=DIFF_HEADS), :].astype(BF16)
    for heads in ((0, 1), (2, 3)):
        s_c, s_n = {}, {}
        for h in heads:
            slope2 = slope_ref[h] * LOG2E
            q2 = _stack_queries(q_ref[:, cols[h]])
            s_c[h] = lax.dot_general(head_rows(kc_ref, h), q2, NT_DIMS, preferred_element_type=F32) - slope2 * dist_c
            s_n[h] = lax.dot_general(kn_ref[:, cols[h]], q2, NT_DIMS, preferred_element_type=F32) - slope2 * dist_n
        p_c, p_n, l = {}, {}, {}
        for h in heads:
            m = jnp.maximum(jnp.max(s_c[h], axis=0, keepdims=True), jnp.max(s_n[h], axis=0, keepdims=True))
            p_c[h] = jnp.exp2(s_c[h] - m)
            p_n[h] = jnp.exp2(s_n[h] - m)
            l[h] = jnp.sum(p_c[h], axis=0, keepdims=True) + jnp.sum(p_n[h], axis=0, keepdims=True)
        for h in heads:
            acc = (lax.dot_general(p_c[h].astype(BF16), head_rows(vc_ref, h), TN_DIMS, preferred_element_type=F32)
                   + lax.dot_general(p_n[h].astype(BF16), vn_ref[:, cols[h]], TN_DIMS, preferred_element_type=F32))
            l_col = jnp.transpose(jnp.broadcast_to(l[h], (HEAD_LANES, HEAD_LANES)))[:, 0:1]
            o = acc / l_col
            o = o[0:DEC_SEQ] - lam * o[DEC_SEQ:2 * DEC_SEQ]
            o_ref[:, cols[h]] = _subln(o, g_ref[...]).astype(BF16)


def _diff_sample(slopes, lamv, g, qd, kd16, vd16, cache_k, cache_v):
    batch = cache_k.shape[0]
    new_spec = pl.BlockSpec((DEC_SEQ, GROUP_WIDTH), lambda b: (b, 0))
    cache_spec = pl.BlockSpec((None, PAST_LEN * DIFF_HEADS, HEAD_LANES), lambda b: (b, 0, 0))
    return pl.pallas_call(
        _diff_sample_kernel,
        grid=(batch,),
        in_specs=[pl.BlockSpec(memory_space=pltpu.SMEM),
                  pl.BlockSpec(lamv.shape, lambda b: (0, 0)),
                  pl.BlockSpec(g.shape, lambda b: (0, 0)),
                  new_spec, new_spec, new_spec, cache_spec, cache_spec],
        out_specs=new_spec,
        out_shape=jax.ShapeDtypeStruct(qd.shape, BF16),
        compiler_params=pltpu.CompilerParams(dimension_semantics=("parallel",), vmem_limit_bytes=VMEM_LIMIT),
        name="diff_sample",
    )(slopes, lamv, g, qd, kd16, vd16, cache_k, cache_v)


def _sb_sample_kernel(q_ref, kn_ref, vn_ref, kct_ref, vct_ref, o_ref, carry_ref, acc_ref):
    q2 = _stack_queries(q_ref[...])

    key_idx = lax.broadcasted_iota(jnp.int32, (HEAD_LANES, DEC_SEQ), 1)
    query_idx = lax.broadcasted_iota(jnp.int32, (HEAD_LANES, DEC_SEQ), 0) % HALF
    visible = key_idx < query_idx
    z = lax.dot_general(q2, kn_ref[...], NT_DIMS, preferred_element_type=F32)
    log2w, carry = _stick_block(z, _suffix_matrix(DEC_SEQ), visible)
    a = jnp.where(visible, jnp.exp2(log2w), 0.0)
    acc_ref[...] = jnp.dot(a.astype(BF16), vn_ref[...], preferred_element_type=F32)
    carry_ref[...] = jnp.broadcast_to(carry, carry_ref.shape)
    suffix = _suffix_matrix(CUM_BLOCK)
    n_blocks = SAMPLE_SWEEP // CUM_BLOCK

    def unfinished(state):
        j, top = state
        return jnp.logical_and(j >= 0, top > EXIT_LOG2)

    def sweep(state):
        j, _ = state
        cols = pl.ds(pl.multiple_of(j * SAMPLE_SWEEP, SAMPLE_SWEEP), SAMPLE_SWEEP)
        z = jnp.dot(q2, kct_ref[:, cols].astype(BF16), preferred_element_type=F32)
        parts = [_stick_block(z[:, b * CUM_BLOCK:(b + 1) * CUM_BLOCK], suffix, None) for b in range(n_blocks)]
        carry = carry_ref[...]
        weights = [None] * n_blocks
        for b in reversed(range(n_blocks)):
            weights[b] = _stick_weights(parts[b][0], carry, None)
            carry = carry + parts[b][1]
        a = jnp.concatenate(weights, axis=-1).astype(BF16)
        acc_ref[...] += lax.dot_general(a, vct_ref[:, cols].astype(BF16), NT_DIMS, preferred_element_type=F32)
        carry_ref[...] = carry
        return j - 1, jnp.max(carry)

    lax.while_loop(unfinished, sweep, (PAST_LEN // SAMPLE_SWEEP - 1, jnp.max(carry)))
    lane = lax.broadcasted_iota(jnp.int32, (DEC_SEQ, HEAD_LANES), 1)
    o_ref[...] = jnp.where(lane < HALF, acc_ref[0:DEC_SEQ, :], acc_ref[DEC_SEQ:2 * DEC_SEQ, :]).astype(BF16)


def _sb_sample(qs, ks16, vs16, cache_kt, cache_vt):
    batch = cache_kt.shape[0]
    new_spec = pl.BlockSpec((DEC_SEQ, HEAD_LANES), lambda b, p: (b, p))
    cache_spec = pl.BlockSpec((None, HEAD_LANES, PAST_LEN), lambda b, p: (b, p, 0))
    return pl.pallas_call(
        _sb_sample_kernel,
        grid=(batch, SB_PAIRS),
        in_specs=[new_spec, new_spec, new_spec, cache_spec, cache_spec],
        out_specs=new_spec,
        out_shape=jax.ShapeDtypeStruct(qs.shape, BF16),
        scratch_shapes=[pltpu.VMEM((HEAD_LANES, STAT_LANES), F32), pltpu.VMEM((HEAD_LANES, HEAD_LANES), F32)],
        compiler_params=pltpu.CompilerParams(dimension_semantics=("parallel", "parallel"),
                                             vmem_limit_bytes=VMEM_LIMIT),
        name="sb_sample",
    )(qs, ks16, vs16, cache_kt, cache_vt)


def kernel(x_prompt, x_sample, cache_diff_k, cache_diff_v, cache_sb_k, cache_sb_v, norm_attn_g, w_in,
           lambda_q1, lambda_k1, lambda_q2, lambda_k2, diff_subln_g, w_out, norm_ffn_g, w_gate, w_up, w_down,
           norm_final_g):
    batch, seq, _ = x_prompt.shape
    dec_batch, dec_seq, _ = x_sample.shape
    assert seq == SEQ and dec_seq == DEC_SEQ and cache_diff_k.shape[2] == PAST_LEN and w_in.shape[0] == 1

    w_in16 = w_in[0].astype(BF16)
    w_sb_t16 = jnp.transpose(w_in16[:, 4 * GROUP_WIDTH:])
    w_out16 = w_out[0].astype(BF16)
    w_gate16 = w_gate[0].astype(BF16)
    w_up16 = w_up[0].astype(BF16)
    w_down16 = w_down[0].astype(BF16)
    g_attn = norm_attn_g[0].reshape(1, D_MODEL)
    g_ffn = norm_ffn_g[0].reshape(1, D_MODEL)
    g_final = norm_final_g.reshape(1, D_MODEL)
    g_subln = diff_subln_g[0].reshape(1, HEAD_LANES)
    lamv = jnp.concatenate([lambda_q1, lambda_k1, lambda_q2, lambda_k2], axis=0).astype(F32)
    slopes = jnp.exp2(-8.0 / DIFF_HEADS * jnp.arange(1, DIFF_HEADS + 1, dtype=F32))

    def ffn(x, od, osb):
        return _ffn(x, od, osb, w_out16, g_ffn, w_gate16, w_up16, w_down16, g_final)

    xp = x_prompt.reshape(batch * SEQ, D_MODEL)
    qd, qs, kd, vd, kst, vst, kd16, vd16, kst16, vst16 = _inproj(xp, g_attn, w_in16, w_sb_t16, True)
    od = _diff_prompt(slopes, lamv, g_subln, qd, kd16, vd16)
    osb = _sb_prompt(qs, kst16, vst16)
    y_prompt = ffn(xp, od, osb).reshape(batch, SEQ, D_MODEL)

    xs = x_sample.reshape(dec_batch * DEC_SEQ, D_MODEL)
    qd2, qs2, kd2, vd2, ks2, vs2, kd2_16, vd2_16, ks2_16, vs2_16 = _inproj(xs, g_attn, w_in16, w_sb_t16, False)
    head_major = lambda a: a[0].reshape(dec_batch, PAST_LEN * DIFF_HEADS, HEAD_LANES)
    od2 = _diff_sample(slopes, lamv, g_subln, qd2, kd2_16, vd2_16, head_major(cache_diff_k), head_major(cache_diff_v))
    keys_minor = lambda a: jnp.transpose(a[0], (0, 2, 3, 1)).reshape(dec_batch, GROUP_WIDTH, PAST_LEN)
    osb2 = _sb_sample(qs2, ks2_16, vs2_16, keys_minor(cache_sb_k), keys_minor(cache_sb_v))
    y_sample = ffn(xs, od2, osb2).reshape(dec_batch, DEC_SEQ, D_MODEL)

    diff_shape = lambda b, t: (1, b, t, DIFF_HEADS, HEAD_LANES)
    sb_shape = lambda b, t: (1, b, t, 2 * SB_PAIRS, HALF)
    from_keys_minor = lambda a: jnp.transpose(a.reshape(batch, 2 * SB_PAIRS, HALF, SEQ), (0, 3, 1, 2))[None]
    return (y_prompt, y_sample,
            kd.reshape(diff_shape(batch, SEQ)), vd.reshape(diff_shape(batch, SEQ)),
            from_keys_minor(kst), from_keys_minor(vst),
            kd2.reshape(diff_shape(dec_batch, DEC_SEQ)), vd2.reshape(diff_shape(dec_batch, DEC_SEQ)),
            ks2.reshape(sb_shape(dec_batch, DEC_SEQ)), vs2.reshape(sb_shape(dec_batch, DEC_SEQ)))
```

```python
import functools
import math

import jax
import jax.numpy as jnp
from jax import lax
from jax.experimental import pallas as pl
from jax.experimental.pallas import tpu as pltpu

D_MODEL = 1024
SEQ = 2048
DEC_SEQ = 64
PAST_LEN = 4096
CHUNK = 64
GROUP_WIDTH = 512
N_GROUPS = 6
DIFF_HEADS = 4
HEAD_LANES = 128
HALF = 64
SB_PAIRS = 4
D_FF = 2816
FF_CHUNK = 256
RMS_EPS = 1e-6
NEG_INF = -1e30
LOG2E = math.log2(math.e)
QK_SCALE = 0.125 * LOG2E
LAMBDA_INIT = 0.8 - 0.6 * math.exp(-0.3 * 0)

ROW_TILE = 512
TQ = 256
TK = TQ
CUM_BLOCK = 256
SB_BLOCK = CUM_BLOCK
SAMPLE_SWEEP = 512
EXIT_LOG2 = -160.0
STAT_LANES = 128
VMEM_LIMIT = 56 * 1024 * 1024
FUSED_VMEM_LIMIT = 58 * 1024 * 1024

F32 = jnp.float32
BF16 = jnp.bfloat16
BF16_ROWS = 16
NT_DIMS = (((1,), (1,)), ((), ()))
TN_DIMS = (((0,), (0,)), ((), ()))


def _rms(x, g):
    return x * lax.rsqrt(jnp.mean(x * x, axis=-1, keepdims=True) + RMS_EPS) * g


def _lambda(lamv_ref):
    lv = lamv_ref[...]
    a = jnp.sum(lv[0:1] * lv[1:2], axis=-1, keepdims=True)
    b = jnp.sum(lv[2:3] * lv[3:4], axis=-1, keepdims=True)
    return jnp.exp(a) - jnp.exp(b) + LAMBDA_INIT


def _log2_keep(z2):
    nz = -z2
    return jnp.minimum(nz, 0.0) - jnp.log(1.0 + jnp.exp2(jnp.minimum(z2, nz))) * LOG2E


def _lanes(x, width):
    if width <= STAT_LANES:
        return x[:, :width]
    return jnp.concatenate([x] * (width // STAT_LANES), axis=1)


def _suffix_matrix(n):
    j = lax.broadcasted_iota(jnp.int32, (n, n), 0)
    s = lax.broadcasted_iota(jnp.int32, (n, n), 1)
    return jnp.where(j > s, 1.0, 0.0).astype(BF16)


def _stick_block(z, suffix, visible):
    lk = _log2_keep(z)
    if visible is not None:
        lk = jnp.where(visible, lk, 0.0)
    within = jnp.dot(lk.astype(BF16), suffix, preferred_element_type=F32)
    return (z + lk) + within, jnp.sum(lk, axis=-1, keepdims=True)


def _stick_weights(log2w, carry, visible):
    a = jnp.exp2(log2w + _lanes(carry, log2w.shape[-1]))
    return a if visible is None else jnp.where(visible, a, 0.0)


def _inproj_kernel(transpose_sb, x_ref, g_ref, w_ref, wt_ref, qd_ref, qs_ref, kd_ref, vd_ref, ks_ref, vs_ref,
                   kd16_ref, vd16_ref, ks16_ref, vs16_ref):
    n = _rms(x_ref[...], g_ref[...]).astype(BF16)

    def proj(c):
        return jnp.dot(n, w_ref[:, c * GROUP_WIDTH:(c + 1) * GROUP_WIDTH], preferred_element_type=F32)

    qd_ref[...] = (proj(0) * QK_SCALE).astype(BF16)
    qs_ref[...] = (proj(3) * QK_SCALE).astype(BF16)
    for c, out_ref, out16_ref in ((1, kd_ref, kd16_ref), (2, vd_ref, vd16_ref)):
        r = proj(c)
        out16_ref[...] = r.astype(BF16)
        for h in range(DIFF_HEADS):
            out_ref[pl.ds(h, ROW_TILE, stride=DIFF_HEADS), :] = r[:, h * HEAD_LANES:(h + 1) * HEAD_LANES]
    for c, out_ref, out16_ref in ((4, ks_ref, ks16_ref), (5, vs_ref, vs16_ref)):
        if transpose_sb:
            rows = slice((c - 4) * GROUP_WIDTH, (c - 3) * GROUP_WIDTH)
            r = lax.dot_general(wt_ref[rows, :], n, NT_DIMS, preferred_element_type=F32)
            out_ref[...] = r
            for t in range(ROW_TILE // SB_BLOCK):
                out16_ref[t] = r[:, t * SB_BLOCK:(t + 1) * SB_BLOCK].astype(BF16)
        else:
            r = proj(c)
            out_ref[...] = r
            out16_ref[...] = r.astype(BF16)


def _inproj(x, g, w16, wt16, transpose_sb):
    rows = x.shape[0]
    steps = rows // ROW_TILE
    row_spec = lambda width: pl.BlockSpec((ROW_TILE, width), lambda i: (i, 0))
    head_major = pl.BlockSpec((ROW_TILE * DIFF_HEADS, HEAD_LANES), lambda i: (i, 0))
    if transpose_sb:
        blocks_per_seq = SEQ // ROW_TILE
        batch = rows // SEQ
        sb32_spec = pl.BlockSpec((None, GROUP_WIDTH, ROW_TILE), lambda i: (i // blocks_per_seq, 0, i % blocks_per_seq))
        sb16_spec = pl.BlockSpec((None, ROW_TILE // SB_BLOCK, GROUP_WIDTH, SB_BLOCK),
                                 lambda i: (i // blocks_per_seq, i % blocks_per_seq, 0, 0))
        sb32_shape = jax.ShapeDtypeStruct((batch, GROUP_WIDTH, SEQ), F32)
        sb16_shape = jax.ShapeDtypeStruct((batch, SEQ // SB_BLOCK, GROUP_WIDTH, SB_BLOCK), BF16)
    else:
        sb32_spec = sb16_spec = row_spec(GROUP_WIDTH)
        sb32_shape = jax.ShapeDtypeStruct((rows, GROUP_WIDTH), F32)
        sb16_shape = jax.ShapeDtypeStruct((rows, GROUP_WIDTH), BF16)
    bf16_rows = jax.ShapeDtypeStruct((rows, GROUP_WIDTH), BF16)
    f32_head_major = jax.ShapeDtypeStruct((rows * DIFF_HEADS, HEAD_LANES), F32)
    return pl.pallas_call(
        functools.partial(_inproj_kernel, transpose_sb),
        grid=(steps,),
        in_specs=[row_spec(D_MODEL),
                  pl.BlockSpec((1, D_MODEL), lambda i: (0, 0)),
                  pl.BlockSpec(w16.shape, lambda i: (0, 0)),
                  pl.BlockSpec(wt16.shape, lambda i: (0, 0))],
        out_specs=[row_spec(GROUP_WIDTH), row_spec(GROUP_WIDTH), head_major, head_major, sb32_spec, sb32_spec,
                   row_spec(GROUP_WIDTH), row_spec(GROUP_WIDTH), sb16_spec, sb16_spec],
        out_shape=[bf16_rows, bf16_rows, f32_head_major, f32_head_major, sb32_shape, sb32_shape,
                   bf16_rows, bf16_rows, sb16_shape, sb16_shape],
        compiler_params=pltpu.CompilerParams(dimension_semantics=("parallel",), vmem_limit_bytes=VMEM_LIMIT),
        name="inproj",
    )(x, g, w16, wt16)


def _ffn_pieces(n_pieces, x_ref, od, osb_ref, wout_ref, gffn_ref, wg_ref, wu_ref, wd_ref, gfin_ref, y_ref, act_ref):
    state = {}
    n_chunks = D_FF // FF_CHUNK

    def first():
        mixed = jnp.concatenate([od(), osb_ref[...]], axis=-1)
        state["h"] = x_ref[...] + jnp.dot(mixed, wout_ref[...], preferred_element_type=F32)
        state["n"] = _rms(state["h"], gffn_ref[...]).astype(BF16)

    def chunks(lo, hi):
        def run():
            for c in range(lo, hi):
                cols = slice(c * FF_CHUNK, (c + 1) * FF_CHUNK)
                gate = jnp.dot(state["n"], wg_ref[:, cols], preferred_element_type=F32)
                up = jnp.dot(state["n"], wu_ref[:, cols], preferred_element_type=F32)
                act_ref[:, cols] = (gate / (1.0 + jnp.exp(-gate)) * up).astype(BF16)
        return run

    def last():
        y = state["h"] + jnp.dot(act_ref[...], wd_ref[...], preferred_element_type=F32)
        y_ref[...] = _rms(y, gfin_ref[...])

    middle = n_pieces - 2
    bounds = [(n_chunks * m) // middle for m in range(middle + 1)]
    return [first] + [chunks(bounds[m], bounds[m + 1]) for m in range(middle)] + [last]


def _ffn_kernel(x_ref, od_ref, osb_ref, wout_ref, gffn_ref, wg_ref, wu_ref, wd_ref, gfin_ref, y_ref, act_ref):
    for piece in _ffn_pieces(3, x_ref, lambda: od_ref[...], osb_ref, wout_ref, gffn_ref, wg_ref, wu_ref, wd_ref,
                             gfin_ref, y_ref, act_ref):
        piece()


def _ffn(x, od, osb, wout16, gffn, wg16, wu16, wd16, gfin):
    rows = x.shape[0]
    row_spec = lambda width: pl.BlockSpec((ROW_TILE, width), lambda i: (i, 0))
    full = lambda a: pl.BlockSpec(a.shape, lambda i: (0, 0))
    return pl.pallas_call(
        _ffn_kernel,
        grid=(rows // ROW_TILE,),
        in_specs=[row_spec(D_MODEL), row_spec(GROUP_WIDTH), row_spec(GROUP_WIDTH),
                  full(wout16), full(gffn), full(wg16), full(wu16), full(wd16), full(gfin)],
        out_specs=row_spec(D_MODEL),
        out_shape=jax.ShapeDtypeStruct((rows, D_MODEL), F32),
        scratch_shapes=[pltpu.VMEM((ROW_TILE, D_FF), BF16)],
        compiler_params=pltpu.CompilerParams(dimension_semantics=("parallel",), vmem_limit_bytes=VMEM_LIMIT),
        name="outproj_ffn",
    )(x, od, osb, wout16, gffn, wg16, wu16, wd16, gfin)


def _subln(o, g):
    o = o * lax.rsqrt(jnp.mean(o * o, axis=-1, keepdims=True) + RMS_EPS)
    return o * g * (1.0 - LAMBDA_INIT)


def _diff_head(slope2, lam, g_ref, q_ref, k_ref, v_ref, bias_ref, emit, between):
    r = lax.broadcasted_iota(jnp.int32, (TQ, TK), 0)
    c = lax.broadcasted_iota(jnp.int32, (TQ, TK), 1)
    bias_ref[...] = jnp.where((c // CHUNK) <= (r // CHUNK),
                              jnp.where(c > r, (-2.0 * slope2) * (c - r).astype(F32), 0.0), NEG_INF)
    lane = lax.broadcasted_iota(jnp.int32, (TQ, HEAD_LANES), 1)
    key_bias = slope2 * lax.broadcasted_iota(jnp.int32, (BF16_ROWS, SEQ), 1).astype(F32)
    row = lax.broadcasted_iota(jnp.int32, (BF16_ROWS, SEQ), 0)
    hi = key_bias.astype(BF16).astype(F32)
    mid = (key_bias - hi).astype(BF16).astype(F32)
    lo = key_bias - hi - mid
    split = jnp.where(row == 0, hi, jnp.where(row == 1, mid, jnp.where(row == 2, lo, 0.0))).astype(BF16)
    kt = jnp.concatenate([jnp.transpose(k_ref[...]), split,
                          jnp.zeros((HEAD_LANES - BF16_ROWS, SEQ), BF16)], axis=0)
    ones = jnp.where(lane < 3, 1.0, 0.0).astype(BF16)

    def scores(i):
        q = q_ref[i * TQ:(i + 1) * TQ, :]
        qms = [jnp.concatenate([jnp.where((lane < HALF) == first, q, jnp.zeros_like(q)), ones], axis=-1)
               for first in (True, False)]
        raw = [jnp.dot(qm, kt[:, :TK * (i + 1)], preferred_element_type=F32) for qm in qms]
        return [jnp.concatenate([s[:, :TK * i], s[:, TK * i:] + bias_ref[...]], axis=-1) if i else
                s + bias_ref[...] for s in raw]

    def attend(i, s):
        e = [jnp.exp2(sm - jnp.max(sm, axis=-1, keepdims=True)) for sm in s]
        l = [jnp.sum(em, axis=-1, keepdims=True) for em in e]
        a = (e[0] - (lam * l[0] / l[1]) * e[1]).astype(BF16)
        o = jnp.dot(a, v_ref[0:TK * (i + 1), :], preferred_element_type=F32) / l[0]
        emit(i, _subln(o, g_ref[...]).astype(BF16))

    n_q = SEQ // TQ
    s_next = scores(0)
    for i in range(n_q):
        s_now = s_next
        if i + 1 < n_q:
            s_next = scores(i + 1)
        between(i)
        attend(i, s_now)


def _diff_ffn_kernel(slope_ref, lamv_ref, gsub_ref, q_ref, k_ref, v_ref, x_ref, osb_ref, wout_ref, gffn_ref,
                     wg_ref, wu_ref, wd_ref, gfin_ref, y_ref, bias_ref, od_buf, act_ref):
    b, h = pl.program_id(0), pl.program_id(1)
    slot = b % 2
    slope2 = slope_ref[h] * LOG2E

    def emit(i, o):
        od_buf[slot, h, i * TQ:(i + 1) * TQ, :] = o

    @pl.when(b == 0)
    def _():
        _diff_head(slope2, _lambda(lamv_ref), gsub_ref, q_ref, k_ref, v_ref, bias_ref, emit, lambda i: None)

    @pl.when(b > 0)
    def _():
        tile_rows = pl.ds(pl.multiple_of(h * ROW_TILE, ROW_TILE), ROW_TILE)
        previous = lambda: jnp.concatenate([od_buf[1 - slot, hh, tile_rows, :] for hh in range(DIFF_HEADS)],
                                           axis=-1)
        pieces = _ffn_pieces(SEQ // TQ, x_ref, previous, osb_ref, wout_ref, gffn_ref, wg_ref, wu_ref, wd_ref,
                             gfin_ref, y_ref, act_ref)
        _diff_head(slope2, _lambda(lamv_ref), gsub_ref, q_ref, k_ref, v_ref, bias_ref, emit,
                   lambda i: pieces[i]())


def _diff_ffn(slopes, lamv, gsub, qd, kd16, vd16, x, osb, wout16, gffn, wg16, wu16, wd16, gfin):
    rows = x.shape[0]
    batch = rows // SEQ
    assert SEQ // ROW_TILE == DIFF_HEADS
    head_spec = pl.BlockSpec((SEQ, HEAD_LANES), lambda b, h: (jnp.minimum(b, batch - 1), h))
    tile = lambda b, h: (jnp.where(b == 0, 0, (b - 1) * DIFF_HEADS + h), 0)
    row_spec = lambda width: pl.BlockSpec((ROW_TILE, width), tile)
    const = lambda a: pl.BlockSpec(a.shape, lambda b, h: (0, 0))
    weight = lambda a: pl.BlockSpec(a.shape, lambda b, h: (0, 0), pipeline_mode=pl.Buffered(1))
    return pl.pallas_call(
        _diff_ffn_kernel,
        grid=(batch + 1, DIFF_HEADS),
        in_specs=[pl.BlockSpec(memory_space=pltpu.SMEM), const(lamv), const(gsub), head_spec, head_spec, head_spec,
                  row_spec(D_MODEL), row_spec(GROUP_WIDTH), weight(wout16), const(gffn), weight(wg16), weight(wu16),
                  weight(wd16), const(gfin)],
        out_specs=row_spec(D_MODEL),
        out_shape=jax.ShapeDtypeStruct((rows, D_MODEL), F32),
        scratch_shapes=[pltpu.VMEM((TQ, TK), F32), pltpu.VMEM((2, DIFF_HEADS, SEQ, HEAD_LANES), BF16),
                        pltpu.VMEM((ROW_TILE, D_FF), BF16)],
        compiler_params=pltpu.CompilerParams(dimension_semantics=("arbitrary", "arbitrary"),
                                             vmem_limit_bytes=FUSED_VMEM_LIMIT),
        name="diff_prompt_ffn",
    )(slopes, lamv, gsub, qd, kd16, vd16, x, osb, wout16, gffn, wg16, wu16, wd16, gfin)


def _sb_prompt_kernel(q_ref, kt_ref, vt_ref, o_ref, u_ref, carry_ref, acc_ref):
    u_ref[...] = _suffix_matrix(SB_BLOCK)
    causal = (lax.broadcasted_iota(jnp.int32, (SB_BLOCK, SB_BLOCK), 1)
              < lax.broadcasted_iota(jnp.int32, (SB_BLOCK, SB_BLOCK), 0))
    lane = lax.broadcasted_iota(jnp.int32, (SB_BLOCK, HEAD_LANES), 1)

    def rows_of(i):
        return pl.ds(pl.multiple_of(i * SB_BLOCK, SB_BLOCK), SB_BLOCK)

    def begin(slot, i):
        q = q_ref[rows_of(i), :]
        carry_ref[slot] = jnp.zeros(carry_ref.shape[1:], F32)
        acc_ref[slot] = jnp.zeros(acc_ref.shape[1:], F32)
        return jnp.where(lane < HALF, q, jnp.zeros_like(q)), jnp.where(lane >= HALF, q, jnp.zeros_like(q))

    def add_blocks(items, diagonal):
        chains = [(n, h, t) for n, item in enumerate(items) for h in range(2) for t in reversed(range(item[3]))]
        mask = {c: causal if diagonal and c[2] == items[c[0]][3] - 1 else None for c in chains}
        z = {(n, h, t): jnp.dot(items[n][1][h], kt_ref[items[n][2] + t], preferred_element_type=F32)
             for n, h, t in chains}
        part = {c: _stick_block(z[c], u_ref[...], mask[c]) for c in chains}
        carry = {(n, h): carry_ref[item[0], h] for n, item in enumerate(items) for h in range(2)}
        weights = {}
        for n, h, t in chains:
            weights[n, h, t] = _stick_weights(part[n, h, t][0], carry[n, h], mask[n, h, t]).astype(BF16)
            carry[n, h] = carry[n, h] + part[n, h, t][1]
        for n, (slot, _, first, count) in enumerate(items):
            for h in range(2):
                acc = acc_ref[slot, h]
                for t in range(count):
                    acc = acc + lax.dot_general(weights[n, h, t], vt_ref[first + t], NT_DIMS,
                                                preferred_element_type=F32)
                acc_ref[slot, h] = acc
                carry_ref[slot, h] = carry[n, h]

    def finish(slot, qms, i, j):
        def unfinished(state):
            j, top = state
            return jnp.logical_and(j >= 0, top > EXIT_LOG2)

        def earlier(state):
            j, _ = state
            add_blocks([(slot, qms, j, 1)], False)
            return j - 1, jnp.max(carry_ref[slot])

        lax.while_loop(unfinished, earlier, (j, jnp.max(carry_ref[slot])))
        o_ref[rows_of(i), :] = jnp.where(lane < HALF, acc_ref[slot, 0], acc_ref[slot, 1]).astype(BF16)

    def query_pair(i_a, count_a, i_b):
        qms_a, qms_b = begin(0, i_a), begin(1, i_b)
        first_a = i_a - (count_a - 1)
        add_blocks([(0, qms_a, first_a, count_a), (1, qms_b, i_b - 1, 2)], True)
        finish(0, qms_a, i_a, first_a - 1)
        finish(1, qms_b, i_b, i_b - 2)

    n_q = SEQ // SB_BLOCK
    query_pair(0, 1, n_q - 1)

    def middle(m, carry):
        query_pair(2 * m + 1, 2, 2 * m + 2)
        return carry

    lax.fori_loop(0, (n_q - 2) // 2, middle, 0)


def _sb_prompt(qs, kst16, vst16):
    batch = qs.shape[0] // SEQ
    pair_spec = pl.BlockSpec((SEQ, HEAD_LANES), lambda b, p: (b, p))
    kv_spec = pl.BlockSpec((None, SEQ // SB_BLOCK, HEAD_LANES, SB_BLOCK), lambda b, p: (b, 0, p, 0))
    return pl.pallas_call(
        _sb_prompt_kernel,
        grid=(batch, SB_PAIRS),
        in_specs=[pair_spec, kv_spec, kv_spec],
        out_specs=pair_spec,
        out_shape=jax.ShapeDtypeStruct(qs.shape, BF16),
        scratch_shapes=[pltpu.VMEM((SB_BLOCK, SB_BLOCK), BF16),pltpu.VMEM((2, 2, SB_BLOCK, STAT_LANES), F32),
                        pltpu.VMEM((2, 2, SB_BLOCK, HEAD_LANES), F32)],
        compiler_params=pltpu.CompilerParams(dimension_semantics=("parallel", "parallel"),
                                             vmem_limit_bytes=VMEM_LIMIT),
        name="sb_prompt",
    )(qs, kst16, vst16)


def _stack_queries(q):
    lane = lax.broadcasted_iota(jnp.int32, q.shape, 1)
    zero = jnp.zeros_like(q)
    return jnp.concatenate([jnp.where(lane < HALF, q, zero), jnp.where(lane >= HALF, q, zero)], axis=0)


def _diff_sample_kernel(slope_ref, lamv_ref, g_ref, q_ref, kn_ref, vn_ref, kc_ref, vc_ref, o_ref):
    lam = _lambda(lamv_ref)
    kpos = lax.broadcasted_iota(jnp.int32, (PAST_LEN, HEAD_LANES), 0)
    qpos = PAST_LEN + (lax.broadcasted_iota(jnp.int32, (PAST_LEN, HEAD_LANES), 1) % HALF)
    dist_c = (qpos - kpos).astype(F32)
    kn_pos = lax.broadcasted_iota(jnp.int32, (DEC_SEQ, HEAD_LANES), 0)
    qn_pos = lax.broadcasted_iota(jnp.int32, (DEC_SEQ, HEAD_LANES), 1) % HALF
    dist_n = jnp.abs(qn_pos - kn_pos).astype(F32)
    cols = [slice(h * HEAD_LANES, (h + 1) * HEAD_LANES) for h in range(DIFF_HEADS)]
    head_rows = lambda ref, h: ref[pl.ds(h, PAST_LEN, stride=DIFF_HEADS), :].astype(BF16)
    for heads in ((0, 1), (2, 3)):
        s_c, s_n = {}, {}
        for h in heads:
            slope2 = slope_ref[h] * LOG2E
            q2 = _stack_queries(q_ref[:, cols[h]])
            s_c[h] = lax.dot_general(head_rows(kc_ref, h), q2, NT_DIMS, preferred_element_type=F32) - slope2 * dist_c
            s_n[h] = lax.dot_general(kn_ref[:, cols[h]], q2, NT_DIMS, preferred_element_type=F32) - slope2 * dist_n
        p_c, p_n, l = {}, {}, {}
        for h in heads:
            m = jnp.maximum(jnp.max(s_c[h], axis=0, keepdims=True), jnp.max(s_n[h], axis=0, keepdims=True))
            p_c[h] = jnp.exp2(s_c[h] - m)
            p_n[h] = jnp.exp2(s_n[h] - m)
            l[h] = jnp.sum(p_c[h], axis=0, keepdims=True) + jnp.sum(p_n[h], axis=0, keepdims=True)
        for h in heads:
            acc = (lax.dot_general(p_c[h].astype(BF16), head_rows(vc_ref, h), TN_DIMS, preferred_element_type=F32)
                   + lax.dot_general(p_n[h].astype(BF16), vn_ref[:, cols[h]], TN_DIMS, preferred_element_type=F32))
            l_col = jnp.transpose(jnp.broadcast_to(l[h], (HEAD_LANES, HEAD_LANES)))[:, 0:1]
            o = acc / l_col
            o = o[0:DEC_SEQ] - lam * o[DEC_SEQ:2 * DEC_SEQ]
            o_ref[:, cols[h]] = _subln(o, g_ref[...]).astype(BF16)


def _diff_sample(slopes, lamv, g, qd, kd16, vd16, cache_k, cache_v):
    batch = cache_k.shape[0]
    new_spec = pl.BlockSpec((DEC_SEQ, GROUP_WIDTH), lambda b: (b, 0))
    cache_spec = pl.BlockSpec((None, PAST_LEN * DIFF_HEADS, HEAD_LANES), lambda b: (b, 0, 0))
    return pl.pallas_call(
        _diff_sample_kernel,
        grid=(batch,),
        in_specs=[pl.BlockSpec(memory_space=pltpu.SMEM),
                  pl.BlockSpec(lamv.shape, lambda b: (0, 0)),
                  pl.BlockSpec(g.shape, lambda b: (0, 0)),
                  new_spec, new_spec, new_spec, cache_spec, cache_spec],
        out_specs=new_spec,
        out_shape=jax.ShapeDtypeStruct(qd.shape, BF16),
        compiler_params=pltpu.CompilerParams(dimension_semantics=("parallel",), vmem_limit_bytes=VMEM_LIMIT),
        name="diff_sample",
    )(slopes, lamv, g, qd, kd16, vd16, cache_k, cache_v)


def _sb_sample_kernel(q_ref, kn_ref, vn_ref, kct_ref, vct_ref, o_ref, carry_ref, acc_ref):
    q2 = _stack_queries(q_ref[...])

    key_idx = lax.broadcasted_iota(jnp.int32, (HEAD_LANES, DEC_SEQ), 1)
    query_idx = lax.broadcasted_iota(jnp.int32, (HEAD_LANES, DEC_SEQ), 0) % HALF
    visible = key_idx < query_idx
    z = lax.dot_general(q2, kn_ref[...], NT_DIMS, preferred_element_type=F32)
    log2w, carry = _stick_block(z, _suffix_matrix(DEC_SEQ), visible)
    a = jnp.where(visible, jnp.exp2(log2w), 0.0)
    acc_ref[...] = jnp.dot(a.astype(BF16), vn_ref[...], preferred_element_type=F32)
    carry_ref[...] = jnp.broadcast_to(carry, carry_ref.shape)
    suffix = _suffix_matrix(CUM_BLOCK)
    n_blocks = SAMPLE_SWEEP // CUM_BLOCK

    def unfinished(state):
        j, top = state
        return jnp.logical_and(j >= 0, top > EXIT_LOG2)

    def sweep(state):
        j, _ = state
        cols = pl.ds(pl.multiple_of(j * SAMPLE_SWEEP, SAMPLE_SWEEP), SAMPLE_SWEEP)
        z = jnp.dot(q2, kct_ref[:, cols].astype(BF16), preferred_element_type=F32)
        parts = [_stick_block(z[:, b * CUM_BLOCK:(b + 1) * CUM_BLOCK], suffix, None) for b in range(n_blocks)]
        carry = carry_ref[...]
        weights = [None] * n_blocks
        for b in reversed(range(n_blocks)):
            weights[b] = _stick_weights(parts[b][0], carry, None)
            carry = carry + parts[b][1]
        a = jnp.concatenate(weights, axis=-1).astype(BF16)
        acc_ref[...] += lax.dot_general(a, vct_ref[:, cols].astype(BF16), NT_DIMS, preferred_element_type=F32)
        carry_ref[...] = carry
        return j - 1, jnp.max(carry)

    lax.while_loop(unfinished, sweep, (PAST_LEN // SAMPLE_SWEEP - 1, jnp.max(carry)))
    lane = lax.broadcasted_iota(jnp.int32, (DEC_SEQ, HEAD_LANES), 1)
    o_ref[...] = jnp.where(lane < HALF, acc_ref[0:DEC_SEQ, :], acc_ref[DEC_SEQ:2 * DEC_SEQ, :]).astype(BF16)


def _sb_sample(qs, ks16, vs16, cache_kt, cache_vt):
    batch = cache_kt.shape[0]
    new_spec = pl.BlockSpec((DEC_SEQ, HEAD_LANES), lambda b, p: (b, p))
    cache_spec = pl.BlockSpec((None, HEAD_LANES, PAST_LEN), lambda b, p: (b, p, 0))
    return pl.pallas_call(
        _sb_sample_kernel,
        grid=(batch, SB_PAIRS),
        in_specs=[new_spec, new_spec, new_spec, cache_spec, cache_spec],
        out_specs=new_spec,
        out_shape=jax.ShapeDtypeStruct(qs.shape, BF16),
        scratch_shapes=[pltpu.VMEM((HEAD_LANES, STAT_LANES), F32), pltpu.VMEM((HEAD_LANES, HEAD_LANES), F32)],
        compiler_params=pltpu.CompilerParams(dimension_semantics=("parallel", "parallel"),
                                             vmem_limit_bytes=VMEM_LIMIT),
        name="sb_sample",
    )(qs, ks16, vs16, cache_kt, cache_vt)


def kernel(x_prompt, x_sample, cache_diff_k, cache_diff_v, cache_sb_k, cache_sb_v, norm_attn_g, w_in,
           lambda_q1, lambda_k1, lambda_q2, lambda_k2, diff_subln_g, w_out, norm_ffn_g, w_gate, w_up, w_down,
           norm_final_g):
    batch, seq, _ = x_prompt.shape
    dec_batch, dec_seq, _ = x_sample.shape
    assert seq == SEQ and dec_seq == DEC_SEQ and cache_diff_k.shape[2] == PAST_LEN and w_in.shape[0] == 1

    w_in16 = w_in[0].astype(BF16)
    w_sb_t16 = jnp.transpose(w_in16[:, 4 * GROUP_WIDTH:])
    w_out16 = w_out[0].astype(BF16)
    w_gate16 = w_gate[0].astype(BF16)
    w_up16 = w_up[0].astype(BF16)
    w_down16 = w_down[0].astype(BF16)
    g_attn = norm_attn_g[0].reshape(1, D_MODEL)
    g_ffn = norm_ffn_g[0].reshape(1, D_MODEL)
    g_final = norm_final_g.reshape(1, D_MODEL)
    g_subln = diff_subln_g[0].reshape(1, HEAD_LANES)
    lamv = jnp.concatenate([lambda_q1, lambda_k1, lambda_q2, lambda_k2], axis=0).astype(F32)
    slopes = jnp.exp2(-8.0 / DIFF_HEADS * jnp.arange(1, DIFF_HEADS + 1, dtype=F32))

    def ffn(x, od, osb):
        return _ffn(x, od, osb, w_out16, g_ffn, w_gate16, w_up16, w_down16, g_final)

    xp = x_prompt.reshape(batch * SEQ, D_MODEL)
    qd, qs, kd, vd, kst, vst, kd16, vd16, kst16, vst16 = _inproj(xp, g_attn, w_in16, w_sb_t16, True)
    osb = _sb_prompt(qs, kst16, vst16)
    y_prompt = _diff_ffn(slopes, lamv, g_subln, qd, kd16, vd16, xp, osb, w_out16, g_ffn, w_gate16, w_up16, w_down16,
                         g_final).reshape(batch, SEQ, D_MODEL)

    xs = x_sample.reshape(dec_batch * DEC_SEQ, D_MODEL)
    qd2, qs2, kd2, vd2, ks2, vs2, kd2_16, vd2_16, ks2_16, vs2_16 = _inproj(xs, g_attn, w_in16, w_sb_t16, False)
    head_major = lambda a: a[0].reshape(dec_batch, PAST_LEN * DIFF_HEADS, HEAD_LANES)
    od2 = _diff_sample(slopes, lamv, g_subln, qd2, kd2_16, vd2_16, head_major(cache_diff_k), head_major(cache_diff_v))
    keys_minor = lambda a: jnp.transpose(a[0], (0, 2, 3, 1)).reshape(dec_batch, GROUP_WIDTH, PAST_LEN)
    osb2 = _sb_sample(qs2, ks2_16, vs2_16, keys_minor(cache_sb_k), keys_minor(cache_sb_v))
    y_sample = ffn(xs, od2, osb2).reshape(dec_batch, DEC_SEQ, D_MODEL)

    diff_shape = lambda b, t: (1, b, t, DIFF_HEADS, HEAD_LANES)
    sb_shape = lambda b, t: (1, b, t, 2 * SB_PAIRS, HALF)
    from_keys_minor = lambda a: jnp.transpose(a.reshape(batch, 2 * SB_PAIRS, HALF, SEQ), (0, 3, 1, 2))[None]
    return (y_prompt, y_sample,
            kd.reshape(diff_shape(batch, SEQ)), vd.reshape(diff_shape(batch, SEQ)),
            from_keys_minor(kst), from_keys_minor(vst),
            kd2.reshape(diff_shape(dec_batch, DEC_SEQ)), vd2.reshape(diff_shape(dec_batch, DEC_SEQ)),
            ks2.reshape(sb_shape(dec_batch, DEC_SEQ)), vs2.reshape(sb_shape(dec_batch, DEC_SEQ)))
```

```python
import functools
import math

import jax
import jax.numpy as jnp
from jax import lax
from jax.experimental import pallas as pl
from jax.experimental.pallas import tpu as pltpu

D_MODEL = 1024
SEQ = 2048
DEC_SEQ = 64
PAST_LEN = 4096
CHUNK = 64
GROUP_WIDTH = 512
N_GROUPS = 6
DIFF_HEADS = 4
HEAD_LANES = 128
HALF = 64
SB_PAIRS = 4
D_FF = 2816
FF_CHUNK = 256
RMS_EPS = 1e-6
NEG_INF = -1e30
LOG2E = math.log2(math.e)
QK_SCALE = 0.125 * LOG2E
LAMBDA_INIT = 0.8 - 0.6 * math.exp(-0.3 * 0)

ROW_TILE = 512
TQ = 256
TK = TQ
CUM_BLOCK = 256
SB_BLOCK = CUM_BLOCK
SAMPLE_SWEEP = 512
EXIT_LOG2 = -160.0
STAT_LANES = 128
VMEM_LIMIT = 56 * 1024 * 1024
FUSED_VMEM_LIMIT = 58 * 1024 * 1024

F32 = jnp.float32
BF16 = jnp.bfloat16
BF16_ROWS = 16
NT_DIMS = (((1,), (1,)), ((), ()))
TN_DIMS = (((0,), (0,)), ((), ()))


def _rms(x, g):
    return x * lax.rsqrt(jnp.mean(x * x, axis=-1, keepdims=True) + RMS_EPS) * g


def _lambda(lamv_ref):
    lv = lamv_ref[...]
    a = jnp.sum(lv[0:1] * lv[1:2], axis=-1, keepdims=True)
    b = jnp.sum(lv[2:3] * lv[3:4], axis=-1, keepdims=True)
    return jnp.exp(a) - jnp.exp(b) + LAMBDA_INIT


def _log2_keep(z2):
    nz = -z2
    return jnp.minimum(nz, 0.0) - jnp.log(1.0 + jnp.exp2(jnp.minimum(z2, nz))) * LOG2E


def _lanes(x, width):
    if width <= STAT_LANES:
        return x[:, :width]
    return jnp.concatenate([x] * (width // STAT_LANES), axis=1)


def _suffix_matrix(n):
    j = lax.broadcasted_iota(jnp.int32, (n, n), 0)
    s = lax.broadcasted_iota(jnp.int32, (n, n), 1)
    return jnp.where(j > s, 1.0, 0.0).astype(BF16)


def _stick_block(z, suffix, visible):
    lk = _log2_keep(z)
    if visible is not None:
        lk = jnp.where(visible, lk, 0.0)
    within = jnp.dot(lk.astype(BF16), suffix, preferred_element_type=F32)
    return (z + lk) + within, jnp.sum(lk, axis=-1, keepdims=True)


def _stick_weights(log2w, carry, visible):
    a = jnp.exp2(log2w + _lanes(carry, log2w.shape[-1]))
    return a if visible is None else jnp.where(visible, a, 0.0)


def _inproj_kernel(prompt, x_ref, g_ref, w_ref, wt_ref, *refs):
    if prompt:
        sample_refs, outs, scratch = refs[:5], refs[5:16], refs[16:]
        sample = _sb_sample_pieces(*sample_refs, outs[10], *scratch)
    else:
        outs, sample = refs, [lambda: None] * 5
    qd_ref, qs_ref, kd_ref, vd_ref, ks_ref, vs_ref, kd16_ref, vd16_ref, ks16_ref, vs16_ref = outs[:10]
    n = _rms(x_ref[...], g_ref[...]).astype(BF16)

    def proj(c):
        return jnp.dot(n, w_ref[:, c * GROUP_WIDTH:(c + 1) * GROUP_WIDTH], preferred_element_type=F32)

    qd_ref[...] = (proj(0) * QK_SCALE).astype(BF16)
    sample[0]()
    qs_ref[...] = (proj(3) * QK_SCALE).astype(BF16)
    sample[1]()
    for c, out_ref, out16_ref, piece in ((1, kd_ref, kd16_ref, sample[2]), (2, vd_ref, vd16_ref, sample[3])):
        r = proj(c)
        out16_ref[...] = r.astype(BF16)
        for h in range(DIFF_HEADS):
            out_ref[pl.ds(h, ROW_TILE, stride=DIFF_HEADS), :] = r[:, h * HEAD_LANES:(h + 1) * HEAD_LANES]
        piece()
    for c, out_ref, out16_ref in ((4, ks_ref, ks16_ref), (5, vs_ref, vs16_ref)):
        if prompt:
            rows = slice((c - 4) * GROUP_WIDTH, (c - 3) * GROUP_WIDTH)
            r = lax.dot_general(wt_ref[rows, :], n, NT_DIMS, preferred_element_type=F32)
            out_ref[...] = r
            for t in range(ROW_TILE // SB_BLOCK):
                out16_ref[t] = r[:, t * SB_BLOCK:(t + 1) * SB_BLOCK].astype(BF16)
        else:
            r = proj(c)
            out_ref[...] = r
            out16_ref[...] = r.astype(BF16)
    sample[4]()


def _inproj(x, g, w16, wt16, sample=None):
    rows = x.shape[0]
    steps = rows // ROW_TILE
    row_spec = lambda width: pl.BlockSpec((ROW_TILE, width), lambda i: (i, 0))
    head_major = pl.BlockSpec((ROW_TILE * DIFF_HEADS, HEAD_LANES), lambda i: (i, 0))
    weight = lambda a: pl.BlockSpec(a.shape, lambda i: (0, 0), pipeline_mode=pl.Buffered(1))
    bf16_rows = jax.ShapeDtypeStruct((rows, GROUP_WIDTH), BF16)
    f32_head_major = jax.ShapeDtypeStruct((rows * DIFF_HEADS, HEAD_LANES), F32)
    in_specs = [row_spec(D_MODEL), pl.BlockSpec((1, D_MODEL), lambda i: (0, 0)), weight(w16), weight(wt16)]
    operands = [x, g, w16, wt16]
    scratch = []
    if sample is not None:
        blocks_per_seq = SEQ // ROW_TILE
        batch = rows // SEQ
        sb32_spec = pl.BlockSpec((None, GROUP_WIDTH, ROW_TILE), lambda i: (i // blocks_per_seq, 0, i % blocks_per_seq))
        sb16_spec = pl.BlockSpec((None, ROW_TILE // SB_BLOCK, GROUP_WIDTH, SB_BLOCK),
                                 lambda i: (i // blocks_per_seq, i % blocks_per_seq, 0, 0))
        sb32_shape = jax.ShapeDtypeStruct((batch, GROUP_WIDTH, SEQ), F32)
        sb16_shape = jax.ShapeDtypeStruct((batch, SEQ // SB_BLOCK, GROUP_WIDTH, SB_BLOCK), BF16)
        assert steps == sample[3].shape[0] * SB_PAIRS
        new_spec = pl.BlockSpec((DEC_SEQ, HEAD_LANES), lambda i: (i // SB_PAIRS, i % SB_PAIRS))
        cache_spec = pl.BlockSpec((None, HEAD_LANES, PAST_LEN), lambda i: (i // SB_PAIRS, i % SB_PAIRS, 0))
        in_specs += [new_spec, new_spec, new_spec, cache_spec, cache_spec]
        operands += list(sample)
        extra_specs, extra_shapes = [new_spec], [jax.ShapeDtypeStruct(sample[0].shape, BF16)]
        scratch = [pltpu.VMEM((HEAD_LANES, STAT_LANES), F32), pltpu.VMEM((HEAD_LANES, HEAD_LANES), F32)]
    else:
        sb32_spec = sb16_spec = row_spec(GROUP_WIDTH)
        sb32_shape = jax.ShapeDtypeStruct((rows, GROUP_WIDTH), F32)
        sb16_shape = jax.ShapeDtypeStruct((rows, GROUP_WIDTH), BF16)
        extra_specs, extra_shapes = [], []
    return pl.pallas_call(
        functools.partial(_inproj_kernel, sample is not None),
        grid=(steps,),
        in_specs=in_specs,
        out_specs=[row_spec(GROUP_WIDTH), row_spec(GROUP_WIDTH), head_major, head_major, sb32_spec, sb32_spec,
                   row_spec(GROUP_WIDTH), row_spec(GROUP_WIDTH), sb16_spec, sb16_spec] + extra_specs,
        out_shape=[bf16_rows, bf16_rows, f32_head_major, f32_head_major, sb32_shape, sb32_shape,
                   bf16_rows, bf16_rows, sb16_shape, sb16_shape] + extra_shapes,
        scratch_shapes=scratch,
        compiler_params=pltpu.CompilerParams(dimension_semantics=("arbitrary",), vmem_limit_bytes=VMEM_LIMIT),
        name="inproj",
    )(*operands)


def _ffn_pieces(n_pieces, x_ref, od, osb_ref, wout_ref, gffn_ref, wg_ref, wu_ref, wd_ref, gfin_ref, y_ref, act_ref):
    state = {}
    n_chunks = D_FF // FF_CHUNK

    def first():
        mixed = jnp.concatenate([od(), osb_ref[...]], axis=-1)
        state["h"] = x_ref[...] + jnp.dot(mixed, wout_ref[...], preferred_element_type=F32)
        state["n"] = _rms(state["h"], gffn_ref[...]).astype(BF16)

    def chunks(lo, hi):
        def run():
            for c in range(lo, hi):
                cols = slice(c * FF_CHUNK, (c + 1) * FF_CHUNK)
                gate = jnp.dot(state["n"], wg_ref[:, cols], preferred_element_type=F32)
                up = jnp.dot(state["n"], wu_ref[:, cols], preferred_element_type=F32)
                act_ref[:, cols] = (gate / (1.0 + jnp.exp(-gate)) * up).astype(BF16)
        return run

    def last():
        y = state["h"] + jnp.dot(act_ref[...], wd_ref[...], preferred_element_type=F32)
        y_ref[...] = _rms(y, gfin_ref[...])

    middle = n_pieces - 2
    bounds = [(n_chunks * m) // middle for m in range(middle + 1)]
    return [first] + [chunks(bounds[m], bounds[m + 1]) for m in range(middle)] + [last]


def _ffn_kernel(x_ref, od_ref, osb_ref, wout_ref, gffn_ref, wg_ref, wu_ref, wd_ref, gfin_ref, y_ref, act_ref):
    for piece in _ffn_pieces(3, x_ref, lambda: od_ref[...], osb_ref, wout_ref, gffn_ref, wg_ref, wu_ref, wd_ref,
                             gfin_ref, y_ref, act_ref):
        piece()


def _ffn(x, od, osb, wout16, gffn, wg16, wu16, wd16, gfin):
    rows = x.shape[0]
    row_spec = lambda width: pl.BlockSpec((ROW_TILE, width), lambda i: (i, 0))
    full = lambda a: pl.BlockSpec(a.shape, lambda i: (0, 0))
    return pl.pallas_call(
        _ffn_kernel,
        grid=(rows // ROW_TILE,),
        in_specs=[row_spec(D_MODEL), row_spec(GROUP_WIDTH), row_spec(GROUP_WIDTH),
                  full(wout16), full(gffn), full(wg16), full(wu16), full(wd16), full(gfin)],
        out_specs=row_spec(D_MODEL),
        out_shape=jax.ShapeDtypeStruct((rows, D_MODEL), F32),
        scratch_shapes=[pltpu.VMEM((ROW_TILE, D_FF), BF16)],
        compiler_params=pltpu.CompilerParams(dimension_semantics=("parallel",), vmem_limit_bytes=VMEM_LIMIT),
        name="outproj_ffn",
    )(x, od, osb, wout16, gffn, wg16, wu16, wd16, gfin)


def _subln(o, g):
    o = o * lax.rsqrt(jnp.mean(o * o, axis=-1, keepdims=True) + RMS_EPS)
    return o * g * (1.0 - LAMBDA_INIT)


def _diff_head(slope2, lam, g_ref, q_ref, k_ref, v_ref, bias_ref, emit, between):
    r = lax.broadcasted_iota(jnp.int32, (TQ, TK), 0)
    c = lax.broadcasted_iota(jnp.int32, (TQ, TK), 1)
    bias_ref[...] = jnp.where((c // CHUNK) <= (r // CHUNK),
                              jnp.where(c > r, (-2.0 * slope2) * (c - r).astype(F32), 0.0), NEG_INF)
    lane = lax.broadcasted_iota(jnp.int32, (TQ, HEAD_LANES), 1)
    key_bias = slope2 * lax.broadcasted_iota(jnp.int32, (BF16_ROWS, SEQ), 1).astype(F32)
    row = lax.broadcasted_iota(jnp.int32, (BF16_ROWS, SEQ), 0)
    hi = key_bias.astype(BF16).astype(F32)
    mid = (key_bias - hi).astype(BF16).astype(F32)
    lo = key_bias - hi - mid
    split = jnp.where(row == 0, hi, jnp.where(row == 1, mid, jnp.where(row == 2, lo, 0.0))).astype(BF16)
    kt = jnp.concatenate([jnp.transpose(k_ref[...]), split,
                          jnp.zeros((HEAD_LANES - BF16_ROWS, SEQ), BF16)], axis=0)
    ones = jnp.where(lane < 3, 1.0, 0.0).astype(BF16)

    def scores(i):
        q = q_ref[i * TQ:(i + 1) * TQ, :]
        qms = [jnp.concatenate([jnp.where((lane < HALF) == first, q, jnp.zeros_like(q)), ones], axis=-1)
               for first in (True, False)]
        raw = [jnp.dot(qm, kt[:, :TK * (i + 1)], preferred_element_type=F32) for qm in qms]
        return [jnp.concatenate([s[:, :TK * i], s[:, TK * i:] + bias_ref[...]], axis=-1) if i else
                s + bias_ref[...] for s in raw]

    def attend(i, s):
        e = [jnp.exp2(sm - jnp.max(sm, axis=-1, keepdims=True)) for sm in s]
        l = [jnp.sum(em, axis=-1, keepdims=True) for em in e]
        a = (e[0] - (lam * l[0] / l[1]) * e[1]).astype(BF16)
        o = jnp.dot(a, v_ref[0:TK * (i + 1), :], preferred_element_type=F32) / l[0]
        emit(i, _subln(o, g_ref[...]).astype(BF16))

    n_q = SEQ // TQ
    s_next = scores(0)
    for i in range(n_q):
        s_now = s_next
        if i + 1 < n_q:
            s_next = scores(i + 1)
        between(i)
        attend(i, s_now)


def _diff_ffn_kernel(slope_ref, lamv_ref, gsub_ref, q_ref, k_ref, v_ref, x_ref, osb_ref, wout_ref, gffn_ref,
                     wg_ref, wu_ref, wd_ref, gfin_ref, y_ref, bias_ref, od_buf, act_ref):
    b, h = pl.program_id(0), pl.program_id(1)
    slot = b % 2
    slope2 = slope_ref[h] * LOG2E

    def emit(i, o):
        od_buf[slot, h, i * TQ:(i + 1) * TQ, :] = o

    @pl.when(b == 0)
    def _():
        _diff_head(slope2, _lambda(lamv_ref), gsub_ref, q_ref, k_ref, v_ref, bias_ref, emit, lambda i: None)

    @pl.when(b > 0)
    def _():
        tile_rows = pl.ds(pl.multiple_of(h * ROW_TILE, ROW_TILE), ROW_TILE)
        previous = lambda: jnp.concatenate([od_buf[1 - slot, hh, tile_rows, :] for hh in range(DIFF_HEADS)],
                                           axis=-1)
        pieces = _ffn_pieces(SEQ // TQ, x_ref, previous, osb_ref, wout_ref, gffn_ref, wg_ref, wu_ref, wd_ref,
                             gfin_ref, y_ref, act_ref)
        _diff_head(slope2, _lambda(lamv_ref), gsub_ref, q_ref, k_ref, v_ref, bias_ref, emit,
                   lambda i: pieces[i]())


def _diff_ffn(slopes, lamv, gsub, qd, kd16, vd16, x, osb, wout16, gffn, wg16, wu16, wd16, gfin):
    rows = x.shape[0]
    batch = rows // SEQ
    assert SEQ // ROW_TILE == DIFF_HEADS
    head_spec = pl.BlockSpec((SEQ, HEAD_LANES), lambda b, h: (jnp.minimum(b, batch - 1), h))
    tile = lambda b, h: (jnp.where(b == 0, 0, (b - 1) * DIFF_HEADS + h), 0)
    row_spec = lambda width: pl.BlockSpec((ROW_TILE, width), tile)
    const = lambda a: pl.BlockSpec(a.shape, lambda b, h: (0, 0))
    weight = lambda a: pl.BlockSpec(a.shape, lambda b, h: (0, 0), pipeline_mode=pl.Buffered(1))
    return pl.pallas_call(
        _diff_ffn_kernel,
        grid=(batch + 1, DIFF_HEADS),
        in_specs=[pl.BlockSpec(memory_space=pltpu.SMEM), const(lamv), const(gsub), head_spec, head_spec, head_spec,
                  row_spec(D_MODEL), row_spec(GROUP_WIDTH), weight(wout16), const(gffn), weight(wg16), weight(wu16),
                  weight(wd16), const(gfin)],
        out_specs=row_spec(D_MODEL),
        out_shape=jax.ShapeDtypeStruct((rows, D_MODEL), F32),
        scratch_shapes=[pltpu.VMEM((TQ, TK), F32), pltpu.VMEM((2, DIFF_HEADS, SEQ, HEAD_LANES), BF16),
                        pltpu.VMEM((ROW_TILE, D_FF), BF16)],
        compiler_params=pltpu.CompilerParams(dimension_semantics=("arbitrary", "arbitrary"),
                                             vmem_limit_bytes=FUSED_VMEM_LIMIT),
        name="diff_prompt_ffn",
    )(slopes, lamv, gsub, qd, kd16, vd16, x, osb, wout16, gffn, wg16, wu16, wd16, gfin)


def _sb_prompt_kernel(q_ref, kt_ref, vt_ref, o_ref, u_ref, carry_ref, acc_ref):
    u_ref[...] = _suffix_matrix(SB_BLOCK)
    causal = (lax.broadcasted_iota(jnp.int32, (SB_BLOCK, SB_BLOCK), 1)
              < lax.broadcasted_iota(jnp.int32, (SB_BLOCK, SB_BLOCK), 0))
    lane = lax.broadcasted_iota(jnp.int32, (SB_BLOCK, HEAD_LANES), 1)

    def rows_of(i):
        return pl.ds(pl.multiple_of(i * SB_BLOCK, SB_BLOCK), SB_BLOCK)

    def begin(slot, i):
        q = q_ref[rows_of(i), :]
        carry_ref[slot] = jnp.zeros(carry_ref.shape[1:], F32)
        acc_ref[slot] = jnp.zeros(acc_ref.shape[1:], F32)
        return jnp.where(lane < HALF, q, jnp.zeros_like(q)), jnp.where(lane >= HALF, q, jnp.zeros_like(q))

    def add_blocks(items, diagonal):
        chains = [(n, h, t) for n, item in enumerate(items) for h in range(2) for t in reversed(range(item[3]))]
        mask = {c: causal if diagonal and c[2] == items[c[0]][3] - 1 else None for c in chains}
        z = {(n, h, t): jnp.dot(items[n][1][h], kt_ref[items[n][2] + t], preferred_element_type=F32)
             for n, h, t in chains}
        part = {c: _stick_block(z[c], u_ref[...], mask[c]) for c in chains}
        carry = {(n, h): carry_ref[item[0], h] for n, item in enumerate(items) for h in range(2)}
        weights = {}
        for n, h, t in chains:
            weights[n, h, t] = _stick_weights(part[n, h, t][0], carry[n, h], mask[n, h, t]).astype(BF16)
            carry[n, h] = carry[n, h] + part[n, h, t][1]
        for n, (slot, _, first, count) in enumerate(items):
            for h in range(2):
                acc = acc_ref[slot, h]
                for t in range(count):
                    acc = acc + lax.dot_general(weights[n, h, t], vt_ref[first + t], NT_DIMS,
                                                preferred_element_type=F32)
                acc_ref[slot, h] = acc
                carry_ref[slot, h] = carry[n, h]

    def finish(slot, qms, i, j):
        def unfinished(state):
            j, top = state
            return jnp.logical_and(j >= 0, top > EXIT_LOG2)

        def earlier(state):
            j, _ = state
            add_blocks([(slot, qms, j, 1)], False)
            return j - 1, jnp.max(carry_ref[slot])

        lax.while_loop(unfinished, earlier, (j, jnp.max(carry_ref[slot])))
        o_ref[rows_of(i), :] = jnp.where(lane < HALF, acc_ref[slot, 0], acc_ref[slot, 1]).astype(BF16)

    def query_pair(i_a, count_a, i_b):
        qms_a, qms_b = begin(0, i_a), begin(1, i_b)
        first_a = i_a - (count_a - 1)
        add_blocks([(0, qms_a, first_a, count_a), (1, qms_b, i_b - 1, 2)], True)
        finish(0, qms_a, i_a, first_a - 1)
        finish(1, qms_b, i_b, i_b - 2)

    n_q = SEQ // SB_BLOCK
    query_pair(0, 1, n_q - 1)

    def middle(m, carry):
        query_pair(2 * m + 1, 2, 2 * m + 2)
        return carry

    lax.fori_loop(0, (n_q - 2) // 2, middle, 0)


def _sb_prompt(qs, kst16, vst16):
    batch = qs.shape[0] // SEQ
    pair_spec = pl.BlockSpec((SEQ, HEAD_LANES), lambda b, p: (b, p))
    kv_spec = pl.BlockSpec((None, SEQ // SB_BLOCK, HEAD_LANES, SB_BLOCK), lambda b, p: (b, 0, p, 0))
    return pl.pallas_call(
        _sb_prompt_kernel,
        grid=(batch, SB_PAIRS),
        in_specs=[pair_spec, kv_spec, kv_spec],
        out_specs=pair_spec,
        out_shape=jax.ShapeDtypeStruct(qs.shape, BF16),
        scratch_shapes=[pltpu.VMEM((SB_BLOCK, SB_BLOCK), BF16),pltpu.VMEM((2, 2, SB_BLOCK, STAT_LANES), F32),
                        pltpu.VMEM((2, 2, SB_BLOCK, HEAD_LANES), F32)],
        compiler_params=pltpu.CompilerParams(dimension_semantics=("parallel", "parallel"),
                                             vmem_limit_bytes=VMEM_LIMIT),
        name="sb_prompt",
    )(qs, kst16, vst16)


def _stack_queries(q):
    lane = lax.broadcasted_iota(jnp.int32, q.shape, 1)
    zero = jnp.zeros_like(q)
    return jnp.concatenate([jnp.where(lane < HALF, q, zero), jnp.where(lane >= HALF, q, zero)], axis=0)


def _diff_sample_kernel(slope_ref, lamv_ref, g_ref, q_ref, kn_ref, vn_ref, kc_ref, vc_ref, o_ref):
    lam = _lambda(lamv_ref)
    kpos = lax.broadcasted_iota(jnp.int32, (PAST_LEN, HEAD_LANES), 0)
    qpos = PAST_LEN + (lax.broadcasted_iota(jnp.int32, (PAST_LEN, HEAD_LANES), 1) % HALF)
    dist_c = (qpos - kpos).astype(F32)
    kn_pos = lax.broadcasted_iota(jnp.int32, (DEC_SEQ, HEAD_LANES), 0)
    qn_pos = lax.broadcasted_iota(jnp.int32, (DEC_SEQ, HEAD_LANES), 1) % HALF
    dist_n = jnp.abs(qn_pos - kn_pos).astype(F32)
    cols = [slice(h * HEAD_LANES, (h + 1) * HEAD_LANES) for h in range(DIFF_HEADS)]
    head_rows = lambda ref, h: ref[pl.ds(h, PAST_LEN, stride=DIFF_HEADS), :].astype(BF16)
    for heads in ((0, 1), (2, 3)):
        s_c, s_n = {}, {}
        for h in heads:
            slope2 = slope_ref[h] * LOG2E
            q2 = _stack_queries(q_ref[:, cols[h]])
            s_c[h] = lax.dot_general(head_rows(kc_ref, h), q2, NT_DIMS, preferred_element_type=F32) - slope2 * dist_c
            s_n[h] = lax.dot_general(kn_ref[:, cols[h]], q2, NT_DIMS, preferred_element_type=F32) - slope2 * dist_n
        p_c, p_n, l = {}, {}, {}
        for h in heads:
            m = jnp.maximum(jnp.max(s_c[h], axis=0, keepdims=True), jnp.max(s_n[h], axis=0, keepdims=True))
            p_c[h] = jnp.exp2(s_c[h] - m)
            p_n[h] = jnp.exp2(s_n[h] - m)
            l[h] = jnp.sum(p_c[h], axis=0, keepdims=True) + jnp.sum(p_n[h], axis=0, keepdims=True)
        for h in heads:
            acc = (lax.dot_general(p_c[h].astype(BF16), head_rows(vc_ref, h), TN_DIMS, preferred_element_type=F32)
                   + lax.dot_general(p_n[h].astype(BF16), vn_ref[:, cols[h]], TN_DIMS, preferred_element_type=F32))
            l_col = jnp.transpose(jnp.broadcast_to(l[h], (HEAD_LANES, HEAD_LANES)))[:, 0:1]
            o = acc / l_col
            o = o[0:DEC_SEQ] - lam * o[DEC_SEQ:2 * DEC_SEQ]
            o_ref[:, cols[h]] = _subln(o, g_ref[...]).astype(BF16)


def _diff_sample(slopes, lamv, g, qd, kd16, vd16, cache_k, cache_v):
    batch = cache_k.shape[0]
    new_spec = pl.BlockSpec((DEC_SEQ, GROUP_WIDTH), lambda b: (b, 0))
    cache_spec = pl.BlockSpec((None, PAST_LEN * DIFF_HEADS, HEAD_LANES), lambda b: (b, 0, 0))
    return pl.pallas_call(
        _diff_sample_kernel,
        grid=(batch,),
        in_specs=[pl.BlockSpec(memory_space=pltpu.SMEM),
                  pl.BlockSpec(lamv.shape, lambda b: (0, 0)),
                  pl.BlockSpec(g.shape, lambda b: (0, 0)),
                  new_spec, new_spec, new_spec, cache_spec, cache_spec],
        out_specs=new_spec,
        out_shape=jax.ShapeDtypeStruct(qd.shape, BF16),
        compiler_params=pltpu.CompilerParams(dimension_semantics=("parallel",), vmem_limit_bytes=VMEM_LIMIT),
        name="diff_sample",
    )(slopes, lamv, g, qd, kd16, vd16, cache_k, cache_v)


def _sb_sample_pieces(q_ref, kn_ref, vn_ref, kct_ref, vct_ref, o_ref, carry_ref, acc_ref):
    state = {}
    n_sweeps = PAST_LEN // SAMPLE_SWEEP
    n_blocks = SAMPLE_SWEEP // CUM_BLOCK
    last_cols = slice((n_sweeps - 1) * SAMPLE_SWEEP, n_sweeps * SAMPLE_SWEEP)
    key_idx = lax.broadcasted_iota(jnp.int32, (HEAD_LANES, DEC_SEQ), 1)
    query_idx = lax.broadcasted_iota(jnp.int32, (HEAD_LANES, DEC_SEQ), 0) % HALF
    visible = key_idx < query_idx

    def block_parts(z):
        return [_stick_block(z[:, b * CUM_BLOCK:(b + 1) * CUM_BLOCK], state["suffix"], None) for b in range(n_blocks)]

    def add_values(parts, cols):
        carry = carry_ref[...]
        weights = [None] * n_blocks
        for b in reversed(range(n_blocks)):
            weights[b] = _stick_weights(parts[b][0], carry, None)
            carry = carry + parts[b][1]
        a = jnp.concatenate(weights, axis=-1).astype(BF16)
        acc_ref[...] += lax.dot_general(a, vct_ref[:, cols].astype(BF16), NT_DIMS, preferred_element_type=F32)
        carry_ref[...] = carry
        return jnp.max(carry)

    def logits():
        state["q2"] = _stack_queries(q_ref[...])
        state["suffix"] = _suffix_matrix(CUM_BLOCK)
        state["z_new"] = lax.dot_general(state["q2"], kn_ref[...], NT_DIMS, preferred_element_type=F32)
        state["z_last"] = jnp.dot(state["q2"], kct_ref[:, last_cols].astype(BF16), preferred_element_type=F32)

    def block_sums():
        state["new"] = _stick_block(state["z_new"], _suffix_matrix(DEC_SEQ), visible)
        state["last"] = block_parts(state["z_last"])

    def new_values():
        log2w, carry = state["new"]
        a = jnp.where(visible, jnp.exp2(log2w), 0.0)
        acc_ref[...] = jnp.dot(a.astype(BF16), vn_ref[...], preferred_element_type=F32)
        carry_ref[...] = jnp.broadcast_to(carry, carry_ref.shape)

    def last_values():
        state["top"] = add_values(state["last"], last_cols)

    def earlier():
        def unfinished(loop):
            j, top = loop
            return jnp.logical_and(j >= 0, top > EXIT_LOG2)

        def body(loop):
            j, _ = loop
            cols = pl.ds(pl.multiple_of(j * SAMPLE_SWEEP, SAMPLE_SWEEP), SAMPLE_SWEEP)
            z = jnp.dot(state["q2"], kct_ref[:, cols].astype(BF16), preferred_element_type=F32)
            return j - 1, add_values(block_parts(z), cols)

        lax.while_loop(unfinished, body, (n_sweeps - 2, state["top"]))
        lane = lax.broadcasted_iota(jnp.int32, (DEC_SEQ, HEAD_LANES), 1)
        o_ref[...] = jnp.where(lane < HALF, acc_ref[0:DEC_SEQ, :], acc_ref[DEC_SEQ:2 * DEC_SEQ, :]).astype(BF16)

    return [logits, block_sums, new_values, last_values, earlier]


def kernel(x_prompt, x_sample, cache_diff_k, cache_diff_v, cache_sb_k, cache_sb_v, norm_attn_g, w_in,
           lambda_q1, lambda_k1, lambda_q2, lambda_k2, diff_subln_g, w_out, norm_ffn_g, w_gate, w_up, w_down,
           norm_final_g):
    batch, seq, _ = x_prompt.shape
    dec_batch, dec_seq, _ = x_sample.shape
    assert seq == SEQ and dec_seq == DEC_SEQ and cache_diff_k.shape[2] == PAST_LEN and w_in.shape[0] == 1

    w_in16 = w_in[0].astype(BF16)
    w_sb_t16 = jnp.transpose(w_in16[:, 4 * GROUP_WIDTH:])
    w_out16 = w_out[0].astype(BF16)
    w_gate16 = w_gate[0].astype(BF16)
    w_up16 = w_up[0].astype(BF16)
    w_down16 = w_down[0].astype(BF16)
    g_attn = norm_attn_g[0].reshape(1, D_MODEL)
    g_ffn = norm_ffn_g[0].reshape(1, D_MODEL)
    g_final = norm_final_g.reshape(1, D_MODEL)
    g_subln = diff_subln_g[0].reshape(1, HEAD_LANES)
    lamv = jnp.concatenate([lambda_q1, lambda_k1, lambda_q2, lambda_k2], axis=0).astype(F32)
    slopes = jnp.exp2(-8.0 / DIFF_HEADS * jnp.arange(1, DIFF_HEADS + 1, dtype=F32))

    def ffn(x, od, osb):
        return _ffn(x, od, osb, w_out16, g_ffn, w_gate16, w_up16, w_down16, g_final)

    xs = x_sample.reshape(dec_batch * DEC_SEQ, D_MODEL)
    qd2, qs2, kd2, vd2, ks2, vs2, kd2_16, vd2_16, ks2_16, vs2_16 = _inproj(xs, g_attn, w_in16, w_sb_t16)
    keys_minor = lambda a: jnp.transpose(a[0], (0, 2, 3, 1)).reshape(dec_batch, GROUP_WIDTH, PAST_LEN)

    xp = x_prompt.reshape(batch * SEQ, D_MODEL)
    qd, qs, kd, vd, kst, vst, kd16, vd16, kst16, vst16, osb2 = _inproj(
        xp, g_attn, w_in16, w_sb_t16, (qs2, ks2_16, vs2_16, keys_minor(cache_sb_k), keys_minor(cache_sb_v)))
    osb = _sb_prompt(qs, kst16, vst16)
    y_prompt = _diff_ffn(slopes, lamv, g_subln, qd, kd16, vd16, xp, osb, w_out16, g_ffn, w_gate16, w_up16, w_down16,
                         g_final).reshape(batch, SEQ, D_MODEL)

    head_major = lambda a: a[0].reshape(dec_batch, PAST_LEN * DIFF_HEADS, HEAD_LANES)
    od2 = _diff_sample(slopes, lamv, g_subln, qd2, kd2_16, vd2_16, head_major(cache_diff_k), head_major(cache_diff_v))
    y_sample = ffn(xs, od2, osb2).reshape(dec_batch, DEC_SEQ, D_MODEL)

    diff_shape = lambda b, t: (1, b, t, DIFF_HEADS, HEAD_LANES)
    sb_shape = lambda b, t: (1, b, t, 2 * SB_PAIRS, HALF)
    from_keys_minor = lambda a: jnp.transpose(a.reshape(batch, 2 * SB_PAIRS, HALF, SEQ), (0, 3, 1, 2))[None]
    return (y_prompt, y_sample,
            kd.reshape(diff_shape(batch, SEQ)), vd.reshape(diff_shape(batch, SEQ)),
            from_keys_minor(kst), from_keys_minor(vst),
            kd2.reshape(diff_shape(dec_batch, DEC_SEQ)), vd2.reshape(diff_shape(dec_batch, DEC_SEQ)),
            ks2.reshape(sb_shape(dec_batch, DEC_SEQ)), vs2.reshape(sb_shape(dec_batch, DEC_SEQ)))
```

```python
import functools
import math

import jax
import jax.numpy as jnp
from jax import lax
from jax.experimental import pallas as pl
from jax.experimental.pallas import tpu as pltpu

D_MODEL = 1024
SEQ = 2048
DEC_SEQ = 64
PAST_LEN = 4096
CHUNK = 64
GROUP_WIDTH = 512
N_GROUPS = 6
DIFF_HEADS = 4
HEAD_LANES = 128
HALF = 64
SB_PAIRS = 4
D_FF = 2816
FF_CHUNK = 256
RMS_EPS = 1e-6
NEG_INF = -1e30
LOG2E = math.log2(math.e)
QK_SCALE = 0.125 * LOG2E
LAMBDA_INIT = 0.8 - 0.6 * math.exp(-0.3 * 0)

ROW_TILE = 512
TQ = 256
TK = TQ
CUM_BLOCK = 256
SB_BLOCK = CUM_BLOCK
SAMPLE_SWEEP = 512
EXIT_LOG2 = -160.0
STAT_LANES = 128
VMEM_LIMIT = 56 * 1024 * 1024
FUSED_VMEM_LIMIT = 58 * 1024 * 1024

F32 = jnp.float32
BF16 = jnp.bfloat16
BF16_ROWS = 16
NT_DIMS = (((1,), (1,)), ((), ()))
TN_DIMS = (((0,), (0,)), ((), ()))


def _rms(x, g):
    return x * lax.rsqrt(jnp.mean(x * x, axis=-1, keepdims=True) + RMS_EPS) * g


def _lambda(lamv_ref):
    lv = lamv_ref[...]
    a = jnp.sum(lv[0:1] * lv[1:2], axis=-1, keepdims=True)
    b = jnp.sum(lv[2:3] * lv[3:4], axis=-1, keepdims=True)
    return jnp.exp(a) - jnp.exp(b) + LAMBDA_INIT


def _log2_keep(z2):
    nz = -z2
    return jnp.minimum(nz, 0.0) - jnp.log(1.0 + jnp.exp2(jnp.minimum(z2, nz))) * LOG2E


def _lanes(x, width):
    if width <= STAT_LANES:
        return x[:, :width]
    return jnp.concatenate([x] * (width // STAT_LANES), axis=1)


def _suffix_matrix(n):
    j = lax.broadcasted_iota(jnp.int32, (n, n), 0)
    s = lax.broadcasted_iota(jnp.int32, (n, n), 1)
    return jnp.where(j > s, 1.0, 0.0).astype(BF16)


def _stick_block(z, suffix, visible):
    lk = _log2_keep(z)
    if visible is not None:
        lk = jnp.where(visible, lk, 0.0)
    within = jnp.dot(lk.astype(BF16), suffix, preferred_element_type=F32)
    return (z + lk) + within, jnp.sum(lk, axis=-1, keepdims=True)


def _stick_weights(log2w, carry, visible):
    a = jnp.exp2(log2w + _lanes(carry, log2w.shape[-1]))
    return a if visible is None else jnp.where(visible, a, 0.0)


def _inproj_slots(x_ref, g_ref, w_ref, wt_ref, qd_ref, kd_ref, vd_ref, kd16_ref, vd16_ref, put_qs, put_ks, put_vs,
                  sample):
    state = {}

    def proj(c):
        return jnp.dot(state["n"], w_ref[:, c * GROUP_WIDTH:(c + 1) * GROUP_WIDTH], preferred_element_type=F32)

    def diff_q():
        state["n"] = _rms(x_ref[...], g_ref[...]).astype(BF16)
        qd_ref[...] = (proj(0) * QK_SCALE).astype(BF16)
        sample[0]()

    def sb_q():
        put_qs((proj(3) * QK_SCALE).astype(BF16))
        sample[1]()

    def diff_kv(c, out_ref, out16_ref, piece):
        def run():
            r = proj(c)
            out16_ref[...] = r.astype(BF16)
            for h in range(DIFF_HEADS):
                out_ref[pl.ds(h, ROW_TILE, stride=DIFF_HEADS), :] = r[:, h * HEAD_LANES:(h + 1) * HEAD_LANES]
            piece()
        return run

    return [diff_q, sb_q, diff_kv(1, kd_ref, kd16_ref, sample[2]), diff_kv(2, vd_ref, vd16_ref, sample[3]),
            lambda: put_ks(proj, state), lambda: put_vs(proj, state), lambda: None, sample[4]]


def _inproj_sample_kernel(x_ref, g_ref, w_ref, qd_ref, qs_ref, kd_ref, vd_ref, ks_ref, vs_ref,
                          kd16_ref, vd16_ref, ks16_ref, vs16_ref):
    def put_qs(q):
        qs_ref[...] = q

    def put(c, out_ref, out16_ref):
        def run(proj, state):
            r = proj(c)
            out_ref[...] = r
            out16_ref[...] = r.astype(BF16)
        return run

    for slot in _inproj_slots(x_ref, g_ref, w_ref, None, qd_ref, kd_ref, vd_ref, kd16_ref, vd16_ref, put_qs,
                              put(4, ks_ref, ks16_ref), put(5, vs_ref, vs16_ref), [lambda: None] * 5):
        slot()


def _inproj_sample(x, g, w16):
    rows = x.shape[0]
    row_spec = lambda width: pl.BlockSpec((ROW_TILE, width), lambda i: (i, 0))
    head_major = pl.BlockSpec((ROW_TILE * DIFF_HEADS, HEAD_LANES), lambda i: (i, 0))
    bf16_rows = jax.ShapeDtypeStruct((rows, GROUP_WIDTH), BF16)
    f32_rows = jax.ShapeDtypeStruct((rows, GROUP_WIDTH), F32)
    f32_head_major = jax.ShapeDtypeStruct((rows * DIFF_HEADS, HEAD_LANES), F32)
    return pl.pallas_call(
        _inproj_sample_kernel,
        grid=(rows // ROW_TILE,),
        in_specs=[row_spec(D_MODEL), pl.BlockSpec((1, D_MODEL), lambda i: (0, 0)),
                  pl.BlockSpec(w16.shape, lambda i: (0, 0))],
        out_specs=[row_spec(GROUP_WIDTH), row_spec(GROUP_WIDTH), head_major, head_major, row_spec(GROUP_WIDTH),
                   row_spec(GROUP_WIDTH), row_spec(GROUP_WIDTH), row_spec(GROUP_WIDTH), row_spec(GROUP_WIDTH),
                   row_spec(GROUP_WIDTH)],
        out_shape=[bf16_rows, bf16_rows, f32_head_major, f32_head_major, f32_rows, f32_rows,
                   bf16_rows, bf16_rows, bf16_rows, bf16_rows],
        compiler_params=pltpu.CompilerParams(dimension_semantics=("parallel",), vmem_limit_bytes=VMEM_LIMIT),
        name="inproj_sample",
    )(x, g, w16)


def _ffn_pieces(n_pieces, x_ref, od, osb_ref, wout_ref, gffn_ref, wg_ref, wu_ref, wd_ref, gfin_ref, y_ref, act_ref):
    state = {}
    n_chunks = D_FF // FF_CHUNK

    def first():
        mixed = jnp.concatenate([od(), osb_ref[...]], axis=-1)
        state["h"] = x_ref[...] + jnp.dot(mixed, wout_ref[...], preferred_element_type=F32)
        state["n"] = _rms(state["h"], gffn_ref[...]).astype(BF16)

    def chunks(lo, hi):
        def run():
            for c in range(lo, hi):
                cols = slice(c * FF_CHUNK, (c + 1) * FF_CHUNK)
                gate = jnp.dot(state["n"], wg_ref[:, cols], preferred_element_type=F32)
                up = jnp.dot(state["n"], wu_ref[:, cols], preferred_element_type=F32)
                act_ref[:, cols] = (gate / (1.0 + jnp.exp(-gate)) * up).astype(BF16)
        return run

    def last():
        y = state["h"] + jnp.dot(act_ref[...], wd_ref[...], preferred_element_type=F32)
        y_ref[...] = _rms(y, gfin_ref[...])

    middle = n_pieces - 2
    bounds = [(n_chunks * m) // middle for m in range(middle + 1)]
    return [first] + [chunks(bounds[m], bounds[m + 1]) for m in range(middle)] + [last]


def _ffn_kernel(x_ref, od_ref, osb_ref, wout_ref, gffn_ref, wg_ref, wu_ref, wd_ref, gfin_ref, y_ref, act_ref):
    for piece in _ffn_pieces(3, x_ref, lambda: od_ref[...], osb_ref, wout_ref, gffn_ref, wg_ref, wu_ref, wd_ref,
                             gfin_ref, y_ref, act_ref):
        piece()


def _ffn(x, od, osb, wout16, gffn, wg16, wu16, wd16, gfin):
    rows = x.shape[0]
    row_spec = lambda width: pl.BlockSpec((ROW_TILE, width), lambda i: (i, 0))
    full = lambda a: pl.BlockSpec(a.shape, lambda i: (0, 0))
    return pl.pallas_call(
        _ffn_kernel,
        grid=(rows // ROW_TILE,),
        in_specs=[row_spec(D_MODEL), row_spec(GROUP_WIDTH), row_spec(GROUP_WIDTH),
                  full(wout16), full(gffn), full(wg16), full(wu16), full(wd16), full(gfin)],
        out_specs=row_spec(D_MODEL),
        out_shape=jax.ShapeDtypeStruct((rows, D_MODEL), F32),
        scratch_shapes=[pltpu.VMEM((ROW_TILE, D_FF), BF16)],
        compiler_params=pltpu.CompilerParams(dimension_semantics=("parallel",), vmem_limit_bytes=VMEM_LIMIT),
        name="outproj_ffn",
    )(x, od, osb, wout16, gffn, wg16, wu16, wd16, gfin)


def _subln(o, g):
    o = o * lax.rsqrt(jnp.mean(o * o, axis=-1, keepdims=True) + RMS_EPS)
    return o * g * (1.0 - LAMBDA_INIT)


def _diff_head(slope2, lam, g_ref, q_ref, k_ref, v_ref, bias_ref, emit, between):
    r = lax.broadcasted_iota(jnp.int32, (TQ, TK), 0)
    c = lax.broadcasted_iota(jnp.int32, (TQ, TK), 1)
    bias_ref[...] = jnp.where((c // CHUNK) <= (r // CHUNK),
                              jnp.where(c > r, (-2.0 * slope2) * (c - r).astype(F32), 0.0), NEG_INF)
    lane = lax.broadcasted_iota(jnp.int32, (TQ, HEAD_LANES), 1)
    key_bias = slope2 * lax.broadcasted_iota(jnp.int32, (BF16_ROWS, SEQ), 1).astype(F32)
    row = lax.broadcasted_iota(jnp.int32, (BF16_ROWS, SEQ), 0)
    hi = key_bias.astype(BF16).astype(F32)
    mid = (key_bias - hi).astype(BF16).astype(F32)
    lo = key_bias - hi - mid
    split = jnp.where(row == 0, hi, jnp.where(row == 1, mid, jnp.where(row == 2, lo, 0.0))).astype(BF16)
    kt = jnp.concatenate([jnp.transpose(k_ref[...]), split,
                          jnp.zeros((HEAD_LANES - BF16_ROWS, SEQ), BF16)], axis=0)
    ones = jnp.where(lane < 3, 1.0, 0.0).astype(BF16)

    def scores(i):
        q = q_ref[i * TQ:(i + 1) * TQ, :]
        qms = [jnp.concatenate([jnp.where((lane < HALF) == first, q, jnp.zeros_like(q)), ones], axis=-1)
               for first in (True, False)]
        raw = [jnp.dot(qm, kt[:, :TK * (i + 1)], preferred_element_type=F32) for qm in qms]
        return [jnp.concatenate([s[:, :TK * i], s[:, TK * i:] + bias_ref[...]], axis=-1) if i else
                s + bias_ref[...] for s in raw]

    def attend(i, s):
        e = [jnp.exp2(sm - jnp.max(sm, axis=-1, keepdims=True)) for sm in s]
        l = [jnp.sum(em, axis=-1, keepdims=True) for em in e]
        a = (e[0] - (lam * l[0] / l[1]) * e[1]).astype(BF16)
        o = jnp.dot(a, v_ref[0:TK * (i + 1), :], preferred_element_type=F32) / l[0]
        emit(i, _subln(o, g_ref[...]).astype(BF16))

    n_q = SEQ // TQ
    s_next = scores(0)
    for i in range(n_q):
        s_now = s_next
        if i + 1 < n_q:
            s_next = scores(i + 1)
        between(i)
        attend(i, s_now)


def _diff_ffn_kernel(slope_ref, lamv_ref, gsub_ref, q_ref, k_ref, v_ref, x_ref, osb_ref, wout_ref, gffn_ref,
                     wg_ref, wu_ref, wd_ref, gfin_ref, y_ref, bias_ref, od_buf, act_ref):
    b, h = pl.program_id(0), pl.program_id(1)
    slot = b % 2
    slope2 = slope_ref[h] * LOG2E

    def emit(i, o):
        od_buf[slot, h, i * TQ:(i + 1) * TQ, :] = o

    @pl.when(b == 0)
    def _():
        _diff_head(slope2, _lambda(lamv_ref), gsub_ref, q_ref, k_ref, v_ref, bias_ref, emit, lambda i: None)

    @pl.when(b > 0)
    def _():
        tile_rows = pl.ds(pl.multiple_of(h * ROW_TILE, ROW_TILE), ROW_TILE)
        previous = lambda: jnp.concatenate([od_buf[1 - slot, hh, tile_rows, :] for hh in range(DIFF_HEADS)],
                                           axis=-1)
        pieces = _ffn_pieces(SEQ // TQ, x_ref, previous, osb_ref, wout_ref, gffn_ref, wg_ref, wu_ref, wd_ref,
                             gfin_ref, y_ref, act_ref)
        _diff_head(slope2, _lambda(lamv_ref), gsub_ref, q_ref, k_ref, v_ref, bias_ref, emit,
                   lambda i: pieces[i]())


def _diff_ffn(slopes, lamv, gsub, qd, kd16, vd16, x, osb, wout16, gffn, wg16, wu16, wd16, gfin):
    rows = x.shape[0]
    batch = rows // SEQ
    assert SEQ // ROW_TILE == DIFF_HEADS
    head_spec = pl.BlockSpec((SEQ, HEAD_LANES), lambda b, h: (jnp.minimum(b, batch - 1), h))
    tile = lambda b, h: (jnp.where(b == 0, 0, (b - 1) * DIFF_HEADS + h), 0)
    row_spec = lambda width: pl.BlockSpec((ROW_TILE, width), tile)
    const = lambda a: pl.BlockSpec(a.shape, lambda b, h: (0, 0))
    weight = lambda a: pl.BlockSpec(a.shape, lambda b, h: (0, 0), pipeline_mode=pl.Buffered(1))
    return pl.pallas_call(
        _diff_ffn_kernel,
        grid=(batch + 1, DIFF_HEADS),
        in_specs=[pl.BlockSpec(memory_space=pltpu.SMEM), const(lamv), const(gsub), head_spec, head_spec, head_spec,
                  row_spec(D_MODEL), row_spec(GROUP_WIDTH), weight(wout16), const(gffn), weight(wg16), weight(wu16),
                  weight(wd16), const(gfin)],
        out_specs=row_spec(D_MODEL),
        out_shape=jax.ShapeDtypeStruct((rows, D_MODEL), F32),
        scratch_shapes=[pltpu.VMEM((TQ, TK), F32), pltpu.VMEM((2, DIFF_HEADS, SEQ, HEAD_LANES), BF16),
                        pltpu.VMEM((ROW_TILE, D_FF), BF16)],
        compiler_params=pltpu.CompilerParams(dimension_semantics=("arbitrary", "arbitrary"),
                                             vmem_limit_bytes=FUSED_VMEM_LIMIT),
        name="diff_prompt_ffn",
    )(slopes, lamv, gsub, qd, kd16, vd16, x, osb, wout16, gffn, wg16, wu16, wd16, gfin)


def _sb_prompt_steps(q_ref, kt_ref, vt_ref, o_ref, u_ref, carry_ref, acc_ref):
    u_ref[...] = _suffix_matrix(SB_BLOCK)
    causal = (lax.broadcasted_iota(jnp.int32, (SB_BLOCK, SB_BLOCK), 1)
              < lax.broadcasted_iota(jnp.int32, (SB_BLOCK, SB_BLOCK), 0))
    lane = lax.broadcasted_iota(jnp.int32, (SB_BLOCK, HEAD_LANES), 1)

    def rows_of(i):
        return pl.ds(i * SB_BLOCK, SB_BLOCK)

    def begin(slot, i):
        q = q_ref[rows_of(i), :]
        carry_ref[slot] = jnp.zeros(carry_ref.shape[1:], F32)
        acc_ref[slot] = jnp.zeros(acc_ref.shape[1:], F32)
        return jnp.where(lane < HALF, q, jnp.zeros_like(q)), jnp.where(lane >= HALF, q, jnp.zeros_like(q))

    def add_blocks(items, diagonal, between=(lambda: None, lambda: None)):
        chains = [(n, h, t) for n, item in enumerate(items) for h in range(2) for t in reversed(range(item[3]))]
        mask = {c: causal if diagonal and c[2] == items[c[0]][3] - 1 else None for c in chains}
        z = {(n, h, t): jnp.dot(items[n][1][h], kt_ref[items[n][2] + t], preferred_element_type=F32)
             for n, h, t in chains}
        between[0]()
        part = {c: _stick_block(z[c], u_ref[...], mask[c]) for c in chains}
        between[1]()
        carry = {(n, h): carry_ref[item[0], h] for n, item in enumerate(items) for h in range(2)}
        weights = {}
        for n, h, t in chains:
            weights[n, h, t] = _stick_weights(part[n, h, t][0], carry[n, h], mask[n, h, t]).astype(BF16)
            carry[n, h] = carry[n, h] + part[n, h, t][1]
        for n, (slot, _, first, count) in enumerate(items):
            for h in range(2):
                acc = acc_ref[slot, h]
                for t in range(count):
                    acc = acc + lax.dot_general(weights[n, h, t], vt_ref[first + t], NT_DIMS,
                                                preferred_element_type=F32)
                acc_ref[slot, h] = acc
                carry_ref[slot, h] = carry[n, h]

    def finish(slot, qms, i, j):
        def unfinished(state):
            j, top = state
            return jnp.logical_and(j >= 0, top > EXIT_LOG2)

        def earlier(state):
            j, _ = state
            add_blocks([(slot, qms, j, 1)], False)
            return j - 1, jnp.max(carry_ref[slot])

        lax.while_loop(unfinished, earlier, (jnp.int32(j), jnp.max(carry_ref[slot])))
        o_ref[rows_of(i), :] = jnp.where(lane < HALF, acc_ref[slot, 0], acc_ref[slot, 1]).astype(BF16)

    def query_pair(i_a, count_a, i_b, between):
        qms_a, qms_b = begin(0, i_a), begin(1, i_b)
        first_a = i_a - (count_a - 1)
        add_blocks([(0, qms_a, first_a, count_a), (1, qms_b, i_b - 1, 2)], True, between)
        finish(0, qms_a, i_a, first_a - 1)
        finish(1, qms_b, i_b, i_b - 2)

    n_q = SEQ // SB_BLOCK
    pairs = [(0, 1, n_q - 1)] + [(2 * m + 1, 2, 2 * m + 2) for m in range((n_q - 2) // 2)]
    return [functools.partial(query_pair, *p) for p in pairs]


def _inproj_sb_kernel(batch, x_ref, g_ref, w_ref, wt_ref, qs2_ref, kn_ref, vn_ref, kct_ref, vct_ref,
                      qd_ref, kd_ref, vd_ref, ks_ref, vs_ref, kd16_ref, vd16_ref, osb2_ref, osb_ref,
                      scarry_ref, sacc_ref, q_buf, kt_buf, vt_buf, u_ref, carry_ref, acc_ref):
    b, t = pl.program_id(0), pl.program_id(1)
    slot = b % 2
    tile_rows = pl.ds(pl.multiple_of(t * ROW_TILE, ROW_TILE), ROW_TILE)
    blocks_per_tile = ROW_TILE // SB_BLOCK

    def put_qs(q):
        for p in range(SB_PAIRS):
            q_buf[slot, p, tile_rows, :] = q[:, p * HEAD_LANES:(p + 1) * HEAD_LANES]

    def put_transposed(c, out_ref, buf):
        def run(proj, state):
            rows = slice((c - 4) * GROUP_WIDTH, (c - 3) * GROUP_WIDTH)
            r = lax.dot_general(wt_ref[rows, :], state["n"], NT_DIMS, preferred_element_type=F32)
            out_ref[...] = r
            for p in range(SB_PAIRS):
                for j in range(blocks_per_tile):
                    buf[slot, p, t * blocks_per_tile + j] = r[p * HEAD_LANES:(p + 1) * HEAD_LANES,
                                                              j * SB_BLOCK:(j + 1) * SB_BLOCK].astype(BF16)
        return run

    def projection_slots():
        sample = _sb_sample_pieces(qs2_ref, kn_ref, vn_ref, kct_ref, vct_ref, osb2_ref, scarry_ref, sacc_ref)
        return _inproj_slots(x_ref, g_ref, w_ref, wt_ref, qd_ref, kd_ref, vd_ref, kd16_ref, vd16_ref, put_qs,
                             put_transposed(4, ks_ref, kt_buf), put_transposed(5, vs_ref, vt_buf), sample)

    def attention_steps():
        return _sb_prompt_steps(q_buf.at[1 - slot, t], kt_buf.at[1 - slot, t], vt_buf.at[1 - slot, t], osb_ref,
                                u_ref, carry_ref, acc_ref)

    @pl.when(b == 0)
    def _():
        for piece in projection_slots():
            piece()

    @pl.when(jnp.logical_and(b > 0, b < batch))
    def _():
        slots = projection_slots()
        for k, step in enumerate(attention_steps()):
            step((slots[2 * k], slots[2 * k + 1]))

    @pl.when(b == batch)
    def _():
        for step in attention_steps():
            step((lambda: None, lambda: None))


def _inproj_sb(x, g, w16, wt16, qs2, ks2_16, vs2_16, cache_kt, cache_vt):
    rows = x.shape[0]
    batch = rows // SEQ
    tiles = SEQ // ROW_TILE
    assert tiles == SB_PAIRS and cache_kt.shape[0] == batch
    tile = lambda b, t: jnp.where(b == batch, batch * tiles - 1, b * tiles + t)
    row_spec = lambda width: pl.BlockSpec((ROW_TILE, width), lambda b, t: (tile(b, t), 0))
    head_major = pl.BlockSpec((ROW_TILE * DIFF_HEADS, HEAD_LANES), lambda b, t: (tile(b, t), 0))
    weight = lambda a: pl.BlockSpec(a.shape, lambda b, t: (0, 0), pipeline_mode=pl.Buffered(1))
    sb32_spec = pl.BlockSpec((None, GROUP_WIDTH, ROW_TILE), lambda b, t: (tile(b, t) // tiles, 0, tile(b, t) % tiles))
    new_spec = pl.BlockSpec((DEC_SEQ, HEAD_LANES), lambda b, t: (tile(b, t) // tiles, tile(b, t) % tiles))
    cache_spec = pl.BlockSpec((None, HEAD_LANES, PAST_LEN), lambda b, t: (tile(b, t) // tiles, tile(b, t) % tiles, 0))
    pair_spec = pl.BlockSpec((SEQ, HEAD_LANES), lambda b, t: (jnp.maximum(b - 1, 0), jnp.where(b == 0, 0, t)))
    bf16_rows = jax.ShapeDtypeStruct((rows, GROUP_WIDTH), BF16)
    f32_head_major = jax.ShapeDtypeStruct((rows * DIFF_HEADS, HEAD_LANES), F32)
    sb32_shape = jax.ShapeDtypeStruct((batch, GROUP_WIDTH, SEQ), F32)
    n_blocks = SEQ // SB_BLOCK
    return pl.pallas_call(
        functools.partial(_inproj_sb_kernel, batch),
        grid=(batch + 1, tiles),
        in_specs=[row_spec(D_MODEL), pl.BlockSpec((1, D_MODEL), lambda b, t: (0, 0)), weight(w16), weight(wt16),
                  new_spec, new_spec, new_spec, cache_spec, cache_spec],
        out_specs=[row_spec(GROUP_WIDTH), head_major, head_major, sb32_spec, sb32_spec, row_spec(GROUP_WIDTH),
                   row_spec(GROUP_WIDTH), new_spec, pair_spec],
        out_shape=[bf16_rows, f32_head_major, f32_head_major, sb32_shape, sb32_shape, bf16_rows, bf16_rows,
                   jax.ShapeDtypeStruct(qs2.shape, BF16), bf16_rows],
        scratch_shapes=[pltpu.VMEM((HEAD_LANES, STAT_LANES), F32), pltpu.VMEM((HEAD_LANES, HEAD_LANES), F32),
                        pltpu.VMEM((2, SB_PAIRS, SEQ, HEAD_LANES), BF16),
                        pltpu.VMEM((2, SB_PAIRS, n_blocks, HEAD_LANES, SB_BLOCK), BF16),
                        pltpu.VMEM((2, SB_PAIRS, n_blocks, HEAD_LANES, SB_BLOCK), BF16),
                        pltpu.VMEM((SB_BLOCK, SB_BLOCK), BF16), pltpu.VMEM((2, 2, SB_BLOCK, STAT_LANES), F32),
                        pltpu.VMEM((2, 2, SB_BLOCK, HEAD_LANES), F32)],
        compiler_params=pltpu.CompilerParams(dimension_semantics=("arbitrary", "arbitrary"),
                                             vmem_limit_bytes=VMEM_LIMIT),
        name="inproj_sb_prompt",
    )(x, g, w16, wt16, qs2, ks2_16, vs2_16, cache_kt, cache_vt)


def _stack_queries(q):
    lane = lax.broadcasted_iota(jnp.int32, q.shape, 1)
    zero = jnp.zeros_like(q)
    return jnp.concatenate([jnp.where(lane < HALF, q, zero), jnp.where(lane >= HALF, q, zero)], axis=0)


def _diff_sample_kernel(slope_ref, lamv_ref, g_ref, q_ref, kn_ref, vn_ref, kc_ref, vc_ref, o_ref):
    lam = _lambda(lamv_ref)
    kpos = lax.broadcasted_iota(jnp.int32, (PAST_LEN, HEAD_LANES), 0)
    qpos = PAST_LEN + (lax.broadcasted_iota(jnp.int32, (PAST_LEN, HEAD_LANES), 1) % HALF)
    dist_c = (qpos - kpos).astype(F32)
    kn_pos = lax.broadcasted_iota(jnp.int32, (DEC_SEQ, HEAD_LANES), 0)
    qn_pos = lax.broadcasted_iota(jnp.int32, (DEC_SEQ, HEAD_LANES), 1) % HALF
    dist_n = jnp.abs(qn_pos - kn_pos).astype(F32)
    cols = [slice(h * HEAD_LANES, (h + 1) * HEAD_LANES) for h in range(DIFF_HEADS)]
    head_rows = lambda ref, h: ref[pl.ds(h, PAST_LEN, stride=DIFF_HEADS), :].astype(BF16)
    for heads in ((0, 1), (2, 3)):
        s_c, s_n = {}, {}
        for h in heads:
            slope2 = slope_ref[h] * LOG2E
            q2 = _stack_queries(q_ref[:, cols[h]])
            s_c[h] = lax.dot_general(head_rows(kc_ref, h), q2, NT_DIMS, preferred_element_type=F32) - slope2 * dist_c
            s_n[h] = lax.dot_general(kn_ref[:, cols[h]], q2, NT_DIMS, preferred_element_type=F32) - slope2 * dist_n
        p_c, p_n, l = {}, {}, {}
        for h in heads:
            m = jnp.maximum(jnp.max(s_c[h], axis=0, keepdims=True), jnp.max(s_n[h], axis=0, keepdims=True))
            p_c[h] = jnp.exp2(s_c[h] - m)
            p_n[h] = jnp.exp2(s_n[h] - m)
            l[h] = jnp.sum(p_c[h], axis=0, keepdims=True) + jnp.sum(p_n[h], axis=0, keepdims=True)
        for h in heads:
            acc = (lax.dot_general(p_c[h].astype(BF16), head_rows(vc_ref, h), TN_DIMS, preferred_element_type=F32)
                   + lax.dot_general(p_n[h].astype(BF16), vn_ref[:, cols[h]], TN_DIMS, preferred_element_type=F32))
            l_col = jnp.transpose(jnp.broadcast_to(l[h], (HEAD_LANES, HEAD_LANES)))[:, 0:1]
            o = acc / l_col
            o = o[0:DEC_SEQ] - lam * o[DEC_SEQ:2 * DEC_SEQ]
            o_ref[:, cols[h]] = _subln(o, g_ref[...]).astype(BF16)


def _diff_sample(slopes, lamv, g, qd, kd16, vd16, cache_k, cache_v):
    batch = cache_k.shape[0]
    new_spec = pl.BlockSpec((DEC_SEQ, GROUP_WIDTH), lambda b: (b, 0))
    cache_spec = pl.BlockSpec((None, PAST_LEN * DIFF_HEADS, HEAD_LANES), lambda b: (b, 0, 0))
    return pl.pallas_call(
        _diff_sample_kernel,
        grid=(batch,),
        in_specs=[pl.BlockSpec(memory_space=pltpu.SMEM),
                  pl.BlockSpec(lamv.shape, lambda b: (0, 0)),
                  pl.BlockSpec(g.shape, lambda b: (0, 0)),
                  new_spec, new_spec, new_spec, cache_spec, cache_spec],
        out_specs=new_spec,
        out_shape=jax.ShapeDtypeStruct(qd.shape, BF16),
        compiler_params=pltpu.CompilerParams(dimension_semantics=("parallel",), vmem_limit_bytes=VMEM_LIMIT),
        name="diff_sample",
    )(slopes, lamv, g, qd, kd16, vd16, cache_k, cache_v)


def _sb_sample_pieces(q_ref, kn_ref, vn_ref, kct_ref, vct_ref, o_ref, carry_ref, acc_ref):
    state = {}
    n_sweeps = PAST_LEN // SAMPLE_SWEEP
    n_blocks = SAMPLE_SWEEP // CUM_BLOCK
    last_cols = slice((n_sweeps - 1) * SAMPLE_SWEEP, n_sweeps * SAMPLE_SWEEP)
    key_idx = lax.broadcasted_iota(jnp.int32, (HEAD_LANES, DEC_SEQ), 1)
    query_idx = lax.broadcasted_iota(jnp.int32, (HEAD_LANES, DEC_SEQ), 0) % HALF
    visible = key_idx < query_idx

    def block_parts(z):
        return [_stick_block(z[:, b * CUM_BLOCK:(b + 1) * CUM_BLOCK], state["suffix"], None) for b in range(n_blocks)]

    def add_values(parts, cols):
        carry = carry_ref[...]
        weights = [None] * n_blocks
        for b in reversed(range(n_blocks)):
            weights[b] = _stick_weights(parts[b][0], carry, None)
            carry = carry + parts[b][1]
        a = jnp.concatenate(weights, axis=-1).astype(BF16)
        acc_ref[...] += lax.dot_general(a, vct_ref[:, cols].astype(BF16), NT_DIMS, preferred_element_type=F32)
        carry_ref[...] = carry
        return jnp.max(carry)

    def logits():
        state["q2"] = _stack_queries(q_ref[...])
        state["suffix"] = _suffix_matrix(CUM_BLOCK)
        state["z_new"] = lax.dot_general(state["q2"], kn_ref[...], NT_DIMS, preferred_element_type=F32)
        state["z_last"] = jnp.dot(state["q2"], kct_ref[:, last_cols].astype(BF16), preferred_element_type=F32)

    def block_sums():
        state["new"] = _stick_block(state["z_new"], _suffix_matrix(DEC_SEQ), visible)
        state["last"] = block_parts(state["z_last"])

    def new_values():
        log2w, carry = state["new"]
        a = jnp.where(visible, jnp.exp2(log2w), 0.0)
        acc_ref[...] = jnp.dot(a.astype(BF16), vn_ref[...], preferred_element_type=F32)
        carry_ref[...] = jnp.broadcast_to(carry, carry_ref.shape)

    def last_values():
        state["top"] = add_values(state["last"], last_cols)

    def earlier():
        def unfinished(loop):
            j, top = loop
            return jnp.logical_and(j >= 0, top > EXIT_LOG2)

        def body(loop):
            j, _ = loop
            cols = pl.ds(pl.multiple_of(j * SAMPLE_SWEEP, SAMPLE_SWEEP), SAMPLE_SWEEP)
            z = jnp.dot(state["q2"], kct_ref[:, cols].astype(BF16), preferred_element_type=F32)
            return j - 1, add_values(block_parts(z), cols)

        lax.while_loop(unfinished, body, (n_sweeps - 2, state["top"]))
        lane = lax.broadcasted_iota(jnp.int32, (DEC_SEQ, HEAD_LANES), 1)
        o_ref[...] = jnp.where(lane < HALF, acc_ref[0:DEC_SEQ, :], acc_ref[DEC_SEQ:2 * DEC_SEQ, :]).astype(BF16)

    return [logits, block_sums, new_values, last_values, earlier]


def kernel(x_prompt, x_sample, cache_diff_k, cache_diff_v, cache_sb_k, cache_sb_v, norm_attn_g, w_in,
           lambda_q1, lambda_k1, lambda_q2, lambda_k2, diff_subln_g, w_out, norm_ffn_g, w_gate, w_up, w_down,
           norm_final_g):
    batch, seq, _ = x_prompt.shape
    dec_batch, dec_seq, _ = x_sample.shape
    assert seq == SEQ and dec_seq == DEC_SEQ and cache_diff_k.shape[2] == PAST_LEN and w_in.shape[0] == 1

    w_in16 = w_in[0].astype(BF16)
    w_sb_t16 = jnp.transpose(w_in16[:, 4 * GROUP_WIDTH:])
    w_out16 = w_out[0].astype(BF16)
    w_gate16 = w_gate[0].astype(BF16)
    w_up16 = w_up[0].astype(BF16)
    w_down16 = w_down[0].astype(BF16)
    g_attn = norm_attn_g[0].reshape(1, D_MODEL)
    g_ffn = norm_ffn_g[0].reshape(1, D_MODEL)
    g_final = norm_final_g.reshape(1, D_MODEL)
    g_subln = diff_subln_g[0].reshape(1, HEAD_LANES)
    lamv = jnp.concatenate([lambda_q1, lambda_k1, lambda_q2, lambda_k2], axis=0).astype(F32)
    slopes = jnp.exp2(-8.0 / DIFF_HEADS * jnp.arange(1, DIFF_HEADS + 1, dtype=F32))

    def ffn(x, od, osb):
        return _ffn(x, od, osb, w_out16, g_ffn, w_gate16, w_up16, w_down16, g_final)

    xs = x_sample.reshape(dec_batch * DEC_SEQ, D_MODEL)
    qd2, qs2, kd2, vd2, ks2, vs2, kd2_16, vd2_16, ks2_16, vs2_16 = _inproj_sample(xs, g_attn, w_in16)
    keys_minor = lambda a: jnp.transpose(a[0], (0, 2, 3, 1)).reshape(dec_batch, GROUP_WIDTH, PAST_LEN)

    xp = x_prompt.reshape(batch * SEQ, D_MODEL)
    qd, kd, vd, kst, vst, kd16, vd16, osb2, osb = _inproj_sb(
        xp, g_attn, w_in16, w_sb_t16, qs2, ks2_16, vs2_16, keys_minor(cache_sb_k), keys_minor(cache_sb_v))
    y_prompt = _diff_ffn(slopes, lamv, g_subln, qd, kd16, vd16, xp, osb, w_out16, g_ffn, w_gate16, w_up16, w_down16,
                         g_final).reshape(batch, SEQ, D_MODEL)

    head_major = lambda a: a[0].reshape(dec_batch, PAST_LEN * DIFF_HEADS, HEAD_LANES)
    od2 = _diff_sample(slopes, lamv, g_subln, qd2, kd2_16, vd2_16, head_major(cache_diff_k), head_major(cache_diff_v))
    y_sample = ffn(xs, od2, osb2).reshape(dec_batch, DEC_SEQ, D_MODEL)

    diff_shape = lambda b, t: (1, b, t, DIFF_HEADS, HEAD_LANES)
    sb_shape = lambda b, t: (1, b, t, 2 * SB_PAIRS, HALF)
    from_keys_minor = lambda a: jnp.transpose(a.reshape(batch, 2 * SB_PAIRS, HALF, SEQ), (0, 3, 1, 2))[None]
    return (y_prompt, y_sample,
            kd.reshape(diff_shape(batch, SEQ)), vd.reshape(diff_shape(batch, SEQ)),
            from_keys_minor(kst), from_keys_minor(vst),
            kd2.reshape(diff_shape(dec_batch, DEC_SEQ)), vd2.reshape(diff_shape(dec_batch, DEC_SEQ)),
            ks2.reshape(sb_shape(dec_batch, DEC_SEQ)), vs2.reshape(sb_shape(dec_batch, DEC_SEQ)))
```

```python
import functools
import math

import jax
import jax.numpy as jnp
from jax import lax
from jax.experimental import pallas as pl
from jax.experimental.pallas import tpu as pltpu

D_MODEL = 1024
SEQ = 2048
DEC_SEQ = 64
PAST_LEN = 4096
CHUNK = 64
GROUP_WIDTH = 512
N_GROUPS = 6
DIFF_HEADS = 4
HEAD_LANES = 128
HALF = 64
SB_PAIRS = 4
D_FF = 2816
FF_CHUNK = 256
RMS_EPS = 1e-6
NEG_INF = -1e30
LOG2E = math.log2(math.e)
QK_SCALE = 0.125 * LOG2E
LAMBDA_INIT = 0.8 - 0.6 * math.exp(-0.3 * 0)

ROW_TILE = 512
TQ = 256
TK = TQ
CUM_BLOCK = 256
SB_BLOCK = CUM_BLOCK
SAMPLE_SWEEP = 512
EXIT_LOG2 = -160.0
STAT_LANES = 128
VMEM_LIMIT = 56 * 1024 * 1024
FUSED_VMEM_LIMIT = 58 * 1024 * 1024

F32 = jnp.float32
BF16 = jnp.bfloat16
BF16_ROWS = 16
NT_DIMS = (((1,), (1,)), ((), ()))
TN_DIMS = (((0,), (0,)), ((), ()))


def _rms(x, g):
    return x * lax.rsqrt(jnp.mean(x * x, axis=-1, keepdims=True) + RMS_EPS) * g


def _lambda(lamv_ref):
    lv = lamv_ref[...]
    a = jnp.sum(lv[0:1] * lv[1:2], axis=-1, keepdims=True)
    b = jnp.sum(lv[2:3] * lv[3:4], axis=-1, keepdims=True)
    return jnp.exp(a) - jnp.exp(b) + LAMBDA_INIT


def _log2_keep(z2):
    nz = -z2
    return jnp.minimum(nz, 0.0) - jnp.log(1.0 + jnp.exp2(jnp.minimum(z2, nz))) * LOG2E


def _lanes(x, width):
    if width <= STAT_LANES:
        return x[:, :width]
    return jnp.concatenate([x] * (width // STAT_LANES), axis=1)


def _suffix_matrix(n):
    j = lax.broadcasted_iota(jnp.int32, (n, n), 0)
    s = lax.broadcasted_iota(jnp.int32, (n, n), 1)
    return jnp.where(j > s, 1.0, 0.0).astype(BF16)


def _stick_block(z, suffix, visible):
    lk = _log2_keep(z)
    if visible is not None:
        lk = jnp.where(visible, lk, 0.0)
    within = jnp.dot(lk.astype(BF16), suffix, preferred_element_type=F32)
    return (z + lk) + within, jnp.sum(lk, axis=-1, keepdims=True)


def _stick_weights(log2w, carry, visible):
    a = jnp.exp2(log2w + _lanes(carry, log2w.shape[-1]))
    return a if visible is None else jnp.where(visible, a, 0.0)


def _inproj_slots(x_ref, g_ref, w_ref, wt_ref, qd_ref, kd_ref, vd_ref, kd16_ref, vd16_ref, put_qs, put_ks, put_vs,
                  sample):
    state = {}

    def proj(c):
        return jnp.dot(state["n"], w_ref[:, c * GROUP_WIDTH:(c + 1) * GROUP_WIDTH], preferred_element_type=F32)

    def diff_q():
        state["n"] = _rms(x_ref[...], g_ref[...]).astype(BF16)
        qd_ref[...] = (proj(0) * QK_SCALE).astype(BF16)
        sample[0]()

    def sb_q():
        put_qs((proj(3) * QK_SCALE).astype(BF16))
        sample[1]()

    def diff_kv(c, out_ref, out16_ref, piece):
        def run():
            r = proj(c)
            out16_ref[...] = r.astype(BF16)
            for h in range(DIFF_HEADS):
                out_ref[pl.ds(h, ROW_TILE, stride=DIFF_HEADS), :] = r[:, h * HEAD_LANES:(h + 1) * HEAD_LANES]
            piece()
        return run

    return [diff_q, sb_q, diff_kv(1, kd_ref, kd16_ref, sample[2]), diff_kv(2, vd_ref, vd16_ref, sample[3]),
            lambda: put_ks(proj, state), lambda: put_vs(proj, state), lambda: None, sample[4]]


def _inproj_sample_kernel(x_ref, g_ref, w_ref, qd_ref, qs_ref, kd_ref, vd_ref, ks_ref, vs_ref,
                          kd16_ref, vd16_ref, ks16_ref, vs16_ref):
    def put_qs(q):
        qs_ref[...] = q

    def put(c, out_ref, out16_ref):
        def run(proj, state):
            r = proj(c)
            out_ref[...] = r
            out16_ref[...] = r.astype(BF16)
        return run

    for slot in _inproj_slots(x_ref, g_ref, w_ref, None, qd_ref, kd_ref, vd_ref, kd16_ref, vd16_ref, put_qs,
                              put(4, ks_ref, ks16_ref), put(5, vs_ref, vs16_ref), [lambda: None] * 5):
        slot()


def _inproj_sample(x, g, w16):
    rows = x.shape[0]
    row_spec = lambda width: pl.BlockSpec((ROW_TILE, width), lambda i: (i, 0))
    head_major = pl.BlockSpec((ROW_TILE * DIFF_HEADS, HEAD_LANES), lambda i: (i, 0))
    bf16_rows = jax.ShapeDtypeStruct((rows, GROUP_WIDTH), BF16)
    f32_rows = jax.ShapeDtypeStruct((rows, GROUP_WIDTH), F32)
    f32_head_major = jax.ShapeDtypeStruct((rows * DIFF_HEADS, HEAD_LANES), F32)
    return pl.pallas_call(
        _inproj_sample_kernel,
        grid=(rows // ROW_TILE,),
        in_specs=[row_spec(D_MODEL), pl.BlockSpec((1, D_MODEL), lambda i: (0, 0)),
                  pl.BlockSpec(w16.shape, lambda i: (0, 0))],
        out_specs=[row_spec(GROUP_WIDTH), row_spec(GROUP_WIDTH), head_major, head_major, row_spec(GROUP_WIDTH),
                   row_spec(GROUP_WIDTH), row_spec(GROUP_WIDTH), row_spec(GROUP_WIDTH), row_spec(GROUP_WIDTH),
                   row_spec(GROUP_WIDTH)],
        out_shape=[bf16_rows, bf16_rows, f32_head_major, f32_head_major, f32_rows, f32_rows,
                   bf16_rows, bf16_rows, bf16_rows, bf16_rows],
        compiler_params=pltpu.CompilerParams(dimension_semantics=("parallel",), vmem_limit_bytes=VMEM_LIMIT),
        name="inproj_sample",
    )(x, g, w16)


def _ffn_pieces(n_pieces, x_ref, od, osb_ref, wout_ref, gffn_ref, wg_ref, wu_ref, wd_ref, gfin_ref, y_ref, act_ref):
    state = {}
    n_chunks = D_FF // FF_CHUNK

    def first():
        mixed = jnp.concatenate([od(), osb_ref[...]], axis=-1)
        state["h"] = x_ref[...] + jnp.dot(mixed, wout_ref[...], preferred_element_type=F32)
        state["n"] = _rms(state["h"], gffn_ref[...]).astype(BF16)

    def chunks(lo, hi):
        def run():
            for c in range(lo, hi):
                cols = slice(c * FF_CHUNK, (c + 1) * FF_CHUNK)
                gate = jnp.dot(state["n"], wg_ref[:, cols], preferred_element_type=F32)
                up = jnp.dot(state["n"], wu_ref[:, cols], preferred_element_type=F32)
                act_ref[:, cols] = (gate / (1.0 + jnp.exp(-gate)) * up).astype(BF16)
        return run

    def last():
        y = state["h"] + jnp.dot(act_ref[...], wd_ref[...], preferred_element_type=F32)
        y_ref[...] = _rms(y, gfin_ref[...])

    middle = n_pieces - 2
    bounds = [(n_chunks * m) // middle for m in range(middle + 1)]
    return [first] + [chunks(bounds[m], bounds[m + 1]) for m in range(middle)] + [last]


def _ffn_kernel(x_ref, od_ref, osb_ref, wout_ref, gffn_ref, wg_ref, wu_ref, wd_ref, gfin_ref, y_ref, act_ref):
    for piece in _ffn_pieces(3, x_ref, lambda: od_ref[...], osb_ref, wout_ref, gffn_ref, wg_ref, wu_ref, wd_ref,
                             gfin_ref, y_ref, act_ref):
        piece()


def _ffn(x, od, osb, wout16, gffn, wg16, wu16, wd16, gfin):
    rows = x.shape[0]
    row_spec = lambda width: pl.BlockSpec((ROW_TILE, width), lambda i: (i, 0))
    full = lambda a: pl.BlockSpec(a.shape, lambda i: (0, 0))
    return pl.pallas_call(
        _ffn_kernel,
        grid=(rows // ROW_TILE,),
        in_specs=[row_spec(D_MODEL), row_spec(GROUP_WIDTH), row_spec(GROUP_WIDTH),
                  full(wout16), full(gffn), full(wg16), full(wu16), full(wd16), full(gfin)],
        out_specs=row_spec(D_MODEL),
        out_shape=jax.ShapeDtypeStruct((rows, D_MODEL), F32),
        scratch_shapes=[pltpu.VMEM((ROW_TILE, D_FF), BF16)],
        compiler_params=pltpu.CompilerParams(dimension_semantics=("parallel",), vmem_limit_bytes=VMEM_LIMIT),
        name="outproj_ffn",
    )(x, od, osb, wout16, gffn, wg16, wu16, wd16, gfin)


def _subln(o, g):
    o = o * lax.rsqrt(jnp.mean(o * o, axis=-1, keepdims=True) + RMS_EPS)
    return o * g * (1.0 - LAMBDA_INIT)


def _diff_head(slope2, lam, g_ref, q_ref, k_ref, v_ref, bias_ref, emit, between):
    r = lax.broadcasted_iota(jnp.int32, (TQ, TK), 0)
    c = lax.broadcasted_iota(jnp.int32, (TQ, TK), 1)
    bias_ref[...] = jnp.where((c // CHUNK) <= (r // CHUNK),
                              jnp.where(c > r, (-2.0 * slope2) * (c - r).astype(F32), 0.0), NEG_INF)
    lane = lax.broadcasted_iota(jnp.int32, (TQ, HEAD_LANES), 1)
    key_bias = slope2 * lax.broadcasted_iota(jnp.int32, (BF16_ROWS, SEQ), 1).astype(F32)
    row = lax.broadcasted_iota(jnp.int32, (BF16_ROWS, SEQ), 0)
    hi = key_bias.astype(BF16).astype(F32)
    mid = (key_bias - hi).astype(BF16).astype(F32)
    lo = key_bias - hi - mid
    split = jnp.where(row == 0, hi, jnp.where(row == 1, mid, jnp.where(row == 2, lo, 0.0))).astype(BF16)
    kt = jnp.concatenate([jnp.transpose(k_ref[...]), split,
                          jnp.zeros((HEAD_LANES - BF16_ROWS, SEQ), BF16)], axis=0)
    ones = jnp.where(lane < 3, 1.0, 0.0).astype(BF16)

    def scores(i):
        q = q_ref[i * TQ:(i + 1) * TQ, :]
        qms = [jnp.concatenate([jnp.where((lane < HALF) == first, q, jnp.zeros_like(q)), ones], axis=-1)
               for first in (True, False)]
        raw = [jnp.dot(qm, kt[:, :TK * (i + 1)], preferred_element_type=F32) for qm in qms]
        return [jnp.concatenate([s[:, :TK * i], s[:, TK * i:] + bias_ref[...]], axis=-1) if i else
                s + bias_ref[...] for s in raw]

    def attend(i, s):
        e = [jnp.exp2(sm - jnp.max(sm, axis=-1, keepdims=True)) for sm in s]
        l = [jnp.sum(em, axis=-1, keepdims=True) for em in e]
        a = (e[0] - (lam * l[0] / l[1]) * e[1]).astype(BF16)
        o = jnp.dot(a, v_ref[0:TK * (i + 1), :], preferred_element_type=F32) / l[0]
        emit(i, _subln(o, g_ref[...]).astype(BF16))

    n_q = SEQ // TQ
    s_next = scores(0)
    for i in range(n_q):
        s_now = s_next
        if i + 1 < n_q:
            s_next = scores(i + 1)
        between(i)
        attend(i, s_now)


def _diff_ffn_kernel(batch, slope_ref, lamv_ref, gsub_ref, q_ref, k_ref, v_ref, x_ref, osb_ref, wout_ref,
                     gffn_ref, wg_ref, wu_ref, wd_ref, gfin_ref, y_ref, bias_ref, od_buf, act_ref):
    b, h = pl.program_id(0), pl.program_id(1)
    slot = b % 2
    slope2 = slope_ref[h] * LOG2E

    def emit(i, o):
        od_buf[slot, h, i * TQ:(i + 1) * TQ, :] = o

    @pl.when(b == 0)
    def _():
        _diff_head(slope2, _lambda(lamv_ref), gsub_ref, q_ref, k_ref, v_ref, bias_ref, emit, lambda i: None)

    def ffn_pieces():
        tile_rows = pl.ds(pl.multiple_of(h * ROW_TILE, ROW_TILE), ROW_TILE)
        previous = lambda: jnp.concatenate([od_buf[1 - slot, hh, tile_rows, :] for hh in range(DIFF_HEADS)],
                                           axis=-1)
        return _ffn_pieces(SEQ // TQ, x_ref, previous, osb_ref, wout_ref, gffn_ref, wg_ref, wu_ref, wd_ref,
                           gfin_ref, y_ref, act_ref)

    @pl.when(jnp.logical_and(b > 0, b < batch))
    def _():
        pieces = ffn_pieces()
        _diff_head(slope2, _lambda(lamv_ref), gsub_ref, q_ref, k_ref, v_ref, bias_ref, emit,
                   lambda i: pieces[i]())

    @pl.when(b == batch)
    def _():
        for piece in ffn_pieces():
            piece()


def _diff_ffn(slopes, lamv, gsub, qd, kd16, vd16, x, osb, wout16, gffn, wg16, wu16, wd16, gfin):
    rows = x.shape[0]
    batch = rows // SEQ
    assert SEQ // ROW_TILE == DIFF_HEADS
    head_spec = pl.BlockSpec((SEQ, HEAD_LANES), lambda b, h: (jnp.minimum(b, batch - 1), h))
    tile = lambda b, h: (jnp.where(b == 0, 0, (b - 1) * DIFF_HEADS + h), 0)
    row_spec = lambda width: pl.BlockSpec((ROW_TILE, width), tile)
    const = lambda a: pl.BlockSpec(a.shape, lambda b, h: (0, 0))
    weight = lambda a: pl.BlockSpec(a.shape, lambda b, h: (0, 0), pipeline_mode=pl.Buffered(1))
    return pl.pallas_call(
        functools.partial(_diff_ffn_kernel, batch),
        grid=(batch + 1, DIFF_HEADS),
        in_specs=[pl.BlockSpec(memory_space=pltpu.SMEM), const(lamv), const(gsub), head_spec, head_spec, head_spec,
                  row_spec(D_MODEL), row_spec(GROUP_WIDTH), weight(wout16), const(gffn), weight(wg16), weight(wu16),
                  weight(wd16), const(gfin)],
        out_specs=row_spec(D_MODEL),
        out_shape=jax.ShapeDtypeStruct((rows, D_MODEL), F32),
        scratch_shapes=[pltpu.VMEM((TQ, TK), F32), pltpu.VMEM((2, DIFF_HEADS, SEQ, HEAD_LANES), BF16),
                        pltpu.VMEM((ROW_TILE, D_FF), BF16)],
        compiler_params=pltpu.CompilerParams(dimension_semantics=("arbitrary", "arbitrary"),
                                             vmem_limit_bytes=FUSED_VMEM_LIMIT),
        name="diff_prompt_ffn",
    )(slopes, lamv, gsub, qd, kd16, vd16, x, osb, wout16, gffn, wg16, wu16, wd16, gfin)


def _sb_prompt_steps(q_ref, kt_ref, vt_ref, o_ref, u_ref, carry_ref, acc_ref):
    u_ref[...] = _suffix_matrix(SB_BLOCK)
    causal = (lax.broadcasted_iota(jnp.int32, (SB_BLOCK, SB_BLOCK), 1)
              < lax.broadcasted_iota(jnp.int32, (SB_BLOCK, SB_BLOCK), 0))
    lane = lax.broadcasted_iota(jnp.int32, (SB_BLOCK, HEAD_LANES), 1)

    def rows_of(i):
        return pl.ds(i * SB_BLOCK, SB_BLOCK)

    def begin(slot, i):
        q = q_ref[rows_of(i), :]
        carry_ref[slot] = jnp.zeros(carry_ref.shape[1:], F32)
        acc_ref[slot] = jnp.zeros(acc_ref.shape[1:], F32)
        return jnp.where(lane < HALF, q, jnp.zeros_like(q)), jnp.where(lane >= HALF, q, jnp.zeros_like(q))

    def add_blocks(items, diagonal, between=(lambda: None, lambda: None)):
        chains = [(n, h, t) for n, item in enumerate(items) for h in range(2) for t in reversed(range(item[3]))]
        mask = {c: causal if diagonal and c[2] == items[c[0]][3] - 1 else None for c in chains}
        z = {(n, h, t): jnp.dot(items[n][1][h], kt_ref[items[n][2] + t], preferred_element_type=F32)
             for n, h, t in chains}
        between[0]()
        part = {c: _stick_block(z[c], u_ref[...], mask[c]) for c in chains}
        between[1]()
        carry = {(n, h): carry_ref[item[0], h] for n, item in enumerate(items) for h in range(2)}
        weights = {}
        for n, h, t in chains:
            weights[n, h, t] = _stick_weights(part[n, h, t][0], carry[n, h], mask[n, h, t]).astype(BF16)
            carry[n, h] = carry[n, h] + part[n, h, t][1]
        for n, (slot, _, first, count) in enumerate(items):
            for h in range(2):
                acc = acc_ref[slot, h]
                for t in range(count):
                    acc = acc + lax.dot_general(weights[n, h, t], vt_ref[first + t], NT_DIMS,
                                                preferred_element_type=F32)
                acc_ref[slot, h] = acc
                carry_ref[slot, h] = carry[n, h]

    def finish(slot, qms, i, j):
        def unfinished(state):
            j, top = state
            return jnp.logical_and(j >= 0, top > EXIT_LOG2)

        def earlier(state):
            j, _ = state
            add_blocks([(slot, qms, j, 1)], False)
            return j - 1, jnp.max(carry_ref[slot])

        lax.while_loop(unfinished, earlier, (jnp.int32(j), jnp.max(carry_ref[slot])))
        o_ref[rows_of(i), :] = jnp.where(lane < HALF, acc_ref[slot, 0], acc_ref[slot, 1]).astype(BF16)

    def query_pair(i_a, count_a, i_b, between):
        qms_a, qms_b = begin(0, i_a), begin(1, i_b)
        first_a = i_a - (count_a - 1)
        add_blocks([(0, qms_a, first_a, count_a), (1, qms_b, i_b - 1, 2)], True, between)
        finish(0, qms_a, i_a, first_a - 1)
        finish(1, qms_b, i_b, i_b - 2)

    n_q = SEQ // SB_BLOCK
    pairs = [(0, 1, n_q - 1)] + [(2 * m + 1, 2, 2 * m + 2) for m in range((n_q - 2) // 2)]
    return [functools.partial(query_pair, *p) for p in pairs]


def _inproj_sb_kernel(batch, x_ref, g_ref, w_ref, wt_ref, qs2_ref, kn_ref, vn_ref, kct_ref, vct_ref,
                      qd_ref, kd_ref, vd_ref, ks_ref, vs_ref, kd16_ref, vd16_ref, osb2_ref, osb_ref,
                      scarry_ref, sacc_ref, q_buf, kt_buf, vt_buf, u_ref, carry_ref, acc_ref):
    b, t = pl.program_id(0), pl.program_id(1)
    slot = b % 2
    tile_rows = pl.ds(pl.multiple_of(t * ROW_TILE, ROW_TILE), ROW_TILE)
    blocks_per_tile = ROW_TILE // SB_BLOCK

    def put_qs(q):
        for p in range(SB_PAIRS):
            q_buf[slot, p, tile_rows, :] = q[:, p * HEAD_LANES:(p + 1) * HEAD_LANES]

    def put_transposed(c, out_ref, buf):
        def run(proj, state):
            rows = slice((c - 4) * GROUP_WIDTH, (c - 3) * GROUP_WIDTH)
            r = lax.dot_general(wt_ref[rows, :], state["n"], NT_DIMS, preferred_element_type=F32)
            out_ref[...] = r
            for p in range(SB_PAIRS):
                for j in range(blocks_per_tile):
                    buf[slot, p, t * blocks_per_tile + j] = r[p * HEAD_LANES:(p + 1) * HEAD_LANES,
                                                              j * SB_BLOCK:(j + 1) * SB_BLOCK].astype(BF16)
        return run

    def projection_slots():
        sample = _sb_sample_pieces(qs2_ref, kn_ref, vn_ref, kct_ref, vct_ref, osb2_ref, scarry_ref, sacc_ref)
        return _inproj_slots(x_ref, g_ref, w_ref, wt_ref, qd_ref, kd_ref, vd_ref, kd16_ref, vd16_ref, put_qs,
                             put_transposed(4, ks_ref, kt_buf), put_transposed(5, vs_ref, vt_buf), sample)

    def attention_steps():
        return _sb_prompt_steps(q_buf.at[1 - slot, t], kt_buf.at[1 - slot, t], vt_buf.at[1 - slot, t], osb_ref,
                                u_ref, carry_ref, acc_ref)

    @pl.when(b == 0)
    def _():
        for piece in projection_slots():
            piece()

    @pl.when(jnp.logical_and(b > 0, b < batch))
    def _():
        slots = projection_slots()
        for k, step in enumerate(attention_steps()):
            step((slots[2 * k], slots[2 * k + 1]))

    @pl.when(b == batch)
    def _():
        for step in attention_steps():
            step((lambda: None, lambda: None))


def _inproj_sb(x, g, w16, wt16, qs2, ks2_16, vs2_16, cache_kt, cache_vt):
    rows = x.shape[0]
    batch = rows // SEQ
    tiles = SEQ // ROW_TILE
    assert tiles == SB_PAIRS and cache_kt.shape[0] == batch
    tile = lambda b, t: jnp.where(b == batch, batch * tiles - 1, b * tiles + t)
    row_spec = lambda width: pl.BlockSpec((ROW_TILE, width), lambda b, t: (tile(b, t), 0))
    head_major = pl.BlockSpec((ROW_TILE * DIFF_HEADS, HEAD_LANES), lambda b, t: (tile(b, t), 0))
    weight = lambda a: pl.BlockSpec(a.shape, lambda b, t: (0, 0), pipeline_mode=pl.Buffered(1))
    sb32_spec = pl.BlockSpec((None, GROUP_WIDTH, ROW_TILE), lambda b, t: (tile(b, t) // tiles, 0, tile(b, t) % tiles))
    new_spec = pl.BlockSpec((DEC_SEQ, HEAD_LANES), lambda b, t: (tile(b, t) // tiles, tile(b, t) % tiles))
    cache_spec = pl.BlockSpec((None, HEAD_LANES, PAST_LEN), lambda b, t: (tile(b, t) // tiles, tile(b, t) % tiles, 0))
    pair_spec = pl.BlockSpec((SEQ, HEAD_LANES), lambda b, t: (jnp.maximum(b - 1, 0), jnp.where(b == 0, 0, t)))
    bf16_rows = jax.ShapeDtypeStruct((rows, GROUP_WIDTH), BF16)
    f32_head_major = jax.ShapeDtypeStruct((rows * DIFF_HEADS, HEAD_LANES), F32)
    sb32_shape = jax.ShapeDtypeStruct((batch, GROUP_WIDTH, SEQ), F32)
    n_blocks = SEQ // SB_BLOCK
    return pl.pallas_call(
        functools.partial(_inproj_sb_kernel, batch),
        grid=(batch + 1, tiles),
        in_specs=[row_spec(D_MODEL), pl.BlockSpec((1, D_MODEL), lambda b, t: (0, 0)), weight(w16), weight(wt16),
                  new_spec, new_spec, new_spec, cache_spec, cache_spec],
        out_specs=[row_spec(GROUP_WIDTH), head_major, head_major, sb32_spec, sb32_spec, row_spec(GROUP_WIDTH),
                   row_spec(GROUP_WIDTH), new_spec, pair_spec],
        out_shape=[bf16_rows, f32_head_major, f32_head_major, sb32_shape, sb32_shape, bf16_rows, bf16_rows,
                   jax.ShapeDtypeStruct(qs2.shape, BF16), bf16_rows],
        scratch_shapes=[pltpu.VMEM((HEAD_LANES, STAT_LANES), F32), pltpu.VMEM((HEAD_LANES, HEAD_LANES), F32),
                        pltpu.VMEM((2, SB_PAIRS, SEQ, HEAD_LANES), BF16),
                        pltpu.VMEM((2, SB_PAIRS, n_blocks, HEAD_LANES, SB_BLOCK), BF16),
                        pltpu.VMEM((2, SB_PAIRS, n_blocks, HEAD_LANES, SB_BLOCK), BF16),
                        pltpu.VMEM((SB_BLOCK, SB_BLOCK), BF16), pltpu.VMEM((2, 2, SB_BLOCK, STAT_LANES), F32),
                        pltpu.VMEM((2, 2, SB_BLOCK, HEAD_LANES), F32)],
        compiler_params=pltpu.CompilerParams(dimension_semantics=("arbitrary", "arbitrary"),
                                             vmem_limit_bytes=VMEM_LIMIT),
        name="inproj_sb_prompt",
    )(x, g, w16, wt16, qs2, ks2_16, vs2_16, cache_kt, cache_vt)


def _stack_queries(q):
    lane = lax.broadcasted_iota(jnp.int32, q.shape, 1)
    zero = jnp.zeros_like(q)
    return jnp.concatenate([jnp.where(lane < HALF, q, zero), jnp.where(lane >= HALF, q, zero)], axis=0)


def _diff_sample_kernel(slope_ref, lamv_ref, g_ref, q_ref, kn_ref, vn_ref, kc_ref, vc_ref, o_ref):
    lam = _lambda(lamv_ref)
    kpos = lax.broadcasted_iota(jnp.int32, (PAST_LEN, HEAD_LANES), 0)
    qpos = PAST_LEN + (lax.broadcasted_iota(jnp.int32, (PAST_LEN, HEAD_LANES), 1) % HALF)
    dist_c = (qpos - kpos).astype(F32)
    kn_pos = lax.broadcasted_iota(jnp.int32, (DEC_SEQ, HEAD_LANES), 0)
    qn_pos = lax.broadcasted_iota(jnp.int32, (DEC_SEQ, HEAD_LANES), 1) % HALF
    dist_n = jnp.abs(qn_pos - kn_pos).astype(F32)
    cols = [slice(h * HEAD_LANES, (h + 1) * HEAD_LANES) for h in range(DIFF_HEADS)]
    head_rows = lambda ref, h: ref[pl.ds(h, PAST_LEN, stride=DIFF_HEADS), :].astype(BF16)
    for heads in ((0, 1), (2, 3)):
        s_c, s_n = {}, {}
        for h in heads:
            slope2 = slope_ref[h] * LOG2E
            q2 = _stack_queries(q_ref[:, cols[h]])
            s_c[h] = lax.dot_general(head_rows(kc_ref, h), q2, NT_DIMS, preferred_element_type=F32) - slope2 * dist_c
            s_n[h] = lax.dot_general(kn_ref[:, cols[h]], q2, NT_DIMS, preferred_element_type=F32) - slope2 * dist_n
        p_c, p_n, l = {}, {}, {}
        for h in heads:
            m = jnp.maximum(jnp.max(s_c[h], axis=0, keepdims=True), jnp.max(s_n[h], axis=0, keepdims=True))
            p_c[h] = jnp.exp2(s_c[h] - m)
            p_n[h] = jnp.exp2(s_n[h] - m)
            l[h] = jnp.sum(p_c[h], axis=0, keepdims=True) + jnp.sum(p_n[h], axis=0, keepdims=True)
        for h in heads:
            acc = (lax.dot_general(p_c[h].astype(BF16), head_rows(vc_ref, h), TN_DIMS, preferred_element_type=F32)
                   + lax.dot_general(p_n[h].astype(BF16), vn_ref[:, cols[h]], TN_DIMS, preferred_element_type=F32))
            l_col = jnp.transpose(jnp.broadcast_to(l[h], (HEAD_LANES, HEAD_LANES)))[:, 0:1]
            o = acc / l_col
            o = o[0:DEC_SEQ] - lam * o[DEC_SEQ:2 * DEC_SEQ]
            o_ref[:, cols[h]] = _subln(o, g_ref[...]).astype(BF16)


def _diff_sample(slopes, lamv, g, qd, kd16, vd16, cache_k, cache_v):
    batch = cache_k.shape[0]
    new_spec = pl.BlockSpec((DEC_SEQ, GROUP_WIDTH), lambda b: (b, 0))
    cache_spec = pl.BlockSpec((None, PAST_LEN * DIFF_HEADS, HEAD_LANES), lambda b: (b, 0, 0))
    return pl.pallas_call(
        _diff_sample_kernel,
        grid=(batch,),
        in_specs=[pl.BlockSpec(memory_space=pltpu.SMEM),
                  pl.BlockSpec(lamv.shape, lambda b: (0, 0)),
                  pl.BlockSpec(g.shape, lambda b: (0, 0)),
                  new_spec, new_spec, new_spec, cache_spec, cache_spec],
        out_specs=new_spec,
        out_shape=jax.ShapeDtypeStruct(qd.shape, BF16),
        compiler_params=pltpu.CompilerParams(dimension_semantics=("parallel",), vmem_limit_bytes=VMEM_LIMIT),
        name="diff_sample",
    )(slopes, lamv, g, qd, kd16, vd16, cache_k, cache_v)


def _sb_sample_pieces(q_ref, kn_ref, vn_ref, kct_ref, vct_ref, o_ref, carry_ref, acc_ref):
    state = {}
    n_sweeps = PAST_LEN // SAMPLE_SWEEP
    n_blocks = SAMPLE_SWEEP // CUM_BLOCK
    last_cols = slice((n_sweeps - 1) * SAMPLE_SWEEP, n_sweeps * SAMPLE_SWEEP)
    key_idx = lax.broadcasted_iota(jnp.int32, (HEAD_LANES, DEC_SEQ), 1)
    query_idx = lax.broadcasted_iota(jnp.int32, (HEAD_LANES, DEC_SEQ), 0) % HALF
    visible = key_idx < query_idx

    def block_parts(z):
        return [_stick_block(z[:, b * CUM_BLOCK:(b + 1) * CUM_BLOCK], state["suffix"], None) for b in range(n_blocks)]

    def add_values(parts, cols):
        carry = carry_ref[...]
        weights = [None] * n_blocks
        for b in reversed(range(n_blocks)):
            weights[b] = _stick_weights(parts[b][0], carry, None)
            carry = carry + parts[b][1]
        a = jnp.concatenate(weights, axis=-1).astype(BF16)
        acc_ref[...] += lax.dot_general(a, vct_ref[:, cols].astype(BF16), NT_DIMS, preferred_element_type=F32)
        carry_ref[...] = carry
        return jnp.max(carry)

    def logits():
        state["q2"] = _stack_queries(q_ref[...])
        state["suffix"] = _suffix_matrix(CUM_BLOCK)
        state["z_new"] = lax.dot_general(state["q2"], kn_ref[...], NT_DIMS, preferred_element_type=F32)
        state["z_last"] = jnp.dot(state["q2"], kct_ref[:, last_cols].astype(BF16), preferred_element_type=F32)

    def block_sums():
        state["new"] = _stick_block(state["z_new"], _suffix_matrix(DEC_SEQ), visible)
        state["last"] = block_parts(state["z_last"])

    def new_values():
        log2w, carry = state["new"]
        a = jnp.where(visible, jnp.exp2(log2w), 0.0)
        acc_ref[...] = jnp.dot(a.astype(BF16), vn_ref[...], preferred_element_type=F32)
        carry_ref[...] = jnp.broadcast_to(carry, carry_ref.shape)

    def last_values():
        state["top"] = add_values(state["last"], last_cols)

    def earlier():
        def unfinished(loop):
            j, top = loop
            return jnp.logical_and(j >= 0, top > EXIT_LOG2)

        def body(loop):
            j, _ = loop
            cols = pl.ds(pl.multiple_of(j * SAMPLE_SWEEP, SAMPLE_SWEEP), SAMPLE_SWEEP)
            z = jnp.dot(state["q2"], kct_ref[:, cols].astype(BF16), preferred_element_type=F32)
            return j - 1, add_values(block_parts(z), cols)

        lax.while_loop(unfinished, body, (n_sweeps - 2, state["top"]))
        lane = lax.broadcasted_iota(jnp.int32, (DEC_SEQ, HEAD_LANES), 1)
        o_ref[...] = jnp.where(lane < HALF, acc_ref[0:DEC_SEQ, :], acc_ref[DEC_SEQ:2 * DEC_SEQ, :]).astype(BF16)

    return [logits, block_sums, new_values, last_values, earlier]


def kernel(x_prompt, x_sample, cache_diff_k, cache_diff_v, cache_sb_k, cache_sb_v, norm_attn_g, w_in,
           lambda_q1, lambda_k1, lambda_q2, lambda_k2, diff_subln_g, w_out, norm_ffn_g, w_gate, w_up, w_down,
           norm_final_g):
    batch, seq, _ = x_prompt.shape
    dec_batch, dec_seq, _ = x_sample.shape
    assert seq == SEQ and dec_seq == DEC_SEQ and cache_diff_k.shape[2] == PAST_LEN and w_in.shape[0] == 1

    w_in16 = w_in[0].astype(BF16)
    w_sb_t16 = jnp.transpose(w_in16[:, 4 * GROUP_WIDTH:])
    w_out16 = w_out[0].astype(BF16)
    w_gate16 = w_gate[0].astype(BF16)
    w_up16 = w_up[0].astype(BF16)
    w_down16 = w_down[0].astype(BF16)
    g_attn = norm_attn_g[0].reshape(1, D_MODEL)
    g_ffn = norm_ffn_g[0].reshape(1, D_MODEL)
    g_final = norm_final_g.reshape(1, D_MODEL)
    g_subln = diff_subln_g[0].reshape(1, HEAD_LANES)
    lamv = jnp.concatenate([lambda_q1, lambda_k1, lambda_q2, lambda_k2], axis=0).astype(F32)
    slopes = jnp.exp2(-8.0 / DIFF_HEADS * jnp.arange(1, DIFF_HEADS + 1, dtype=F32))

    def ffn(x, od, osb):
        return _ffn(x, od, osb, w_out16, g_ffn, w_gate16, w_up16, w_down16, g_final)

    xs = x_sample.reshape(dec_batch * DEC_SEQ, D_MODEL)
    qd2, qs2, kd2, vd2, ks2, vs2, kd2_16, vd2_16, ks2_16, vs2_16 = _inproj_sample(xs, g_attn, w_in16)
    keys_minor = lambda a: jnp.transpose(a[0], (0, 2, 3, 1)).reshape(dec_batch, GROUP_WIDTH, PAST_LEN)

    xp = x_prompt.reshape(batch * SEQ, D_MODEL)
    qd, kd, vd, kst, vst, kd16, vd16, osb2, osb = _inproj_sb(
        xp, g_attn, w_in16, w_sb_t16, qs2, ks2_16, vs2_16, keys_minor(cache_sb_k), keys_minor(cache_sb_v))
    y_prompt = _diff_ffn(slopes, lamv, g_subln, qd, kd16, vd16, xp, osb, w_out16, g_ffn, w_gate16, w_up16, w_down16,
                         g_final).reshape(batch, SEQ, D_MODEL)

    head_major = lambda a: a[0].reshape(dec_batch, PAST_LEN * DIFF_HEADS, HEAD_LANES)
    od2 = _diff_sample(slopes, lamv, g_subln, qd2, kd2_16, vd2_16, head_major(cache_diff_k), head_major(cache_diff_v))
    y_sample = ffn(xs, od2, osb2).reshape(dec_batch, DEC_SEQ, D_MODEL)

    diff_shape = lambda b, t: (1, b, t, DIFF_HEADS, HEAD_LANES)
    sb_shape = lambda b, t: (1, b, t, 2 * SB_PAIRS, HALF)
    from_keys_minor = lambda a: jnp.transpose(a.reshape(batch, 2 * SB_PAIRS, HALF, SEQ), (0, 3, 1, 2))[None]
    return (y_prompt, y_sample,
            kd.reshape(diff_shape(batch, SEQ)), vd.reshape(diff_shape(batch, SEQ)),
            from_keys_minor(kst), from_keys_minor(vst),
            kd2.reshape(diff_shape(dec_batch, DEC_SEQ)), vd2.reshape(diff_shape(dec_batch, DEC_SEQ)),
            ks2.reshape(sb_shape(dec_batch, DEC_SEQ)), vs2.reshape(sb_shape(dec_batch, DEC_SEQ)))
```

```python
import functools
import math

import jax
import jax.numpy as jnp
from jax import lax
from jax.experimental import pallas as pl
from jax.experimental.pallas import tpu as pltpu

D_MODEL = 1024
SEQ = 2048
DEC_SEQ = 64
PAST_LEN = 4096
CHUNK = 64
GROUP_WIDTH = 512
DIFF_HEADS = 4
HEAD_LANES = 128
HALF = 64
SB_PAIRS = 4
D_FF = 2816
FF_CHUNK = 256
RMS_EPS = 1e-6
NEG_INF = -1e30
LOG2E = math.log2(math.e)
QK_SCALE = 0.125 * LOG2E
LAMBDA_INIT = 0.8 - 0.6 * math.exp(-0.3 * 0)

ROW_TILE = 512
TQ = 256
TK = TQ
CUM_BLOCK = 256
SB_BLOCK = CUM_BLOCK
SAMPLE_SWEEP = 512
EXIT_LOG2 = -160.0
STAT_LANES = 128
VMEM_LIMIT = 56 * 1024 * 1024
FUSED_VMEM_LIMIT = 58 * 1024 * 1024

F32 = jnp.float32
BF16 = jnp.bfloat16
BF16_ROWS = 16
ALIBI_TERMS = 3
NT_DIMS = (((1,), (1,)), ((), ()))
TN_DIMS = (((0,), (0,)), ((), ()))


def _rms(x, g):
    return x * lax.rsqrt(jnp.mean(x * x, axis=-1, keepdims=True) + RMS_EPS) * g


def _lambda(lamv_ref):
    lv = lamv_ref[...]
    a = jnp.sum(lv[0:1] * lv[1:2], axis=-1, keepdims=True)
    b = jnp.sum(lv[2:3] * lv[3:4], axis=-1, keepdims=True)
    return jnp.exp(a) - jnp.exp(b) + LAMBDA_INIT


def _log2_keep(z2):
    nz = -z2
    return jnp.minimum(nz, 0.0) - jnp.log(1.0 + jnp.exp2(jnp.minimum(z2, nz))) * LOG2E


def _lanes(x, width):
    if width <= STAT_LANES:
        return x[:, :width]
    return jnp.concatenate([x] * (width // STAT_LANES), axis=1)


def _suffix_matrix(n):
    j = lax.broadcasted_iota(jnp.int32, (n, n), 0)
    s = lax.broadcasted_iota(jnp.int32, (n, n), 1)
    return jnp.where(j > s, 1.0, 0.0).astype(BF16)


def _stick_block(z, suffix, visible):
    lk = _log2_keep(z)
    if visible is not None:
        lk = jnp.where(visible, lk, 0.0)
    within = jnp.dot(lk.astype(BF16), suffix, preferred_element_type=F32)
    return (z + lk) + within, jnp.sum(lk, axis=-1, keepdims=True)


def _stick_weights(log2w, carry, visible):
    a = jnp.exp2(log2w + _lanes(carry, log2w.shape[-1]))
    return a if visible is None else jnp.where(visible, a, 0.0)


def _inproj_slots(x_ref, g_ref, w_ref, wt_ref, qd_ref, kd_ref, vd_ref, kd16_ref, vd16_ref, put_qs, put_ks, put_vs,
                  sample):
    state = {}

    def proj(c):
        return jnp.dot(state["n"], w_ref[:, c * GROUP_WIDTH:(c + 1) * GROUP_WIDTH], preferred_element_type=F32)

    def diff_q():
        state["n"] = _rms(x_ref[...], g_ref[...]).astype(BF16)
        qd_ref[...] = (proj(0) * QK_SCALE).astype(BF16)
        sample[0]()

    def sb_q():
        put_qs((proj(3) * QK_SCALE).astype(BF16))
        sample[1]()

    def diff_kv(c, out_ref, out16_ref, piece):
        def run():
            r = proj(c)
            out16_ref[...] = r.astype(BF16)
            for h in range(DIFF_HEADS):
                out_ref[pl.ds(h, ROW_TILE, stride=DIFF_HEADS), :] = r[:, h * HEAD_LANES:(h + 1) * HEAD_LANES]
            piece()
        return run

    return [diff_q, sb_q, diff_kv(1, kd_ref, kd16_ref, sample[2]), diff_kv(2, vd_ref, vd16_ref, sample[3]),
            lambda: put_ks(proj, state), lambda: put_vs(proj, state), lambda: None, sample[4]]


def _inproj_sample_kernel(x_ref, g_ref, w_ref, qd_ref, qs_ref, kd_ref, vd_ref, ks_ref, vs_ref,
                          kd16_ref, vd16_ref, ks16_ref, vs16_ref):
    def put_qs(q):
        qs_ref[...] = q

    def put(c, out_ref, out16_ref):
        def run(proj, state):
            r = proj(c)
            out_ref[...] = r
            out16_ref[...] = r.astype(BF16)
        return run

    for slot in _inproj_slots(x_ref, g_ref, w_ref, None, qd_ref, kd_ref, vd_ref, kd16_ref, vd16_ref, put_qs,
                              put(4, ks_ref, ks16_ref), put(5, vs_ref, vs16_ref), [lambda: None] * 5):
        slot()


def _inproj_sample(x, g, w16):
    rows = x.shape[0]
    row_spec = lambda width: pl.BlockSpec((ROW_TILE, width), lambda i: (i, 0))
    head_major = pl.BlockSpec((ROW_TILE * DIFF_HEADS, HEAD_LANES), lambda i: (i, 0))
    bf16_rows = jax.ShapeDtypeStruct((rows, GROUP_WIDTH), BF16)
    f32_rows = jax.ShapeDtypeStruct((rows, GROUP_WIDTH), F32)
    f32_head_major = jax.ShapeDtypeStruct((rows * DIFF_HEADS, HEAD_LANES), F32)
    return pl.pallas_call(
        _inproj_sample_kernel,
        grid=(rows // ROW_TILE,),
        in_specs=[row_spec(D_MODEL), pl.BlockSpec((1, D_MODEL), lambda i: (0, 0)),
                  pl.BlockSpec(w16.shape, lambda i: (0, 0))],
        out_specs=[row_spec(GROUP_WIDTH), row_spec(GROUP_WIDTH), head_major, head_major, row_spec(GROUP_WIDTH),
                   row_spec(GROUP_WIDTH), row_spec(GROUP_WIDTH), row_spec(GROUP_WIDTH), row_spec(GROUP_WIDTH),
                   row_spec(GROUP_WIDTH)],
        out_shape=[bf16_rows, bf16_rows, f32_head_major, f32_head_major, f32_rows, f32_rows,
                   bf16_rows, bf16_rows, bf16_rows, bf16_rows],
        compiler_params=pltpu.CompilerParams(dimension_semantics=("parallel",), vmem_limit_bytes=VMEM_LIMIT),
        name="inproj_sample",
    )(x, g, w16)


def _ffn_pieces(n_pieces, x_ref, od, osb_ref, wout_ref, gffn_ref, wg_ref, wu_ref, wd_ref, gfin_ref, y_ref, act_ref):
    state = {}
    n_chunks = D_FF // FF_CHUNK

    def first():
        mixed = jnp.concatenate([od(), osb_ref[...]], axis=-1)
        state["h"] = x_ref[...] + jnp.dot(mixed, wout_ref[...], preferred_element_type=F32)
        state["n"] = _rms(state["h"], gffn_ref[...]).astype(BF16)

    def chunks(lo, hi):
        def run():
            for c in range(lo, hi):
                cols = slice(c * FF_CHUNK, (c + 1) * FF_CHUNK)
                gate = jnp.dot(state["n"], wg_ref[:, cols], preferred_element_type=F32)
                up = jnp.dot(state["n"], wu_ref[:, cols], preferred_element_type=F32)
                act_ref[:, cols] = (gate / (1.0 + jnp.exp(-gate)) * up).astype(BF16)
        return run

    def last():
        y = state["h"] + jnp.dot(act_ref[...], wd_ref[...], preferred_element_type=F32)
        y_ref[...] = _rms(y, gfin_ref[...])

    middle = n_pieces - 2
    bounds = [(n_chunks * m) // middle for m in range(middle + 1)]
    return [first] + [chunks(bounds[m], bounds[m + 1]) for m in range(middle)] + [last]


def _ffn_kernel(x_ref, od_ref, osb_ref, wout_ref, gffn_ref, wg_ref, wu_ref, wd_ref, gfin_ref, y_ref, act_ref):
    for piece in _ffn_pieces(3, x_ref, lambda: od_ref[...], osb_ref, wout_ref, gffn_ref, wg_ref, wu_ref, wd_ref,
                             gfin_ref, y_ref, act_ref):
        piece()


def _ffn(x, od, osb, wout16, gffn, wg16, wu16, wd16, gfin):
    rows = x.shape[0]
    row_spec = lambda width: pl.BlockSpec((ROW_TILE, width), lambda i: (i, 0))
    full = lambda a: pl.BlockSpec(a.shape, lambda i: (0, 0))
    return pl.pallas_call(
        _ffn_kernel,
        grid=(rows // ROW_TILE,),
        in_specs=[row_spec(D_MODEL), row_spec(GROUP_WIDTH), row_spec(GROUP_WIDTH),
                  full(wout16), full(gffn), full(wg16), full(wu16), full(wd16), full(gfin)],
        out_specs=row_spec(D_MODEL),
        out_shape=jax.ShapeDtypeStruct((rows, D_MODEL), F32),
        scratch_shapes=[pltpu.VMEM((ROW_TILE, D_FF), BF16)],
        compiler_params=pltpu.CompilerParams(dimension_semantics=("parallel",), vmem_limit_bytes=VMEM_LIMIT),
        name="outproj_ffn",
    )(x, od, osb, wout16, gffn, wg16, wu16, wd16, gfin)


def _subln(o, g):
    o = o * lax.rsqrt(jnp.mean(o * o, axis=-1, keepdims=True) + RMS_EPS)
    return o * g * (1.0 - LAMBDA_INIT)


def _diff_head(slope2, lam, g_ref, q_ref, k_ref, v_ref, bias_ref, emit, between):
    r = lax.broadcasted_iota(jnp.int32, (TQ, TK), 0)
    c = lax.broadcasted_iota(jnp.int32, (TQ, TK), 1)
    bias_ref[...] = jnp.where((c // CHUNK) <= (r // CHUNK),
                              jnp.where(c > r, (-2.0 * slope2) * (c - r).astype(F32), 0.0), NEG_INF)
    lane = lax.broadcasted_iota(jnp.int32, (TQ, HEAD_LANES), 1)
    key_bias = slope2 * lax.broadcasted_iota(jnp.int32, (BF16_ROWS, SEQ), 1).astype(F32)
    row = lax.broadcasted_iota(jnp.int32, (BF16_ROWS, SEQ), 0)
    hi = key_bias.astype(BF16).astype(F32)
    mid = (key_bias - hi).astype(BF16).astype(F32)
    lo = key_bias - hi - mid
    split = jnp.where(row == 0, hi, jnp.where(row == 1, mid, jnp.where(row == 2, lo, 0.0))).astype(BF16)
    kt = jnp.concatenate([jnp.transpose(k_ref[...]), split,
                          jnp.zeros((HEAD_LANES - BF16_ROWS, SEQ), BF16)], axis=0)
    ones = jnp.where(lane < ALIBI_TERMS, 1.0, 0.0).astype(BF16)

    def scores(i):
        q = q_ref[i * TQ:(i + 1) * TQ, :]
        qms = [jnp.concatenate([jnp.where((lane < HALF) == first, q, jnp.zeros_like(q)), ones], axis=-1)
               for first in (True, False)]
        raw = [jnp.dot(qm, kt[:, :TK * (i + 1)], preferred_element_type=F32) for qm in qms]
        return [jnp.concatenate([s[:, :TK * i], s[:, TK * i:] + bias_ref[...]], axis=-1) if i else
                s + bias_ref[...] for s in raw]

    def attend(i, s):
        e = [jnp.exp2(sm - jnp.max(sm, axis=-1, keepdims=True)) for sm in s]
        l = [jnp.sum(em, axis=-1, keepdims=True) for em in e]
        a = (e[0] - (lam * l[0] / l[1]) * e[1]).astype(BF16)
        o = jnp.dot(a, v_ref[0:TK * (i + 1), :], preferred_element_type=F32) / l[0]
        emit(i, _subln(o, g_ref[...]).astype(BF16))

    n_q = SEQ // TQ
    s_next = scores(0)
    for i in range(n_q):
        s_now = s_next
        if i + 1 < n_q:
            s_next = scores(i + 1)
        between(i)
        attend(i, s_now)


def _diff_ffn_kernel(slope_ref, lamv_ref, gsub_ref, q_ref, k_ref, v_ref, x_ref, osb_ref, wout_ref, gffn_ref,
                     wg_ref, wu_ref, wd_ref, gfin_ref, y_ref, bias_ref, od_buf, act_ref):
    b, h = pl.program_id(0), pl.program_id(1)
    slot = b % 2
    slope2 = slope_ref[h] * LOG2E

    def emit(i, o):
        od_buf[slot, h, i * TQ:(i + 1) * TQ, :] = o

    @pl.when(b == 0)
    def _():
        _diff_head(slope2, _lambda(lamv_ref), gsub_ref, q_ref, k_ref, v_ref, bias_ref, emit, lambda i: None)

    @pl.when(b > 0)
    def _():
        tile_rows = pl.ds(pl.multiple_of(h * ROW_TILE, ROW_TILE), ROW_TILE)
        previous = lambda: jnp.concatenate([od_buf[1 - slot, hh, tile_rows, :] for hh in range(DIFF_HEADS)],
                                           axis=-1)
        pieces = _ffn_pieces(SEQ // TQ, x_ref, previous, osb_ref, wout_ref, gffn_ref, wg_ref, wu_ref, wd_ref,
                             gfin_ref, y_ref, act_ref)
        _diff_head(slope2, _lambda(lamv_ref), gsub_ref, q_ref, k_ref, v_ref, bias_ref, emit,
                   lambda i: pieces[i]())


def _diff_ffn(slopes, lamv, gsub, qd, kd16, vd16, x, osb, wout16, gffn, wg16, wu16, wd16, gfin):
    rows = x.shape[0]
    batch = rows // SEQ
    assert SEQ // ROW_TILE == DIFF_HEADS
    head_spec = pl.BlockSpec((SEQ, HEAD_LANES), lambda b, h: (jnp.minimum(b, batch - 1), h))
    tile = lambda b, h: (jnp.where(b == 0, 0, (b - 1) * DIFF_HEADS + h), 0)
    row_spec = lambda width: pl.BlockSpec((ROW_TILE, width), tile)
    const = lambda a: pl.BlockSpec(a.shape, lambda b, h: (0, 0))
    weight = lambda a: pl.BlockSpec(a.shape, lambda b, h: (0, 0), pipeline_mode=pl.Buffered(1))
    return pl.pallas_call(
        _diff_ffn_kernel,
        grid=(batch + 1, DIFF_HEADS),
        in_specs=[pl.BlockSpec(memory_space=pltpu.SMEM), const(lamv), const(gsub), head_spec, head_spec, head_spec,
                  row_spec(D_MODEL), row_spec(GROUP_WIDTH), weight(wout16), const(gffn), weight(wg16), weight(wu16),
                  weight(wd16), const(gfin)],
        out_specs=row_spec(D_MODEL),
        out_shape=jax.ShapeDtypeStruct((rows, D_MODEL), F32),
        scratch_shapes=[pltpu.VMEM((TQ, TK), F32), pltpu.VMEM((2, DIFF_HEADS, SEQ, HEAD_LANES), BF16),
                        pltpu.VMEM((ROW_TILE, D_FF), BF16)],
        compiler_params=pltpu.CompilerParams(dimension_semantics=("arbitrary", "arbitrary"),
                                             vmem_limit_bytes=FUSED_VMEM_LIMIT),
        name="diff_prompt_ffn",
    )(slopes, lamv, gsub, qd, kd16, vd16, x, osb, wout16, gffn, wg16, wu16, wd16, gfin)


def _sb_prompt_steps(q_ref, kt_ref, vt_ref, o_ref, u_ref, carry_ref, acc_ref):
    u_ref[...] = _suffix_matrix(SB_BLOCK)
    causal = (lax.broadcasted_iota(jnp.int32, (SB_BLOCK, SB_BLOCK), 1)
              < lax.broadcasted_iota(jnp.int32, (SB_BLOCK, SB_BLOCK), 0))
    lane = lax.broadcasted_iota(jnp.int32, (SB_BLOCK, HEAD_LANES), 1)

    def rows_of(i):
        return pl.ds(i * SB_BLOCK, SB_BLOCK)

    def begin(slot, i):
        q = q_ref[rows_of(i), :]
        carry_ref[slot] = jnp.zeros(carry_ref.shape[1:], F32)
        acc_ref[slot] = jnp.zeros(acc_ref.shape[1:], F32)
        return jnp.where(lane < HALF, q, jnp.zeros_like(q)), jnp.where(lane >= HALF, q, jnp.zeros_like(q))

    def add_blocks(items, diagonal, between=(lambda: None, lambda: None)):
        chains = [(n, h, t) for n, item in enumerate(items) for h in range(2) for t in reversed(range(item[3]))]
        mask = {c: causal if diagonal and c[2] == items[c[0]][3] - 1 else None for c in chains}
        z = {(n, h, t): jnp.dot(items[n][1][h], kt_ref[items[n][2] + t], preferred_element_type=F32)
             for n, h, t in chains}
        between[0]()
        part = {c: _stick_block(z[c], u_ref[...], mask[c]) for c in chains}
        between[1]()
        carry = {(n, h): carry_ref[item[0], h] for n, item in enumerate(items) for h in range(2)}
        weights = {}
        for n, h, t in chains:
            weights[n, h, t] = _stick_weights(part[n, h, t][0], carry[n, h], mask[n, h, t]).astype(BF16)
            carry[n, h] = carry[n, h] + part[n, h, t][1]
        for n, (slot, _, first, count) in enumerate(items):
            for h in range(2):
                acc = acc_ref[slot, h]
                for t in range(count):
                    acc = acc + lax.dot_general(weights[n, h, t], vt_ref[first + t], NT_DIMS,
                                                preferred_element_type=F32)
                acc_ref[slot, h] = acc
                carry_ref[slot, h] = carry[n, h]

    def finish(slot, qms, i, j):
        def unfinished(state):
            j, top = state
            return jnp.logical_and(j >= 0, top > EXIT_LOG2)

        def earlier(state):
            j, _ = state
            add_blocks([(slot, qms, j, 1)], False)
            return j - 1, jnp.max(carry_ref[slot])

        lax.while_loop(unfinished, earlier, (jnp.int32(j), jnp.max(carry_ref[slot])))
        o_ref[rows_of(i), :] = jnp.where(lane < HALF, acc_ref[slot, 0], acc_ref[slot, 1]).astype(BF16)

    def query_pair(i_a, count_a, i_b, between):
        qms_a, qms_b = begin(0, i_a), begin(1, i_b)
        first_a = i_a - (count_a - 1)
        add_blocks([(0, qms_a, first_a, count_a), (1, qms_b, i_b - 1, 2)], True, between)
        finish(0, qms_a, i_a, first_a - 1)
        finish(1, qms_b, i_b, i_b - 2)

    n_q = SEQ // SB_BLOCK
    pairs = [(0, 1, n_q - 1)] + [(2 * m + 1, 2, 2 * m + 2) for m in range((n_q - 2) // 2)]
    return [functools.partial(query_pair, *p) for p in pairs]


def _inproj_sb_kernel(batch, x_ref, g_ref, w_ref, wt_ref, qs2_ref, kn_ref, vn_ref, kct_ref, vct_ref,
                      qd_ref, kd_ref, vd_ref, ks_ref, vs_ref, kd16_ref, vd16_ref, osb2_ref, osb_ref,
                      scarry_ref, sacc_ref, q_buf, kt_buf, vt_buf, u_ref, carry_ref, acc_ref):
    b, t = pl.program_id(0), pl.program_id(1)
    slot = b % 2
    tile_rows = pl.ds(pl.multiple_of(t * ROW_TILE, ROW_TILE), ROW_TILE)
    blocks_per_tile = ROW_TILE // SB_BLOCK

    def put_qs(q):
        for p in range(SB_PAIRS):
            q_buf[slot, p, tile_rows, :] = q[:, p * HEAD_LANES:(p + 1) * HEAD_LANES]

    def put_transposed(c, out_ref, buf):
        def run(proj, state):
            rows = slice((c - 4) * GROUP_WIDTH, (c - 3) * GROUP_WIDTH)
            r = lax.dot_general(wt_ref[rows, :], state["n"], NT_DIMS, preferred_element_type=F32)
            out_ref[...] = r
            for p in range(SB_PAIRS):
                for j in range(blocks_per_tile):
                    buf[slot, p, t * blocks_per_tile + j] = r[p * HEAD_LANES:(p + 1) * HEAD_LANES,
                                                              j * SB_BLOCK:(j + 1) * SB_BLOCK].astype(BF16)
        return run

    def projection_slots():
        sample = _sb_sample_pieces(qs2_ref, kn_ref, vn_ref, kct_ref, vct_ref, osb2_ref, scarry_ref, sacc_ref)
        return _inproj_slots(x_ref, g_ref, w_ref, wt_ref, qd_ref, kd_ref, vd_ref, kd16_ref, vd16_ref, put_qs,
                             put_transposed(4, ks_ref, kt_buf), put_transposed(5, vs_ref, vt_buf), sample)

    def attention_steps():
        return _sb_prompt_steps(q_buf.at[1 - slot, t], kt_buf.at[1 - slot, t], vt_buf.at[1 - slot, t], osb_ref,
                                u_ref, carry_ref, acc_ref)

    @pl.when(b == 0)
    def _():
        for piece in projection_slots():
            piece()

    @pl.when(jnp.logical_and(b > 0, b < batch))
    def _():
        slots = projection_slots()
        for k, step in enumerate(attention_steps()):
            step((slots[2 * k], slots[2 * k + 1]))

    @pl.when(b == batch)
    def _():
        for step in attention_steps():
            step((lambda: None, lambda: None))


def _inproj_sb(x, g, w16, wt16, qs2, ks2_16, vs2_16, cache_kt, cache_vt):
    rows = x.shape[0]
    batch = rows // SEQ
    tiles = SEQ // ROW_TILE
    assert tiles == SB_PAIRS and cache_kt.shape[0] == batch
    tile = lambda b, t: jnp.where(b == batch, batch * tiles - 1, b * tiles + t)
    row_spec = lambda width: pl.BlockSpec((ROW_TILE, width), lambda b, t: (tile(b, t), 0))
    head_major = pl.BlockSpec((ROW_TILE * DIFF_HEADS, HEAD_LANES), lambda b, t: (tile(b, t), 0))
    weight = lambda a: pl.BlockSpec(a.shape, lambda b, t: (0, 0), pipeline_mode=pl.Buffered(1))
    sb32_spec = pl.BlockSpec((None, GROUP_WIDTH, ROW_TILE), lambda b, t: (tile(b, t) // tiles, 0, tile(b, t) % tiles))
    new_spec = pl.BlockSpec((DEC_SEQ, HEAD_LANES), lambda b, t: (tile(b, t) // tiles, tile(b, t) % tiles))
    cache_spec = pl.BlockSpec((None, HEAD_LANES, PAST_LEN), lambda b, t: (tile(b, t) // tiles, tile(b, t) % tiles, 0))
    pair_spec = pl.BlockSpec((SEQ, HEAD_LANES), lambda b, t: (jnp.maximum(b - 1, 0), jnp.where(b == 0, 0, t)))
    bf16_rows = jax.ShapeDtypeStruct((rows, GROUP_WIDTH), BF16)
    f32_head_major = jax.ShapeDtypeStruct((rows * DIFF_HEADS, HEAD_LANES), F32)
    sb32_shape = jax.ShapeDtypeStruct((batch, GROUP_WIDTH, SEQ), F32)
    n_blocks = SEQ // SB_BLOCK
    return pl.pallas_call(
        functools.partial(_inproj_sb_kernel, batch),
        grid=(batch + 1, tiles),
        in_specs=[row_spec(D_MODEL), pl.BlockSpec((1, D_MODEL), lambda b, t: (0, 0)), weight(w16), weight(wt16),
                  new_spec, new_spec, new_spec, cache_spec, cache_spec],
        out_specs=[row_spec(GROUP_WIDTH), head_major, head_major, sb32_spec, sb32_spec, row_spec(GROUP_WIDTH),
                   row_spec(GROUP_WIDTH), new_spec, pair_spec],
        out_shape=[bf16_rows, f32_head_major, f32_head_major, sb32_shape, sb32_shape, bf16_rows, bf16_rows,
                   jax.ShapeDtypeStruct(qs2.shape, BF16), bf16_rows],
        scratch_shapes=[pltpu.VMEM((HEAD_LANES, STAT_LANES), F32), pltpu.VMEM((HEAD_LANES, HEAD_LANES), F32),
                        pltpu.VMEM((2, SB_PAIRS, SEQ, HEAD_LANES), BF16),
                        pltpu.VMEM((2, SB_PAIRS, n_blocks, HEAD_LANES, SB_BLOCK), BF16),
                        pltpu.VMEM((2, SB_PAIRS, n_blocks, HEAD_LANES, SB_BLOCK), BF16),
                        pltpu.VMEM((SB_BLOCK, SB_BLOCK), BF16), pltpu.VMEM((2, 2, SB_BLOCK, STAT_LANES), F32),
                        pltpu.VMEM((2, 2, SB_BLOCK, HEAD_LANES), F32)],
        compiler_params=pltpu.CompilerParams(dimension_semantics=("arbitrary", "arbitrary"),
                                             vmem_limit_bytes=VMEM_LIMIT),
        name="inproj_sb_prompt",
    )(x, g, w16, wt16, qs2, ks2_16, vs2_16, cache_kt, cache_vt)


def _stack_queries(q):
    lane = lax.broadcasted_iota(jnp.int32, q.shape, 1)
    zero = jnp.zeros_like(q)
    return jnp.concatenate([jnp.where(lane < HALF, q, zero), jnp.where(lane >= HALF, q, zero)], axis=0)


def _diff_sample_kernel(slope_ref, lamv_ref, g_ref, q_ref, kn_ref, vn_ref, kc_ref, vc_ref, o_ref):
    lam = _lambda(lamv_ref)
    kpos = lax.broadcasted_iota(jnp.int32, (PAST_LEN, HEAD_LANES), 0)
    qpos = PAST_LEN + (lax.broadcasted_iota(jnp.int32, (PAST_LEN, HEAD_LANES), 1) % HALF)
    dist_c = (qpos - kpos).astype(F32)
    kn_pos = lax.broadcasted_iota(jnp.int32, (DEC_SEQ, HEAD_LANES), 0)
    qn_pos = lax.broadcasted_iota(jnp.int32, (DEC_SEQ, HEAD_LANES), 1) % HALF
    dist_n = jnp.abs(qn_pos - kn_pos).astype(F32)
    cols = [slice(h * HEAD_LANES, (h + 1) * HEAD_LANES) for h in range(DIFF_HEADS)]
    head_rows = lambda ref, h: ref[pl.ds(h, PAST_LEN, stride=DIFF_HEADS), :].astype(BF16)
    for heads in ((0, 1), (2, 3)):
        s_c, s_n = {}, {}
        for h in heads:
            slope2 = slope_ref[h] * LOG2E
            q2 = _stack_queries(q_ref[:, cols[h]])
            s_c[h] = lax.dot_general(head_rows(kc_ref, h), q2, NT_DIMS, preferred_element_type=F32) - slope2 * dist_c
            s_n[h] = lax.dot_general(kn_ref[:, cols[h]], q2, NT_DIMS, preferred_element_type=F32) - slope2 * dist_n
        p_c, p_n, l = {}, {}, {}
        for h in heads:
            m = jnp.maximum(jnp.max(s_c[h], axis=0, keepdims=True), jnp.max(s_n[h], axis=0, keepdims=True))
            p_c[h] = jnp.exp2(s_c[h] - m)
            p_n[h] = jnp.exp2(s_n[h] - m)
            l[h] = jnp.sum(p_c[h], axis=0, keepdims=True) + jnp.sum(p_n[h], axis=0, keepdims=True)
        for h in heads:
            acc = (lax.dot_general(p_c[h].astype(BF16), head_rows(vc_ref, h), TN_DIMS, preferred_element_type=F32)
                   + lax.dot_general(p_n[h].astype(BF16), vn_ref[:, cols[h]], TN_DIMS, preferred_element_type=F32))
            l_col = jnp.transpose(jnp.broadcast_to(l[h], (HEAD_LANES, HEAD_LANES)))[:, 0:1]
            o = acc / l_col
            o = o[0:DEC_SEQ] - lam * o[DEC_SEQ:2 * DEC_SEQ]
            o_ref[:, cols[h]] = _subln(o, g_ref[...]).astype(BF16)


def _diff_sample(slopes, lamv, g, qd, kd16, vd16, cache_k, cache_v):
    batch = cache_k.shape[0]
    new_spec = pl.BlockSpec((DEC_SEQ, GROUP_WIDTH), lambda b: (b, 0))
    cache_spec = pl.BlockSpec((None, PAST_LEN * DIFF_HEADS, HEAD_LANES), lambda b: (b, 0, 0))
    return pl.pallas_call(
        _diff_sample_kernel,
        grid=(batch,),
        in_specs=[pl.BlockSpec(memory_space=pltpu.SMEM),
                  pl.BlockSpec(lamv.shape, lambda b: (0, 0)),
                  pl.BlockSpec(g.shape, lambda b: (0, 0)),
                  new_spec, new_spec, new_spec, cache_spec, cache_spec],
        out_specs=new_spec,
        out_shape=jax.ShapeDtypeStruct(qd.shape, BF16),
        compiler_params=pltpu.CompilerParams(dimension_semantics=("parallel",), vmem_limit_bytes=VMEM_LIMIT),
        name="diff_sample",
    )(slopes, lamv, g, qd, kd16, vd16, cache_k, cache_v)


def _sb_sample_pieces(q_ref, kn_ref, vn_ref, kct_ref, vct_ref, o_ref, carry_ref, acc_ref):
    state = {}
    n_sweeps = PAST_LEN // SAMPLE_SWEEP
    n_blocks = SAMPLE_SWEEP // CUM_BLOCK
    last_cols = slice((n_sweeps - 1) * SAMPLE_SWEEP, n_sweeps * SAMPLE_SWEEP)
    key_idx = lax.broadcasted_iota(jnp.int32, (HEAD_LANES, DEC_SEQ), 1)
    query_idx = lax.broadcasted_iota(jnp.int32, (HEAD_LANES, DEC_SEQ), 0) % HALF
    visible = key_idx < query_idx

    def block_parts(z):
        return [_stick_block(z[:, b * CUM_BLOCK:(b + 1) * CUM_BLOCK], state["suffix"], None) for b in range(n_blocks)]

    def add_values(parts, cols):
        carry = carry_ref[...]
        weights = [None] * n_blocks
        for b in reversed(range(n_blocks)):
            weights[b] = _stick_weights(parts[b][0], carry, None)
            carry = carry + parts[b][1]
        a = jnp.concatenate(weights, axis=-1).astype(BF16)
        acc_ref[...] += lax.dot_general(a, vct_ref[:, cols].astype(BF16), NT_DIMS, preferred_element_type=F32)
        carry_ref[...] = carry
        return jnp.max(carry)

    def logits():
        state["q2"] = _stack_queries(q_ref[...])
        state["suffix"] = _suffix_matrix(CUM_BLOCK)
        state["z_new"] = lax.dot_general(state["q2"], kn_ref[...], NT_DIMS, preferred_element_type=F32)
        state["z_last"] = jnp.dot(state["q2"], kct_ref[:, last_cols].astype(BF16), preferred_element_type=F32)

    def block_sums():
        state["new"] = _stick_block(state["z_new"], _suffix_matrix(DEC_SEQ), visible)
        state["last"] = block_parts(state["z_last"])

    def new_values():
        log2w, carry = state["new"]
        a = jnp.where(visible, jnp.exp2(log2w), 0.0)
        acc_ref[...] = jnp.dot(a.astype(BF16), vn_ref[...], preferred_element_type=F32)
        carry_ref[...] = jnp.broadcast_to(carry, carry_ref.shape)

    def last_values():
        state["top"] = add_values(state["last"], last_cols)

    def earlier():
        def unfinished(loop):
            j, top = loop
            return jnp.logical_and(j >= 0, top > EXIT_LOG2)

        def body(loop):
            j, _ = loop
            cols = pl.ds(pl.multiple_of(j * SAMPLE_SWEEP, SAMPLE_SWEEP), SAMPLE_SWEEP)
            z = jnp.dot(state["q2"], kct_ref[:, cols].astype(BF16), preferred_element_type=F32)
            return j - 1, add_values(block_parts(z), cols)

        lax.while_loop(unfinished, body, (n_sweeps - 2, state["top"]))
        lane = lax.broadcasted_iota(jnp.int32, (DEC_SEQ, HEAD_LANES), 1)
        o_ref[...] = jnp.where(lane < HALF, acc_ref[0:DEC_SEQ, :], acc_ref[DEC_SEQ:2 * DEC_SEQ, :]).astype(BF16)

    return [logits, block_sums, new_values, last_values, earlier]


def kernel(x_prompt, x_sample, cache_diff_k, cache_diff_v, cache_sb_k, cache_sb_v, norm_attn_g, w_in,
           lambda_q1, lambda_k1, lambda_q2, lambda_k2, diff_subln_g, w_out, norm_ffn_g, w_gate, w_up, w_down,
           norm_final_g):
    batch, seq, _ = x_prompt.shape
    dec_batch, dec_seq, _ = x_sample.shape
    assert seq == SEQ and dec_seq == DEC_SEQ and cache_diff_k.shape[2] == PAST_LEN and w_in.shape[0] == 1

    w_in16 = w_in[0].astype(BF16)
    w_sb_t16 = jnp.transpose(w_in[0][:, 4 * GROUP_WIDTH:]).astype(BF16)
    w_out16 = w_out[0].astype(BF16)
    w_gate16 = w_gate[0].astype(BF16)
    w_up16 = w_up[0].astype(BF16)
    w_down16 = w_down[0].astype(BF16)
    g_attn = norm_attn_g[0].reshape(1, D_MODEL)
    g_ffn = norm_ffn_g[0].reshape(1, D_MODEL)
    g_final = norm_final_g.reshape(1, D_MODEL)
    g_subln = diff_subln_g[0].reshape(1, HEAD_LANES)
    lamv = jnp.concatenate([lambda_q1, lambda_k1, lambda_q2, lambda_k2], axis=0).astype(F32)
    slopes = jnp.exp2(-8.0 / DIFF_HEADS * jnp.arange(1, DIFF_HEADS + 1, dtype=F32))

    def ffn(x, od, osb):
        return _ffn(x, od, osb, w_out16, g_ffn, w_gate16, w_up16, w_down16, g_final)

    xs = x_sample.reshape(dec_batch * DEC_SEQ, D_MODEL)
    qd2, qs2, kd2, vd2, ks2, vs2, kd2_16, vd2_16, ks2_16, vs2_16 = _inproj_sample(xs, g_attn, w_in16)
    keys_minor = lambda a: jnp.transpose(a[0], (0, 2, 3, 1)).reshape(dec_batch, GROUP_WIDTH, PAST_LEN)

    xp = x_prompt.reshape(batch * SEQ, D_MODEL)
    qd, kd, vd, kst, vst, kd16, vd16, osb2, osb = _inproj_sb(
        xp, g_attn, w_in16, w_sb_t16, qs2, ks2_16, vs2_16, keys_minor(cache_sb_k), keys_minor(cache_sb_v))
    y_prompt = _diff_ffn(slopes, lamv, g_subln, qd, kd16, vd16, xp, osb, w_out16, g_ffn, w_gate16, w_up16, w_down16,
                         g_final).reshape(batch, SEQ, D_MODEL)

    head_major = lambda a: a[0].reshape(dec_batch, PAST_LEN * DIFF_HEADS, HEAD_LANES)
    od2 = _diff_sample(slopes, lamv, g_subln, qd2, kd2_16, vd2_16, head_major(cache_diff_k), head_major(cache_diff_v))
    y_sample = ffn(xs, od2, osb2).reshape(dec_batch, DEC_SEQ, D_MODEL)

    diff_shape = lambda b, t: (1, b, t, DIFF_HEADS, HEAD_LANES)
    sb_shape = lambda b, t: (1, b, t, 2 * SB_PAIRS, HALF)
    from_keys_minor = lambda a: jnp.transpose(a.reshape(batch, 2 * SB_PAIRS, HALF, SEQ), (0, 3, 1, 2))[None]
    return (y_prompt, y_sample,
            kd.reshape(diff_shape(batch, SEQ)), vd.reshape(diff_shape(batch, SEQ)),
            from_keys_minor(kst), from_keys_minor(vst),
            kd2.reshape(diff_shape(dec_batch, DEC_SEQ)), vd2.reshape(diff_shape(dec_batch, DEC_SEQ)),
            ks2.reshape(sb_shape(dec_batch, DEC_SEQ)), vs2.reshape(sb_shape(dec_batch, DEC_SEQ)))
```

```python
import functools
import math

import jax
import jax.numpy as jnp
from jax import lax
from jax.experimental import pallas as pl
from jax.experimental.pallas import tpu as pltpu

D_MODEL = 1024
SEQ = 2048
DEC_SEQ = 64
PAST_LEN = 4096
CHUNK = 64
GROUP_WIDTH = 512
DIFF_HEADS = 4
HEAD_LANES = 128
HALF = 64
SB_PAIRS = 4
D_FF = 2816
FF_CHUNK = 256
RMS_EPS = 1e-6
NEG_INF = -1e30
LOG2E = math.log2(math.e)
QK_SCALE = 0.125 * LOG2E
LAMBDA_INIT = 0.8 - 0.6 * math.exp(-0.3 * 0)

ROW_TILE = 512
TQ = 256
TK = TQ
CUM_BLOCK = 256
SB_BLOCK = CUM_BLOCK
SAMPLE_SWEEP = 512
EXIT_LOG2 = -160.0
STAT_LANES = 128
VMEM_LIMIT = 56 * 1024 * 1024
FUSED_VMEM_LIMIT = 58 * 1024 * 1024

F32 = jnp.float32
BF16 = jnp.bfloat16
BF16_ROWS = 16
ALIBI_TERMS = 3
NT_DIMS = (((1,), (1,)), ((), ()))
TN_DIMS = (((0,), (0,)), ((), ()))


def _rms(x, g):
    return x * lax.rsqrt(jnp.mean(x * x, axis=-1, keepdims=True) + RMS_EPS) * g


def _lambda(lamv_ref):
    lv = lamv_ref[...]
    a = jnp.sum(lv[0:1] * lv[1:2], axis=-1, keepdims=True)
    b = jnp.sum(lv[2:3] * lv[3:4], axis=-1, keepdims=True)
    return jnp.exp(a) - jnp.exp(b) + LAMBDA_INIT


def _log2_keep(z2):
    nz = -z2
    return jnp.minimum(nz, 0.0) - jnp.log(1.0 + jnp.exp2(jnp.minimum(z2, nz))) * LOG2E


def _lanes(x, width):
    if width <= STAT_LANES:
        return x[:, :width]
    return jnp.concatenate([x] * (width // STAT_LANES), axis=1)


def _suffix_matrix(n):
    j = lax.broadcasted_iota(jnp.int32, (n, n), 0)
    s = lax.broadcasted_iota(jnp.int32, (n, n), 1)
    return jnp.where(j > s, 1.0, 0.0).astype(BF16)


def _stick_block(z, suffix, visible):
    lk = _log2_keep(z)
    if visible is not None:
        lk = jnp.where(visible, lk, 0.0)
    within = jnp.dot(lk.astype(BF16), suffix, preferred_element_type=F32)
    return (z + lk) + within, jnp.sum(lk, axis=-1, keepdims=True)


def _stick_weights(log2w, carry, visible):
    a = jnp.exp2(log2w + _lanes(carry, log2w.shape[-1]))
    return a if visible is None else jnp.where(visible, a, 0.0)


def _inproj_slots(x_ref, g_ref, w_ref, wt_ref, qd_ref, kd_ref, vd_ref, kd16_ref, vd16_ref, put_qs, put_ks, put_vs,
                  sample):
    state = {}

    def proj(c):
        return jnp.dot(state["n"], w_ref[:, c * GROUP_WIDTH:(c + 1) * GROUP_WIDTH], preferred_element_type=F32)

    def diff_q():
        state["n"] = _rms(x_ref[...], g_ref[...]).astype(BF16)
        qd_ref[...] = (proj(0) * QK_SCALE).astype(BF16)
        sample[0]()

    def sb_q():
        put_qs((proj(3) * QK_SCALE).astype(BF16))
        sample[1]()

    def diff_kv(c, out_ref, out16_ref, piece):
        def run():
            r = proj(c)
            out16_ref[...] = r.astype(BF16)
            for h in range(DIFF_HEADS):
                out_ref[pl.ds(h, ROW_TILE, stride=DIFF_HEADS), :] = r[:, h * HEAD_LANES:(h + 1) * HEAD_LANES]
            piece()
        return run

    return [diff_q, sb_q, diff_kv(1, kd_ref, kd16_ref, sample[2]), diff_kv(2, vd_ref, vd16_ref, sample[3]),
            lambda: put_ks(proj, state), lambda: put_vs(proj, state), lambda: None, sample[4]]


def _inproj_sample_kernel(x_ref, g_ref, w_ref, qd_ref, qs_ref, kd_ref, vd_ref, ks_ref, vs_ref,
                          kd16_ref, vd16_ref, ks16_ref, vs16_ref):
    def put_qs(q):
        qs_ref[...] = q

    def put(c, out_ref, out16_ref):
        def run(proj, state):
            r = proj(c)
            out_ref[...] = r
            out16_ref[...] = r.astype(BF16)
        return run

    for slot in _inproj_slots(x_ref, g_ref, w_ref, None, qd_ref, kd_ref, vd_ref, kd16_ref, vd16_ref, put_qs,
                              put(4, ks_ref, ks16_ref), put(5, vs_ref, vs16_ref), [lambda: None] * 5):
        slot()


def _inproj_sample(x, g, w16):
    rows = x.shape[0]
    row_spec = lambda width: pl.BlockSpec((ROW_TILE, width), lambda i: (i, 0))
    head_major = pl.BlockSpec((ROW_TILE * DIFF_HEADS, HEAD_LANES), lambda i: (i, 0))
    bf16_rows = jax.ShapeDtypeStruct((rows, GROUP_WIDTH), BF16)
    f32_rows = jax.ShapeDtypeStruct((rows, GROUP_WIDTH), F32)
    f32_head_major = jax.ShapeDtypeStruct((rows * DIFF_HEADS, HEAD_LANES), F32)
    return pl.pallas_call(
        _inproj_sample_kernel,
        grid=(rows // ROW_TILE,),
        in_specs=[row_spec(D_MODEL), pl.BlockSpec((1, D_MODEL), lambda i: (0, 0)),
                  pl.BlockSpec(w16.shape, lambda i: (0, 0))],
        out_specs=[row_spec(GROUP_WIDTH), row_spec(GROUP_WIDTH), head_major, head_major, row_spec(GROUP_WIDTH),
                   row_spec(GROUP_WIDTH), row_spec(GROUP_WIDTH), row_spec(GROUP_WIDTH), row_spec(GROUP_WIDTH),
                   row_spec(GROUP_WIDTH)],
        out_shape=[bf16_rows, bf16_rows, f32_head_major, f32_head_major, f32_rows, f32_rows,
                   bf16_rows, bf16_rows, bf16_rows, bf16_rows],
        compiler_params=pltpu.CompilerParams(dimension_semantics=("parallel",), vmem_limit_bytes=VMEM_LIMIT),
        name="inproj_sample",
    )(x, g, w16)


def _ffn_pieces(n_pieces, x_ref, od, osb_ref, wout_ref, gffn_ref, wg_ref, wu_ref, wd_ref, gfin_ref, y_ref, act_ref):
    state = {}
    n_chunks = D_FF // FF_CHUNK

    def first():
        mixed = jnp.concatenate([od(), osb_ref[...]], axis=-1)
        state["h"] = x_ref[...] + jnp.dot(mixed, wout_ref[...], preferred_element_type=F32)
        state["n"] = _rms(state["h"], gffn_ref[...]).astype(BF16)

    def chunks(lo, hi):
        def run():
            for c in range(lo, hi):
                cols = slice(c * FF_CHUNK, (c + 1) * FF_CHUNK)
                gate = jnp.dot(state["n"], wg_ref[:, cols], preferred_element_type=F32)
                up = jnp.dot(state["n"], wu_ref[:, cols], preferred_element_type=F32)
                act_ref[:, cols] = (gate / (1.0 + jnp.exp(-gate)) * up).astype(BF16)
        return run

    def last():
        y = state["h"] + jnp.dot(act_ref[...], wd_ref[...], preferred_element_type=F32)
        y_ref[...] = _rms(y, gfin_ref[...])

    middle = n_pieces - 2
    bounds = [(n_chunks * m) // middle for m in range(middle + 1)]
    return [first] + [chunks(bounds[m], bounds[m + 1]) for m in range(middle)] + [last]


def _ffn_kernel(x_ref, od_ref, osb_ref, wout_ref, gffn_ref, wg_ref, wu_ref, wd_ref, gfin_ref, y_ref, act_ref):
    for piece in _ffn_pieces(3, x_ref, lambda: od_ref[...], osb_ref, wout_ref, gffn_ref, wg_ref, wu_ref, wd_ref,
                             gfin_ref, y_ref, act_ref):
        piece()


def _ffn(x, od, osb, wout16, gffn, wg16, wu16, wd16, gfin):
    rows = x.shape[0]
    row_spec = lambda width: pl.BlockSpec((ROW_TILE, width), lambda i: (i, 0))
    full = lambda a: pl.BlockSpec(a.shape, lambda i: (0, 0))
    return pl.pallas_call(
        _ffn_kernel,
        grid=(rows // ROW_TILE,),
        in_specs=[row_spec(D_MODEL), row_spec(GROUP_WIDTH), row_spec(GROUP_WIDTH),
                  full(wout16), full(gffn), full(wg16), full(wu16), full(wd16), full(gfin)],
        out_specs=row_spec(D_MODEL),
        out_shape=jax.ShapeDtypeStruct((rows, D_MODEL), F32),
        scratch_shapes=[pltpu.VMEM((ROW_TILE, D_FF), BF16)],
        compiler_params=pltpu.CompilerParams(dimension_semantics=("parallel",), vmem_limit_bytes=VMEM_LIMIT),
        name="outproj_ffn",
    )(x, od, osb, wout16, gffn, wg16, wu16, wd16, gfin)


def _subln(o, g):
    o = o * lax.rsqrt(jnp.mean(o * o, axis=-1, keepdims=True) + RMS_EPS)
    return o * g * (1.0 - LAMBDA_INIT)


def _diff_head(slope2, lam, g_ref, q_ref, k_ref, v_ref, bias_ref, emit, between):
    r = lax.broadcasted_iota(jnp.int32, (TQ, TK), 0)
    c = lax.broadcasted_iota(jnp.int32, (TQ, TK), 1)
    bias_ref[...] = jnp.where((c // CHUNK) <= (r // CHUNK),
                              jnp.where(c > r, (-2.0 * slope2) * (c - r).astype(F32), 0.0), NEG_INF)
    lane = lax.broadcasted_iota(jnp.int32, (TQ, HEAD_LANES), 1)
    key_bias = slope2 * lax.broadcasted_iota(jnp.int32, (BF16_ROWS, SEQ), 1).astype(F32)
    row = lax.broadcasted_iota(jnp.int32, (BF16_ROWS, SEQ), 0)
    hi = key_bias.astype(BF16).astype(F32)
    mid = (key_bias - hi).astype(BF16).astype(F32)
    lo = key_bias - hi - mid
    split = jnp.where(row == 0, hi, jnp.where(row == 1, mid, jnp.where(row == 2, lo, 0.0))).astype(BF16)
    kt = jnp.concatenate([jnp.transpose(k_ref[...]), split,
                          jnp.zeros((HEAD_LANES - BF16_ROWS, SEQ), BF16)], axis=0)
    ones = jnp.where(lane < ALIBI_TERMS, 1.0, 0.0).astype(BF16)

    def scores(i):
        q = q_ref[i * TQ:(i + 1) * TQ, :]
        qms = [jnp.concatenate([jnp.where((lane < HALF) == first, q, jnp.zeros_like(q)), ones], axis=-1)
               for first in (True, False)]
        raw = [jnp.dot(qm, kt[:, :TK * (i + 1)], preferred_element_type=F32) for qm in qms]
        return [jnp.concatenate([s[:, :TK * i], s[:, TK * i:] + bias_ref[...]], axis=-1) if i else
                s + bias_ref[...] for s in raw]

    def attend(i, s):
        e = [jnp.exp2(sm - jnp.max(sm, axis=-1, keepdims=True)) for sm in s]
        l = [jnp.sum(em, axis=-1, keepdims=True) for em in e]
        a = (e[0] - (lam * l[0] / l[1]) * e[1]).astype(BF16)
        o = jnp.dot(a, v_ref[0:TK * (i + 1), :], preferred_element_type=F32) / l[0]
        emit(i, _subln(o, g_ref[...]).astype(BF16))

    n_q = SEQ // TQ
    s_next = scores(0)
    for i in range(n_q):
        s_now = s_next
        if i + 1 < n_q:
            s_next = scores(i + 1)
        between(i)
        attend(i, s_now)


def _diff_ffn_kernel(slope_ref, lamv_ref, gsub_ref, q_ref, k_ref, v_ref, x_ref, osb_ref, wout_ref, gffn_ref,
                     wg_ref, wu_ref, wd_ref, gfin_ref, y_ref, bias_ref, od_buf, act_ref):
    b, h = pl.program_id(0), pl.program_id(1)
    slot = b % 2
    slope2 = slope_ref[h] * LOG2E

    def emit(i, o):
        od_buf[slot, h, i * TQ:(i + 1) * TQ, :] = o

    @pl.when(b == 0)
    def _():
        _diff_head(slope2, _lambda(lamv_ref), gsub_ref, q_ref, k_ref, v_ref, bias_ref, emit, lambda i: None)

    @pl.when(b > 0)
    def _():
        tile_rows = pl.ds(pl.multiple_of(h * ROW_TILE, ROW_TILE), ROW_TILE)
        previous = lambda: jnp.concatenate([od_buf[1 - slot, hh, tile_rows, :] for hh in range(DIFF_HEADS)],
                                           axis=-1)
        pieces = _ffn_pieces(SEQ // TQ, x_ref, previous, osb_ref, wout_ref, gffn_ref, wg_ref, wu_ref, wd_ref,
                             gfin_ref, y_ref, act_ref)
        _diff_head(slope2, _lambda(lamv_ref), gsub_ref, q_ref, k_ref, v_ref, bias_ref, emit,
                   lambda i: pieces[i]())


def _diff_ffn(slopes, lamv, gsub, qkv16, x, osb, wout16, gffn, wg16, wu16, wd16, gfin):
    rows = x.shape[0]
    batch = rows // SEQ
    assert SEQ // ROW_TILE == DIFF_HEADS
    head_spec = lambda c: pl.BlockSpec((SEQ, HEAD_LANES),
                                       lambda b, h: (jnp.minimum(b, batch - 1), c * DIFF_HEADS + h))
    tile = lambda b, h: (jnp.where(b == 0, 0, (b - 1) * DIFF_HEADS + h), 0)
    row_spec = lambda width: pl.BlockSpec((ROW_TILE, width), tile)
    const = lambda a: pl.BlockSpec(a.shape, lambda b, h: (0, 0))
    weight = lambda a: pl.BlockSpec(a.shape, lambda b, h: (0, 0), pipeline_mode=pl.Buffered(1))
    return pl.pallas_call(
        _diff_ffn_kernel,
        grid=(batch + 1, DIFF_HEADS),
        in_specs=[pl.BlockSpec(memory_space=pltpu.SMEM), const(lamv), const(gsub), head_spec(0), head_spec(1),
                  head_spec(2), row_spec(D_MODEL), row_spec(GROUP_WIDTH), weight(wout16), const(gffn), weight(wg16), weight(wu16),
                  weight(wd16), const(gfin)],
        out_specs=row_spec(D_MODEL),
        out_shape=jax.ShapeDtypeStruct((rows, D_MODEL), F32),
        scratch_shapes=[pltpu.VMEM((TQ, TK), F32), pltpu.VMEM((2, DIFF_HEADS, SEQ, HEAD_LANES), BF16),
                        pltpu.VMEM((ROW_TILE, D_FF), BF16)],
        compiler_params=pltpu.CompilerParams(dimension_semantics=("arbitrary", "arbitrary"),
                                             vmem_limit_bytes=FUSED_VMEM_LIMIT),
        name="diff_prompt_ffn",
    )(slopes, lamv, gsub, qkv16, qkv16, qkv16, x, osb, wout16, gffn, wg16, wu16, wd16, gfin)


def _sb_prompt_steps(q_ref, kt_ref, vt_ref, o_ref, u_ref, carry_ref, acc_ref):
    u_ref[...] = _suffix_matrix(SB_BLOCK)
    causal = (lax.broadcasted_iota(jnp.int32, (SB_BLOCK, SB_BLOCK), 1)
              < lax.broadcasted_iota(jnp.int32, (SB_BLOCK, SB_BLOCK), 0))
    lane = lax.broadcasted_iota(jnp.int32, (SB_BLOCK, HEAD_LANES), 1)

    def rows_of(i):
        return pl.ds(i * SB_BLOCK, SB_BLOCK)

    def begin(slot, i):
        q = q_ref[rows_of(i), :]
        carry_ref[slot] = jnp.zeros(carry_ref.shape[1:], F32)
        acc_ref[slot] = jnp.zeros(acc_ref.shape[1:], F32)
        return jnp.where(lane < HALF, q, jnp.zeros_like(q)), jnp.where(lane >= HALF, q, jnp.zeros_like(q))

    def add_blocks(items, diagonal, between=(lambda: None, lambda: None)):
        chains = [(n, h, t) for n, item in enumerate(items) for h in range(2) for t in reversed(range(item[3]))]
        mask = {c: causal if diagonal and c[2] == items[c[0]][3] - 1 else None for c in chains}
        z = {(n, h, t): jnp.dot(items[n][1][h], kt_ref[items[n][2] + t], preferred_element_type=F32)
             for n, h, t in chains}
        between[0]()
        part = {c: _stick_block(z[c], u_ref[...], mask[c]) for c in chains}
        between[1]()
        carry = {(n, h): carry_ref[item[0], h] for n, item in enumerate(items) for h in range(2)}
        weights = {}
        for n, h, t in chains:
            weights[n, h, t] = _stick_weights(part[n, h, t][0], carry[n, h], mask[n, h, t]).astype(BF16)
            carry[n, h] = carry[n, h] + part[n, h, t][1]
        for n, (slot, _, first, count) in enumerate(items):
            for h in range(2):
                acc = acc_ref[slot, h]
                for t in range(count):
                    acc = acc + lax.dot_general(weights[n, h, t], vt_ref[first + t], NT_DIMS,
                                                preferred_element_type=F32)
                acc_ref[slot, h] = acc
                carry_ref[slot, h] = carry[n, h]

    def finish(slot, qms, i, j):
        def unfinished(state):
            j, top = state
            return jnp.logical_and(j >= 0, top > EXIT_LOG2)

        def earlier(state):
            j, _ = state
            add_blocks([(slot, qms, j, 1)], False)
            return j - 1, jnp.max(carry_ref[slot])

        lax.while_loop(unfinished, earlier, (jnp.int32(j), jnp.max(carry_ref[slot])))
        o_ref[rows_of(i), :] = jnp.where(lane < HALF, acc_ref[slot, 0], acc_ref[slot, 1]).astype(BF16)

    def query_pair(i_a, count_a, i_b, between):
        qms_a, qms_b = begin(0, i_a), begin(1, i_b)
        first_a = i_a - (count_a - 1)
        add_blocks([(0, qms_a, first_a, count_a), (1, qms_b, i_b - 1, 2)], True, between)
        finish(0, qms_a, i_a, first_a - 1)
        finish(1, qms_b, i_b, i_b - 2)

    n_q = SEQ // SB_BLOCK
    pairs = [(0, 1, n_q - 1)] + [(2 * m + 1, 2, 2 * m + 2) for m in range((n_q - 2) // 2)]
    return [functools.partial(query_pair, *p) for p in pairs]


def _inproj_sb_kernel(batch, x_ref, g_ref, w_ref, wt_ref, qs2_ref, kn_ref, vn_ref, kct_ref, vct_ref,
                      qkv16_ref, kd_ref, vd_ref, ks_ref, vs_ref, osb2_ref, osb_ref,
                      scarry_ref, sacc_ref, q_buf, kt_buf, vt_buf, u_ref, carry_ref, acc_ref):
    b, t = pl.program_id(0), pl.program_id(1)
    slot = b % 2
    tile_rows = pl.ds(pl.multiple_of(t * ROW_TILE, ROW_TILE), ROW_TILE)
    blocks_per_tile = ROW_TILE // SB_BLOCK

    def put_qs(q):
        for p in range(SB_PAIRS):
            q_buf[slot, p, tile_rows, :] = q[:, p * HEAD_LANES:(p + 1) * HEAD_LANES]

    def put_transposed(c, out_ref, buf):
        def run(proj, state):
            rows = slice((c - 4) * GROUP_WIDTH, (c - 3) * GROUP_WIDTH)
            r = lax.dot_general(wt_ref[rows, :], state["n"], NT_DIMS, preferred_element_type=F32)
            out_ref[...] = r
            for p in range(SB_PAIRS):
                for j in range(blocks_per_tile):
                    buf[slot, p, t * blocks_per_tile + j] = r[p * HEAD_LANES:(p + 1) * HEAD_LANES,
                                                              j * SB_BLOCK:(j + 1) * SB_BLOCK].astype(BF16)
        return run

    def projection_slots():
        sample = _sb_sample_pieces(qs2_ref, kn_ref, vn_ref, kct_ref, vct_ref, osb2_ref, scarry_ref, sacc_ref)
        qd_ref, kd16_ref, vd16_ref = (qkv16_ref.at[:, pl.ds(c * GROUP_WIDTH, GROUP_WIDTH)] for c in range(3))
        return _inproj_slots(x_ref, g_ref, w_ref, wt_ref, qd_ref, kd_ref, vd_ref, kd16_ref, vd16_ref, put_qs,
                             put_transposed(4, ks_ref, kt_buf), put_transposed(5, vs_ref, vt_buf), sample)

    def attention_steps():
        return _sb_prompt_steps(q_buf.at[1 - slot, t], kt_buf.at[1 - slot, t], vt_buf.at[1 - slot, t], osb_ref,
                                u_ref, carry_ref, acc_ref)

    @pl.when(b == 0)
    def _():
        for piece in projection_slots():
            piece()

    @pl.when(jnp.logical_and(b > 0, b < batch))
    def _():
        slots = projection_slots()
        for k, step in enumerate(attention_steps()):
            step((slots[2 * k], slots[2 * k + 1]))

    @pl.when(b == batch)
    def _():
        for step in attention_steps():
            step((lambda: None, lambda: None))


def _inproj_sb(x, g, w16, wt16, qs2, ks2_16, vs2_16, cache_kt, cache_vt):
    rows = x.shape[0]
    batch = rows // SEQ
    tiles = SEQ // ROW_TILE
    assert tiles == SB_PAIRS and cache_kt.shape[0] == batch
    tile = lambda b, t: jnp.where(b == batch, batch * tiles - 1, b * tiles + t)
    row_spec = lambda width: pl.BlockSpec((ROW_TILE, width), lambda b, t: (tile(b, t), 0))
    head_major = pl.BlockSpec((ROW_TILE * DIFF_HEADS, HEAD_LANES), lambda b, t: (tile(b, t), 0))
    weight = lambda a: pl.BlockSpec(a.shape, lambda b, t: (0, 0), pipeline_mode=pl.Buffered(1))
    sb32_spec = pl.BlockSpec((None, GROUP_WIDTH, ROW_TILE), lambda b, t: (tile(b, t) // tiles, 0, tile(b, t) % tiles))
    new_spec = pl.BlockSpec((DEC_SEQ, HEAD_LANES), lambda b, t: (tile(b, t) // tiles, tile(b, t) % tiles))
    cache_spec = pl.BlockSpec((None, HEAD_LANES, PAST_LEN), lambda b, t: (tile(b, t) // tiles, tile(b, t) % tiles, 0))
    pair_spec = pl.BlockSpec((SEQ, HEAD_LANES), lambda b, t: (jnp.maximum(b - 1, 0), jnp.where(b == 0, 0, t)))
    bf16_rows = jax.ShapeDtypeStruct((rows, GROUP_WIDTH), BF16)
    f32_head_major = jax.ShapeDtypeStruct((rows * DIFF_HEADS, HEAD_LANES), F32)
    sb32_shape = jax.ShapeDtypeStruct((batch, GROUP_WIDTH, SEQ), F32)
    n_blocks = SEQ // SB_BLOCK
    return pl.pallas_call(
        functools.partial(_inproj_sb_kernel, batch),
        grid=(batch + 1, tiles),
        in_specs=[row_spec(D_MODEL), pl.BlockSpec((1, D_MODEL), lambda b, t: (0, 0)), weight(w16), weight(wt16),
                  new_spec, new_spec, new_spec, cache_spec, cache_spec],
        out_specs=[row_spec(3 * GROUP_WIDTH), head_major, head_major, sb32_spec, sb32_spec, new_spec, pair_spec],
        out_shape=[jax.ShapeDtypeStruct((rows, 3 * GROUP_WIDTH), BF16), f32_head_major, f32_head_major, sb32_shape,
                   sb32_shape, jax.ShapeDtypeStruct(qs2.shape, BF16), bf16_rows],
        scratch_shapes=[pltpu.VMEM((HEAD_LANES, STAT_LANES), F32), pltpu.VMEM((HEAD_LANES, HEAD_LANES), F32),
                        pltpu.VMEM((2, SB_PAIRS, SEQ, HEAD_LANES), BF16),
                        pltpu.VMEM((2, SB_PAIRS, n_blocks, HEAD_LANES, SB_BLOCK), BF16),
                        pltpu.VMEM((2, SB_PAIRS, n_blocks, HEAD_LANES, SB_BLOCK), BF16),
                        pltpu.VMEM((SB_BLOCK, SB_BLOCK), BF16), pltpu.VMEM((2, 2, SB_BLOCK, STAT_LANES), F32),
                        pltpu.VMEM((2, 2, SB_BLOCK, HEAD_LANES), F32)],
        compiler_params=pltpu.CompilerParams(dimension_semantics=("arbitrary", "arbitrary"),
                                             vmem_limit_bytes=VMEM_LIMIT),
        name="inproj_sb_prompt",
    )(x, g, w16, wt16, qs2, ks2_16, vs2_16, cache_kt, cache_vt)


def _stack_queries(q):
    lane = lax.broadcasted_iota(jnp.int32, q.shape, 1)
    zero = jnp.zeros_like(q)
    return jnp.concatenate([jnp.where(lane < HALF, q, zero), jnp.where(lane >= HALF, q, zero)], axis=0)


def _diff_sample_kernel(slope_ref, lamv_ref, g_ref, q_ref, kn_ref, vn_ref, kc_ref, vc_ref, o_ref):
    lam = _lambda(lamv_ref)
    kpos = lax.broadcasted_iota(jnp.int32, (PAST_LEN, HEAD_LANES), 0)
    qpos = PAST_LEN + (lax.broadcasted_iota(jnp.int32, (PAST_LEN, HEAD_LANES), 1) % HALF)
    dist_c = (qpos - kpos).astype(F32)
    kn_pos = lax.broadcasted_iota(jnp.int32, (DEC_SEQ, HEAD_LANES), 0)
    qn_pos = lax.broadcasted_iota(jnp.int32, (DEC_SEQ, HEAD_LANES), 1) % HALF
    dist_n = jnp.abs(qn_pos - kn_pos).astype(F32)
    cols = [slice(h * HEAD_LANES, (h + 1) * HEAD_LANES) for h in range(DIFF_HEADS)]
    head_rows = lambda ref, h: ref[pl.ds(h, PAST_LEN, stride=DIFF_HEADS), :].astype(BF16)
    for heads in ((0, 1), (2, 3)):
        s_c, s_n = {}, {}
        for h in heads:
            slope2 = slope_ref[h] * LOG2E
            q2 = _stack_queries(q_ref[:, cols[h]])
            s_c[h] = lax.dot_general(head_rows(kc_ref, h), q2, NT_DIMS, preferred_element_type=F32) - slope2 * dist_c
            s_n[h] = lax.dot_general(kn_ref[:, cols[h]], q2, NT_DIMS, preferred_element_type=F32) - slope2 * dist_n
        p_c, p_n, l = {}, {}, {}
        for h in heads:
            m = jnp.maximum(jnp.max(s_c[h], axis=0, keepdims=True), jnp.max(s_n[h], axis=0, keepdims=True))
            p_c[h] = jnp.exp2(s_c[h] - m)
            p_n[h] = jnp.exp2(s_n[h] - m)
            l[h] = jnp.sum(p_c[h], axis=0, keepdims=True) + jnp.sum(p_n[h], axis=0, keepdims=True)
        for h in heads:
            acc = (lax.dot_general(p_c[h].astype(BF16), head_rows(vc_ref, h), TN_DIMS, preferred_element_type=F32)
                   + lax.dot_general(p_n[h].astype(BF16), vn_ref[:, cols[h]], TN_DIMS, preferred_element_type=F32))
            l_col = jnp.transpose(jnp.broadcast_to(l[h], (HEAD_LANES, HEAD_LANES)))[:, 0:1]
            o = acc / l_col
            o = o[0:DEC_SEQ] - lam * o[DEC_SEQ:2 * DEC_SEQ]
            o_ref[:, cols[h]] = _subln(o, g_ref[...]).astype(BF16)


def _diff_sample(slopes, lamv, g, qd, kd16, vd16, cache_k, cache_v):
    batch = cache_k.shape[0]
    new_spec = pl.BlockSpec((DEC_SEQ, GROUP_WIDTH), lambda b: (b, 0))
    cache_spec = pl.BlockSpec((None, PAST_LEN * DIFF_HEADS, HEAD_LANES), lambda b: (b, 0, 0))
    return pl.pallas_call(
        _diff_sample_kernel,
        grid=(batch,),
        in_specs=[pl.BlockSpec(memory_space=pltpu.SMEM),
                  pl.BlockSpec(lamv.shape, lambda b: (0, 0)),
                  pl.BlockSpec(g.shape, lambda b: (0, 0)),
                  new_spec, new_spec, new_spec, cache_spec, cache_spec],
        out_specs=new_spec,
        out_shape=jax.ShapeDtypeStruct(qd.shape, BF16),
        compiler_params=pltpu.CompilerParams(dimension_semantics=("parallel",), vmem_limit_bytes=VMEM_LIMIT),
        name="diff_sample",
    )(slopes, lamv, g, qd, kd16, vd16, cache_k, cache_v)


def _sb_sample_pieces(q_ref, kn_ref, vn_ref, kct_ref, vct_ref, o_ref, carry_ref, acc_ref):
    state = {}
    n_sweeps = PAST_LEN // SAMPLE_SWEEP
    n_blocks = SAMPLE_SWEEP // CUM_BLOCK
    last_cols = slice((n_sweeps - 1) * SAMPLE_SWEEP, n_sweeps * SAMPLE_SWEEP)
    key_idx = lax.broadcasted_iota(jnp.int32, (HEAD_LANES, DEC_SEQ), 1)
    query_idx = lax.broadcasted_iota(jnp.int32, (HEAD_LANES, DEC_SEQ), 0) % HALF
    visible = key_idx < query_idx

    def block_parts(z):
        return [_stick_block(z[:, b * CUM_BLOCK:(b + 1) * CUM_BLOCK], state["suffix"], None) for b in range(n_blocks)]

    def add_values(parts, cols):
        carry = carry_ref[...]
        weights = [None] * n_blocks
        for b in reversed(range(n_blocks)):
            weights[b] = _stick_weights(parts[b][0], carry, None)
            carry = carry + parts[b][1]
        a = jnp.concatenate(weights, axis=-1).astype(BF16)
        acc_ref[...] += lax.dot_general(a, vct_ref[:, cols].astype(BF16), NT_DIMS, preferred_element_type=F32)
        carry_ref[...] = carry
        return jnp.max(carry)

    def logits():
        state["q2"] = _stack_queries(q_ref[...])
        state["suffix"] = _suffix_matrix(CUM_BLOCK)
        state["z_new"] = lax.dot_general(state["q2"], kn_ref[...], NT_DIMS, preferred_element_type=F32)
        state["z_last"] = jnp.dot(state["q2"], kct_ref[:, last_cols].astype(BF16), preferred_element_type=F32)

    def block_sums():
        state["new"] = _stick_block(state["z_new"], _suffix_matrix(DEC_SEQ), visible)
        state["last"] = block_parts(state["z_last"])

    def new_values():
        log2w, carry = state["new"]
        a = jnp.where(visible, jnp.exp2(log2w), 0.0)
        acc_ref[...] = jnp.dot(a.astype(BF16), vn_ref[...], preferred_element_type=F32)
        carry_ref[...] = jnp.broadcast_to(carry, carry_ref.shape)

    def last_values():
        state["top"] = add_values(state["last"], last_cols)

    def earlier():
        def unfinished(loop):
            j, top = loop
            return jnp.logical_and(j >= 0, top > EXIT_LOG2)

        def body(loop):
            j, _ = loop
            cols = pl.ds(pl.multiple_of(j * SAMPLE_SWEEP, SAMPLE_SWEEP), SAMPLE_SWEEP)
            z = jnp.dot(state["q2"], kct_ref[:, cols].astype(BF16), preferred_element_type=F32)
            return j - 1, add_values(block_parts(z), cols)

        lax.while_loop(unfinished, body, (n_sweeps - 2, state["top"]))
        lane = lax.broadcasted_iota(jnp.int32, (DEC_SEQ, HEAD_LANES), 1)
        o_ref[...] = jnp.where(lane < HALF, acc_ref[0:DEC_SEQ, :], acc_ref[DEC_SEQ:2 * DEC_SEQ, :]).astype(BF16)

    return [logits, block_sums, new_values, last_values, earlier]


def kernel(x_prompt, x_sample, cache_diff_k, cache_diff_v, cache_sb_k, cache_sb_v, norm_attn_g, w_in,
           lambda_q1, lambda_k1, lambda_q2, lambda_k2, diff_subln_g, w_out, norm_ffn_g, w_gate, w_up, w_down,
           norm_final_g):
    batch, seq, _ = x_prompt.shape
    dec_batch, dec_seq, _ = x_sample.shape
    assert seq == SEQ and dec_seq == DEC_SEQ and cache_diff_k.shape[2] == PAST_LEN and w_in.shape[0] == 1

    w_in16 = w_in[0].astype(BF16)
    w_sb_t16 = jnp.transpose(w_in[0][:, 4 * GROUP_WIDTH:]).astype(BF16)
    w_out16 = w_out[0].astype(BF16)
    w_gate16 = w_gate[0].astype(BF16)
    w_up16 = w_up[0].astype(BF16)
    w_down16 = w_down[0].astype(BF16)
    g_attn = norm_attn_g[0].reshape(1, D_MODEL)
    g_ffn = norm_ffn_g[0].reshape(1, D_MODEL)
    g_final = norm_final_g.reshape(1, D_MODEL)
    g_subln = diff_subln_g[0].reshape(1, HEAD_LANES)
    lamv = jnp.concatenate([lambda_q1, lambda_k1, lambda_q2, lambda_k2], axis=0).astype(F32)
    slopes = jnp.exp2(-8.0 / DIFF_HEADS * jnp.arange(1, DIFF_HEADS + 1, dtype=F32))

    def ffn(x, od, osb):
        return _ffn(x, od, osb, w_out16, g_ffn, w_gate16, w_up16, w_down16, g_final)

    xs = x_sample.reshape(dec_batch * DEC_SEQ, D_MODEL)
    qd2, qs2, kd2, vd2, ks2, vs2, kd2_16, vd2_16, ks2_16, vs2_16 = _inproj_sample(xs, g_attn, w_in16)
    keys_minor = lambda a: jnp.transpose(a[0], (0, 2, 3, 1)).reshape(dec_batch, GROUP_WIDTH, PAST_LEN)

    xp = x_prompt.reshape(batch * SEQ, D_MODEL)
    qkv16, kd, vd, kst, vst, osb2, osb = _inproj_sb(
        xp, g_attn, w_in16, w_sb_t16, qs2, ks2_16, vs2_16, keys_minor(cache_sb_k), keys_minor(cache_sb_v))
    y_prompt = _diff_ffn(slopes, lamv, g_subln, qkv16, xp, osb, w_out16, g_ffn, w_gate16, w_up16, w_down16,
                         g_final).reshape(batch, SEQ, D_MODEL)

    head_major = lambda a: a[0].reshape(dec_batch, PAST_LEN * DIFF_HEADS, HEAD_LANES)
    od2 = _diff_sample(slopes, lamv, g_subln, qd2, kd2_16, vd2_16, head_major(cache_diff_k), head_major(cache_diff_v))
    y_sample = ffn(xs, od2, osb2).reshape(dec_batch, DEC_SEQ, D_MODEL)

    diff_shape = lambda b, t: (1, b, t, DIFF_HEADS, HEAD_LANES)
    sb_shape = lambda b, t: (1, b, t, 2 * SB_PAIRS, HALF)
    from_keys_minor = lambda a: jnp.transpose(a.reshape(batch, 2 * SB_PAIRS, HALF, SEQ), (0, 3, 1, 2))[None]
    return (y_prompt, y_sample,
            kd.reshape(diff_shape(batch, SEQ)), vd.reshape(diff_shape(batch, SEQ)),
            from_keys_minor(kst), from_keys_minor(vst),
            kd2.reshape(diff_shape(dec_batch, DEC_SEQ)), vd2.reshape(diff_shape(dec_batch, DEC_SEQ)),
            ks2.reshape(sb_shape(dec_batch, DEC_SEQ)), vs2.reshape(sb_shape(dec_batch, DEC_SEQ)))
```

```python
import functools
import math

import jax
import jax.numpy as jnp
from jax import lax
from jax.experimental import pallas as pl
from jax.experimental.pallas import tpu as pltpu

D_MODEL = 1024
SEQ = 2048
DEC_SEQ = 64
PAST_LEN = 4096
CHUNK = 64
GROUP_WIDTH = 512
DIFF_HEADS = 4
HEAD_LANES = 128
HALF = 64
SB_PAIRS = 4
D_FF = 2816
FF_CHUNK = 256
RMS_EPS = 1e-6
NEG_INF = -1e30
LOG2E = math.log2(math.e)
QK_SCALE = 0.125 * LOG2E
LAMBDA_INIT = 0.8 - 0.6 * math.exp(-0.3 * 0)

ROW_TILE = 512
TQ = 256
TK = TQ
CUM_BLOCK = 256
SB_BLOCK = CUM_BLOCK
SAMPLE_SWEEP = 512
SAMPLE_KEYS = 2048
EXIT_LOG2 = -160.0
STAT_LANES = 128
VMEM_LIMIT = 56 * 1024 * 1024
FUSED_VMEM_LIMIT = 58 * 1024 * 1024

F32 = jnp.float32
BF16 = jnp.bfloat16
BF16_ROWS = 16
ALIBI_TERMS = 3
NT_DIMS = (((1,), (1,)), ((), ()))
TN_DIMS = (((0,), (0,)), ((), ()))


def _rms(x, g):
    return x * lax.rsqrt(jnp.mean(x * x, axis=-1, keepdims=True) + RMS_EPS) * g


def _lambda(lamv_ref):
    lv = lamv_ref[...]
    a = jnp.sum(lv[0:1] * lv[1:2], axis=-1, keepdims=True)
    b = jnp.sum(lv[2:3] * lv[3:4], axis=-1, keepdims=True)
    return jnp.exp(a) - jnp.exp(b) + LAMBDA_INIT


def _log2_keep(z2):
    nz = -z2
    return jnp.minimum(nz, 0.0) - jnp.log(1.0 + jnp.exp2(jnp.minimum(z2, nz))) * LOG2E


def _lanes(x, width):
    if width <= STAT_LANES:
        return x[:, :width]
    return jnp.concatenate([x] * (width // STAT_LANES), axis=1)


def _suffix_matrix(n):
    j = lax.broadcasted_iota(jnp.int32, (n, n), 0)
    s = lax.broadcasted_iota(jnp.int32, (n, n), 1)
    return jnp.where(j > s, 1.0, 0.0).astype(BF16)


def _stick_block(z, suffix, visible):
    lk = _log2_keep(z)
    if visible is not None:
        lk = jnp.where(visible, lk, 0.0)
    within = jnp.dot(lk.astype(BF16), suffix, preferred_element_type=F32)
    return (z + lk) + within, jnp.sum(lk, axis=-1, keepdims=True)


def _stick_weights(log2w, carry, visible):
    a = jnp.exp2(log2w + _lanes(carry, log2w.shape[-1]))
    return a if visible is None else jnp.where(visible, a, 0.0)


def _inproj_slots(x_ref, g_ref, w_ref, wt_ref, qd_ref, kd_ref, vd_ref, kd16_ref, vd16_ref, put_qs, put_ks, put_vs,
                  sample):
    state = {}

    def proj(c):
        return jnp.dot(state["n"], w_ref[:, c * GROUP_WIDTH:(c + 1) * GROUP_WIDTH], preferred_element_type=F32)

    def diff_q():
        state["n"] = _rms(x_ref[...], g_ref[...]).astype(BF16)
        qd_ref[...] = (proj(0) * QK_SCALE).astype(BF16)
        sample[0]()

    def sb_q():
        put_qs((proj(3) * QK_SCALE).astype(BF16))
        sample[1]()

    def diff_kv(c, out_ref, out16_ref, piece):
        def run():
            r = proj(c)
            out16_ref[...] = r.astype(BF16)
            for h in range(DIFF_HEADS):
                out_ref[pl.ds(h, ROW_TILE, stride=DIFF_HEADS), :] = r[:, h * HEAD_LANES:(h + 1) * HEAD_LANES]
            piece()
        return run

    return [diff_q, sb_q, diff_kv(1, kd_ref, kd16_ref, sample[2]), diff_kv(2, vd_ref, vd16_ref, sample[3]),
            lambda: put_ks(proj, state), lambda: put_vs(proj, state), lambda: None, sample[4]]


def _inproj_sample_kernel(x_ref, g_ref, w_ref, qd_ref, qs_ref, kd_ref, vd_ref, ks_ref, vs_ref,
                          kd16_ref, vd16_ref, ks16_ref, vs16_ref):
    def put_qs(q):
        qs_ref[...] = q

    def put(c, out_ref, out16_ref):
        def run(proj, state):
            r = proj(c)
            out_ref[...] = r
            out16_ref[...] = r.astype(BF16)
        return run

    for slot in _inproj_slots(x_ref, g_ref, w_ref, None, qd_ref, kd_ref, vd_ref, kd16_ref, vd16_ref, put_qs,
                              put(4, ks_ref, ks16_ref), put(5, vs_ref, vs16_ref), [lambda: None] * 5):
        slot()


def _inproj_sample(x, g, w16):
    rows = x.shape[0]
    row_spec = lambda width: pl.BlockSpec((ROW_TILE, width), lambda i: (i, 0))
    head_major = pl.BlockSpec((ROW_TILE * DIFF_HEADS, HEAD_LANES), lambda i: (i, 0))
    bf16_rows = jax.ShapeDtypeStruct((rows, GROUP_WIDTH), BF16)
    f32_rows = jax.ShapeDtypeStruct((rows, GROUP_WIDTH), F32)
    f32_head_major = jax.ShapeDtypeStruct((rows * DIFF_HEADS, HEAD_LANES), F32)
    return pl.pallas_call(
        _inproj_sample_kernel,
        grid=(rows // ROW_TILE,),
        in_specs=[row_spec(D_MODEL), pl.BlockSpec((1, D_MODEL), lambda i: (0, 0)),
                  pl.BlockSpec(w16.shape, lambda i: (0, 0))],
        out_specs=[row_spec(GROUP_WIDTH), row_spec(GROUP_WIDTH), head_major, head_major, row_spec(GROUP_WIDTH),
                   row_spec(GROUP_WIDTH), row_spec(GROUP_WIDTH), row_spec(GROUP_WIDTH), row_spec(GROUP_WIDTH),
                   row_spec(GROUP_WIDTH)],
        out_shape=[bf16_rows, bf16_rows, f32_head_major, f32_head_major, f32_rows, f32_rows,
                   bf16_rows, bf16_rows, bf16_rows, bf16_rows],
        compiler_params=pltpu.CompilerParams(dimension_semantics=("parallel",), vmem_limit_bytes=VMEM_LIMIT),
        name="inproj_sample",
    )(x, g, w16)


def _ffn_pieces(n_pieces, x_ref, od, osb_ref, wout_ref, gffn_ref, wg_ref, wu_ref, wd_ref, gfin_ref, y_ref, act_ref):
    state = {}
    n_chunks = D_FF // FF_CHUNK

    def first():
        mixed = jnp.concatenate([od(), osb_ref[...]], axis=-1)
        state["h"] = x_ref[...] + jnp.dot(mixed, wout_ref[...], preferred_element_type=F32)
        state["n"] = _rms(state["h"], gffn_ref[...]).astype(BF16)

    def chunks(lo, hi):
        def run():
            for c in range(lo, hi):
                cols = slice(c * FF_CHUNK, (c + 1) * FF_CHUNK)
                gate = jnp.dot(state["n"], wg_ref[:, cols], preferred_element_type=F32)
                up = jnp.dot(state["n"], wu_ref[:, cols], preferred_element_type=F32)
                act_ref[:, cols] = (gate / (1.0 + jnp.exp(-gate)) * up).astype(BF16)
        return run

    def last():
        y = state["h"] + jnp.dot(act_ref[...], wd_ref[...], preferred_element_type=F32)
        y_ref[...] = _rms(y, gfin_ref[...])

    middle = n_pieces - 2
    bounds = [(n_chunks * m) // middle for m in range(middle + 1)]
    return [first] + [chunks(bounds[m], bounds[m + 1]) for m in range(middle)] + [last]


def _ffn_kernel(x_ref, od_ref, osb_ref, wout_ref, gffn_ref, wg_ref, wu_ref, wd_ref, gfin_ref, y_ref, act_ref):
    for piece in _ffn_pieces(3, x_ref, lambda: od_ref[...], osb_ref, wout_ref, gffn_ref, wg_ref, wu_ref, wd_ref,
                             gfin_ref, y_ref, act_ref):
        piece()


def _ffn(x, od, osb, wout16, gffn, wg16, wu16, wd16, gfin):
    rows = x.shape[0]
    row_spec = lambda width: pl.BlockSpec((ROW_TILE, width), lambda i: (i, 0))
    full = lambda a: pl.BlockSpec(a.shape, lambda i: (0, 0))
    return pl.pallas_call(
        _ffn_kernel,
        grid=(rows // ROW_TILE,),
        in_specs=[row_spec(D_MODEL), row_spec(GROUP_WIDTH), row_spec(GROUP_WIDTH),
                  full(wout16), full(gffn), full(wg16), full(wu16), full(wd16), full(gfin)],
        out_specs=row_spec(D_MODEL),
        out_shape=jax.ShapeDtypeStruct((rows, D_MODEL), F32),
        scratch_shapes=[pltpu.VMEM((ROW_TILE, D_FF), BF16)],
        compiler_params=pltpu.CompilerParams(dimension_semantics=("parallel",), vmem_limit_bytes=VMEM_LIMIT),
        name="outproj_ffn",
    )(x, od, osb, wout16, gffn, wg16, wu16, wd16, gfin)


def _subln(o, g):
    o = o * lax.rsqrt(jnp.mean(o * o, axis=-1, keepdims=True) + RMS_EPS)
    return o * g * (1.0 - LAMBDA_INIT)


def _diff_head(slope2, lam, g_ref, q_ref, k_ref, v_ref, bias_ref, emit, between):
    r = lax.broadcasted_iota(jnp.int32, (TQ, TK), 0)
    c = lax.broadcasted_iota(jnp.int32, (TQ, TK), 1)
    bias_ref[...] = jnp.where((c // CHUNK) <= (r // CHUNK),
                              jnp.where(c > r, (-2.0 * slope2) * (c - r).astype(F32), 0.0), NEG_INF)
    lane = lax.broadcasted_iota(jnp.int32, (TQ, HEAD_LANES), 1)
    key_bias = slope2 * lax.broadcasted_iota(jnp.int32, (BF16_ROWS, SEQ), 1).astype(F32)
    row = lax.broadcasted_iota(jnp.int32, (BF16_ROWS, SEQ), 0)
    hi = key_bias.astype(BF16).astype(F32)
    mid = (key_bias - hi).astype(BF16).astype(F32)
    lo = key_bias - hi - mid
    split = jnp.where(row == 0, hi, jnp.where(row == 1, mid, jnp.where(row == 2, lo, 0.0))).astype(BF16)
    kt = jnp.concatenate([jnp.transpose(k_ref[...]), split,
                          jnp.zeros((HEAD_LANES - BF16_ROWS, SEQ), BF16)], axis=0)
    ones = jnp.where(lane < ALIBI_TERMS, 1.0, 0.0).astype(BF16)

    def scores(i):
        q = q_ref[i * TQ:(i + 1) * TQ, :]
        qms = [jnp.concatenate([jnp.where((lane < HALF) == first, q, jnp.zeros_like(q)), ones], axis=-1)
               for first in (True, False)]
        raw = [jnp.dot(qm, kt[:, :TK * (i + 1)], preferred_element_type=F32) for qm in qms]
        return [jnp.concatenate([s[:, :TK * i], s[:, TK * i:] + bias_ref[...]], axis=-1) if i else
                s + bias_ref[...] for s in raw]

    def attend(i, s):
        e = [jnp.exp2(sm - jnp.max(sm, axis=-1, keepdims=True)) for sm in s]
        l = [jnp.sum(em, axis=-1, keepdims=True) for em in e]
        a = (e[0] - (lam * l[0] / l[1]) * e[1]).astype(BF16)
        o = jnp.dot(a, v_ref[0:TK * (i + 1), :], preferred_element_type=F32) / l[0]
        emit(i, _subln(o, g_ref[...]).astype(BF16))

    n_q = SEQ // TQ
    s_next = scores(0)
    for i in range(n_q):
        s_now = s_next
        if i + 1 < n_q:
            s_next = scores(i + 1)
        between(i)
        attend(i, s_now)


def _diff_ffn_kernel(slope_ref, lamv_ref, gsub_ref, q_ref, k_ref, v_ref, x_ref, osb_ref, wout_ref, gffn_ref,
                     wg_ref, wu_ref, wd_ref, gfin_ref, y_ref, bias_ref, od_buf, act_ref):
    b, h = pl.program_id(0), pl.program_id(1)
    slot = b % 2
    slope2 = slope_ref[h] * LOG2E

    def emit(i, o):
        od_buf[slot, h, i * TQ:(i + 1) * TQ, :] = o

    @pl.when(b == 0)
    def _():
        _diff_head(slope2, _lambda(lamv_ref), gsub_ref, q_ref, k_ref, v_ref, bias_ref, emit, lambda i: None)

    @pl.when(b > 0)
    def _():
        tile_rows = pl.ds(pl.multiple_of(h * ROW_TILE, ROW_TILE), ROW_TILE)
        previous = lambda: jnp.concatenate([od_buf[1 - slot, hh, tile_rows, :] for hh in range(DIFF_HEADS)],
                                           axis=-1)
        pieces = _ffn_pieces(SEQ // TQ, x_ref, previous, osb_ref, wout_ref, gffn_ref, wg_ref, wu_ref, wd_ref,
                             gfin_ref, y_ref, act_ref)
        _diff_head(slope2, _lambda(lamv_ref), gsub_ref, q_ref, k_ref, v_ref, bias_ref, emit,
                   lambda i: pieces[i]())


def _diff_ffn(slopes, lamv, gsub, qd, kd16, vd16, x, osb, wout16, gffn, wg16, wu16, wd16, gfin):
    rows = x.shape[0]
    batch = rows // SEQ
    assert SEQ // ROW_TILE == DIFF_HEADS
    head_spec = pl.BlockSpec((SEQ, HEAD_LANES), lambda b, h: (jnp.minimum(b, batch - 1), h))
    tile = lambda b, h: (jnp.where(b == 0, 0, (b - 1) * DIFF_HEADS + h), 0)
    row_spec = lambda width: pl.BlockSpec((ROW_TILE, width), tile)
    const = lambda a: pl.BlockSpec(a.shape, lambda b, h: (0, 0))
    weight = lambda a: pl.BlockSpec(a.shape, lambda b, h: (0, 0), pipeline_mode=pl.Buffered(1))
    return pl.pallas_call(
        _diff_ffn_kernel,
        grid=(batch + 1, DIFF_HEADS),
        in_specs=[pl.BlockSpec(memory_space=pltpu.SMEM), const(lamv), const(gsub), head_spec, head_spec, head_spec,
                  row_spec(D_MODEL), row_spec(GROUP_WIDTH), weight(wout16), const(gffn), weight(wg16), weight(wu16),
                  weight(wd16), const(gfin)],
        out_specs=row_spec(D_MODEL),
        out_shape=jax.ShapeDtypeStruct((rows, D_MODEL), F32),
        scratch_shapes=[pltpu.VMEM((TQ, TK), F32), pltpu.VMEM((2, DIFF_HEADS, SEQ, HEAD_LANES), BF16),
                        pltpu.VMEM((ROW_TILE, D_FF), BF16)],
        compiler_params=pltpu.CompilerParams(dimension_semantics=("arbitrary", "arbitrary"),
                                             vmem_limit_bytes=FUSED_VMEM_LIMIT),
        name="diff_prompt_ffn",
    )(slopes, lamv, gsub, qd, kd16, vd16, x, osb, wout16, gffn, wg16, wu16, wd16, gfin)


def _sb_prompt_steps(q_ref, kt_ref, vt_ref, o_ref, u_ref, carry_ref, acc_ref):
    u_ref[...] = _suffix_matrix(SB_BLOCK)
    causal = (lax.broadcasted_iota(jnp.int32, (SB_BLOCK, SB_BLOCK), 1)
              < lax.broadcasted_iota(jnp.int32, (SB_BLOCK, SB_BLOCK), 0))
    lane = lax.broadcasted_iota(jnp.int32, (SB_BLOCK, HEAD_LANES), 1)

    def rows_of(i):
        return pl.ds(i * SB_BLOCK, SB_BLOCK)

    def begin(slot, i):
        q = q_ref[rows_of(i), :]
        carry_ref[slot] = jnp.zeros(carry_ref.shape[1:], F32)
        acc_ref[slot] = jnp.zeros(acc_ref.shape[1:], F32)
        return jnp.where(lane < HALF, q, jnp.zeros_like(q)), jnp.where(lane >= HALF, q, jnp.zeros_like(q))

    def add_blocks(items, diagonal, between=(lambda: None, lambda: None)):
        chains = [(n, h, t) for n, item in enumerate(items) for h in range(2) for t in reversed(range(item[3]))]
        mask = {c: causal if diagonal and c[2] == items[c[0]][3] - 1 else None for c in chains}
        z = {(n, h, t): jnp.dot(items[n][1][h], kt_ref[items[n][2] + t], preferred_element_type=F32)
             for n, h, t in chains}
        between[0]()
        part = {c: _stick_block(z[c], u_ref[...], mask[c]) for c in chains}
        between[1]()
        carry = {(n, h): carry_ref[item[0], h] for n, item in enumerate(items) for h in range(2)}
        weights = {}
        for n, h, t in chains:
            weights[n, h, t] = _stick_weights(part[n, h, t][0], carry[n, h], mask[n, h, t]).astype(BF16)
            carry[n, h] = carry[n, h] + part[n, h, t][1]
        for n, (slot, _, first, count) in enumerate(items):
            for h in range(2):
                acc = acc_ref[slot, h]
                for t in range(count):
                    acc = acc + lax.dot_general(weights[n, h, t], vt_ref[first + t], NT_DIMS,
                                                preferred_element_type=F32)
                acc_ref[slot, h] = acc
                carry_ref[slot, h] = carry[n, h]

    def finish(slot, qms, i, j):
        def unfinished(state):
            j, top = state
            return jnp.logical_and(j >= 0, top > EXIT_LOG2)

        def earlier(state):
            j, _ = state
            add_blocks([(slot, qms, j, 1)], False)
            return j - 1, jnp.max(carry_ref[slot])

        lax.while_loop(unfinished, earlier, (jnp.int32(j), jnp.max(carry_ref[slot])))
        o_ref[rows_of(i), :] = jnp.where(lane < HALF, acc_ref[slot, 0], acc_ref[slot, 1]).astype(BF16)

    def query_pair(i_a, count_a, i_b, between):
        qms_a, qms_b = begin(0, i_a), begin(1, i_b)
        first_a = i_a - (count_a - 1)
        add_blocks([(0, qms_a, first_a, count_a), (1, qms_b, i_b - 1, 2)], True, between)
        finish(0, qms_a, i_a, first_a - 1)
        finish(1, qms_b, i_b, i_b - 2)

    n_q = SEQ // SB_BLOCK
    pairs = [(0, 1, n_q - 1)] + [(2 * m + 1, 2, 2 * m + 2) for m in range((n_q - 2) // 2)]
    return [functools.partial(query_pair, *p) for p in pairs]


def _inproj_sb_kernel(batch, x_ref, g_ref, w_ref, wt_ref, qs2_ref, kn_ref, vn_ref, kct_ref, vct_ref,
                      qd_ref, kd_ref, vd_ref, ks_ref, vs_ref, kd16_ref, vd16_ref, osb2_ref, osb_ref,
                      scarry_ref, sacc_ref, q_buf, kt_buf, vt_buf, u_ref, carry_ref, acc_ref):
    b, t = pl.program_id(0), pl.program_id(1)
    slot = b % 2
    tile_rows = pl.ds(pl.multiple_of(t * ROW_TILE, ROW_TILE), ROW_TILE)
    blocks_per_tile = ROW_TILE // SB_BLOCK

    def put_qs(q):
        for p in range(SB_PAIRS):
            q_buf[slot, p, tile_rows, :] = q[:, p * HEAD_LANES:(p + 1) * HEAD_LANES]

    def put_transposed(c, out_ref, buf):
        def run(proj, state):
            rows = slice((c - 4) * GROUP_WIDTH, (c - 3) * GROUP_WIDTH)
            r = lax.dot_general(wt_ref[rows, :], state["n"], NT_DIMS, preferred_element_type=F32)
            out_ref[...] = r
            for p in range(SB_PAIRS):
                for j in range(blocks_per_tile):
                    buf[slot, p, t * blocks_per_tile + j] = r[p * HEAD_LANES:(p + 1) * HEAD_LANES,
                                                              j * SB_BLOCK:(j + 1) * SB_BLOCK].astype(BF16)
        return run

    def projection_slots():
        sample = _sb_sample_pieces(qs2_ref, kn_ref, vn_ref, kct_ref, vct_ref, osb2_ref, scarry_ref, sacc_ref)
        return _inproj_slots(x_ref, g_ref, w_ref, wt_ref, qd_ref, kd_ref, vd_ref, kd16_ref, vd16_ref, put_qs,
                             put_transposed(4, ks_ref, kt_buf), put_transposed(5, vs_ref, vt_buf), sample)

    def attention_steps():
        return _sb_prompt_steps(q_buf.at[1 - slot, t], kt_buf.at[1 - slot, t], vt_buf.at[1 - slot, t], osb_ref,
                                u_ref, carry_ref, acc_ref)

    @pl.when(b == 0)
    def _():
        for piece in projection_slots():
            piece()

    @pl.when(jnp.logical_and(b > 0, b < batch))
    def _():
        slots = projection_slots()
        for k, step in enumerate(attention_steps()):
            step((slots[2 * k], slots[2 * k + 1]))

    @pl.when(b == batch)
    def _():
        for step in attention_steps():
            step((lambda: None, lambda: None))


def _inproj_sb(x, g, w16, wt16, qs2, ks2_16, vs2_16, cache_kt, cache_vt):
    rows = x.shape[0]
    batch = rows // SEQ
    tiles = SEQ // ROW_TILE
    assert tiles == SB_PAIRS and cache_kt.shape[0] == batch
    tile = lambda b, t: jnp.where(b == batch, batch * tiles - 1, b * tiles + t)
    row_spec = lambda width: pl.BlockSpec((ROW_TILE, width), lambda b, t: (tile(b, t), 0))
    head_major = pl.BlockSpec((ROW_TILE * DIFF_HEADS, HEAD_LANES), lambda b, t: (tile(b, t), 0))
    weight = lambda a: pl.BlockSpec(a.shape, lambda b, t: (0, 0), pipeline_mode=pl.Buffered(1))
    sb32_spec = pl.BlockSpec((None, GROUP_WIDTH, ROW_TILE), lambda b, t: (tile(b, t) // tiles, 0, tile(b, t) % tiles))
    new_spec = pl.BlockSpec((DEC_SEQ, HEAD_LANES), lambda b, t: (tile(b, t) // tiles, tile(b, t) % tiles))
    cache_spec = pl.BlockSpec((None, HEAD_LANES, PAST_LEN), lambda b, t: (tile(b, t) // tiles, tile(b, t) % tiles, 0))
    pair_spec = pl.BlockSpec((SEQ, HEAD_LANES), lambda b, t: (jnp.maximum(b - 1, 0), jnp.where(b == 0, 0, t)))
    bf16_rows = jax.ShapeDtypeStruct((rows, GROUP_WIDTH), BF16)
    f32_head_major = jax.ShapeDtypeStruct((rows * DIFF_HEADS, HEAD_LANES), F32)
    sb32_shape = jax.ShapeDtypeStruct((batch, GROUP_WIDTH, SEQ), F32)
    n_blocks = SEQ // SB_BLOCK
    return pl.pallas_call(
        functools.partial(_inproj_sb_kernel, batch),
        grid=(batch + 1, tiles),
        in_specs=[row_spec(D_MODEL), pl.BlockSpec((1, D_MODEL), lambda b, t: (0, 0)), weight(w16), weight(wt16),
                  new_spec, new_spec, new_spec, cache_spec, cache_spec],
        out_specs=[row_spec(GROUP_WIDTH), head_major, head_major, sb32_spec, sb32_spec, row_spec(GROUP_WIDTH),
                   row_spec(GROUP_WIDTH), new_spec, pair_spec],
        out_shape=[bf16_rows, f32_head_major, f32_head_major, sb32_shape, sb32_shape, bf16_rows, bf16_rows,
                   jax.ShapeDtypeStruct(qs2.shape, BF16), bf16_rows],
        scratch_shapes=[pltpu.VMEM((HEAD_LANES, STAT_LANES), F32), pltpu.VMEM((HEAD_LANES, HEAD_LANES), F32),
                        pltpu.VMEM((2, SB_PAIRS, SEQ, HEAD_LANES), BF16),
                        pltpu.VMEM((2, SB_PAIRS, n_blocks, HEAD_LANES, SB_BLOCK), BF16),
                        pltpu.VMEM((2, SB_PAIRS, n_blocks, HEAD_LANES, SB_BLOCK), BF16),
                        pltpu.VMEM((SB_BLOCK, SB_BLOCK), BF16), pltpu.VMEM((2, 2, SB_BLOCK, STAT_LANES), F32),
                        pltpu.VMEM((2, 2, SB_BLOCK, HEAD_LANES), F32)],
        compiler_params=pltpu.CompilerParams(dimension_semantics=("arbitrary", "arbitrary"),
                                             vmem_limit_bytes=VMEM_LIMIT),
        name="inproj_sb_prompt",
    )(x, g, w16, wt16, qs2, ks2_16, vs2_16, cache_kt, cache_vt)


def _stack_queries(q):
    lane = lax.broadcasted_iota(jnp.int32, q.shape, 1)
    zero = jnp.zeros_like(q)
    return jnp.concatenate([jnp.where(lane < HALF, q, zero), jnp.where(lane >= HALF, q, zero)], axis=0)


def _diff_sample_kernel(slope_ref, lamv_ref, g_ref, q_ref, kn_ref, vn_ref, kc_ref, vc_ref, o_ref,
                        m_ref, l_ref, acc_ref):
    part = pl.program_id(1)
    heads = range(DIFF_HEADS)
    cols = [slice(h * HEAD_LANES, (h + 1) * HEAD_LANES) for h in heads]
    q2 = [_stack_queries(q_ref[:, cols[h]]) for h in heads]
    slope2 = [slope_ref[h] * LOG2E for h in heads]
    query = lax.broadcasted_iota(jnp.int32, (SAMPLE_KEYS, HEAD_LANES), 1) % HALF

    @pl.when(part == 0)
    def _():
        m_ref[...] = jnp.full(m_ref.shape, NEG_INF, F32)
        l_ref[...] = jnp.zeros(l_ref.shape, F32)
        acc_ref[...] = jnp.zeros(acc_ref.shape, F32)

    def update(scores, values):
        stats = {}
        for h in heads:
            m_old = m_ref[h]
            m_new = jnp.maximum(m_old, jnp.max(scores[h], axis=0, keepdims=True))
            alpha = jnp.exp2(m_old - m_new)
            p = jnp.exp2(scores[h] - m_new)
            l_ref[h] = alpha * l_ref[h] + jnp.sum(p, axis=0, keepdims=True)
            m_ref[h] = m_new
            stats[h] = (alpha, p.astype(BF16))
        for h in heads:
            alpha, p = stats[h]
            acc_ref[h] = alpha * acc_ref[h] + lax.dot_general(values(h), p, TN_DIMS, preferred_element_type=F32)

    head_rows = lambda ref, h: ref[pl.ds(h, SAMPLE_KEYS, stride=DIFF_HEADS), :].astype(BF16)
    kpos = part * SAMPLE_KEYS + lax.broadcasted_iota(jnp.int32, (SAMPLE_KEYS, HEAD_LANES), 0)
    dist = (PAST_LEN + query - kpos).astype(F32)
    update({h: lax.dot_general(head_rows(kc_ref, h), q2[h], NT_DIMS, preferred_element_type=F32) - slope2[h] * dist
            for h in heads}, lambda h: head_rows(vc_ref, h))

    @pl.when(part == PAST_LEN // SAMPLE_KEYS - 1)
    def _():
        lam = _lambda(lamv_ref)
        new_query = lax.broadcasted_iota(jnp.int32, (DEC_SEQ, HEAD_LANES), 1) % HALF
        new_dist = jnp.abs(new_query - lax.broadcasted_iota(jnp.int32, (DEC_SEQ, HEAD_LANES), 0)).astype(F32)
        update({h: lax.dot_general(kn_ref[:, cols[h]], q2[h], NT_DIMS, preferred_element_type=F32)
                - slope2[h] * new_dist for h in heads}, lambda h: vn_ref[:, cols[h]])
        for h in heads:
            o = jnp.transpose(acc_ref[h] / l_ref[h])
            o = o[0:DEC_SEQ] - lam * o[DEC_SEQ:2 * DEC_SEQ]
            o_ref[:, cols[h]] = _subln(o, g_ref[...]).astype(BF16)


def _diff_sample(slopes, lamv, g, qd, kd16, vd16, cache_k, cache_v):
    batch = cache_k.shape[0]
    new_spec = pl.BlockSpec((DEC_SEQ, GROUP_WIDTH), lambda b, part: (b, 0))
    cache_spec = pl.BlockSpec((None, SAMPLE_KEYS * DIFF_HEADS, HEAD_LANES), lambda b, part: (b, part, 0))
    return pl.pallas_call(
        _diff_sample_kernel,
        grid=(batch, PAST_LEN // SAMPLE_KEYS),
        in_specs=[pl.BlockSpec(memory_space=pltpu.SMEM),
                  pl.BlockSpec(lamv.shape, lambda b, part: (0, 0)),
                  pl.BlockSpec(g.shape, lambda b, part: (0, 0)),
                  new_spec, new_spec, new_spec, cache_spec, cache_spec],
        out_specs=new_spec,
        out_shape=jax.ShapeDtypeStruct(qd.shape, BF16),
        scratch_shapes=[pltpu.VMEM((DIFF_HEADS, 1, HEAD_LANES), F32), pltpu.VMEM((DIFF_HEADS, 1, HEAD_LANES), F32),
                        pltpu.VMEM((DIFF_HEADS, HEAD_LANES, HEAD_LANES), F32)],
        compiler_params=pltpu.CompilerParams(dimension_semantics=("arbitrary", "arbitrary"),
                                             vmem_limit_bytes=VMEM_LIMIT),
        name="diff_sample",
    )(slopes, lamv, g, qd, kd16, vd16, cache_k, cache_v)


def _sb_sample_pieces(q_ref, kn_ref, vn_ref, kct_ref, vct_ref, o_ref, carry_ref, acc_ref):
    state = {}
    n_sweeps = PAST_LEN // SAMPLE_SWEEP
    n_blocks = SAMPLE_SWEEP // CUM_BLOCK
    last_cols = slice((n_sweeps - 1) * SAMPLE_SWEEP, n_sweeps * SAMPLE_SWEEP)
    key_idx = lax.broadcasted_iota(jnp.int32, (HEAD_LANES, DEC_SEQ), 1)
    query_idx = lax.broadcasted_iota(jnp.int32, (HEAD_LANES, DEC_SEQ), 0) % HALF
    visible = key_idx < query_idx

    def block_parts(z):
        return [_stick_block(z[:, b * CUM_BLOCK:(b + 1) * CUM_BLOCK], state["suffix"], None) for b in range(n_blocks)]

    def add_values(parts, cols):
        carry = carry_ref[...]
        weights = [None] * n_blocks
        for b in reversed(range(n_blocks)):
            weights[b] = _stick_weights(parts[b][0], carry, None)
            carry = carry + parts[b][1]
        a = jnp.concatenate(weights, axis=-1).astype(BF16)
        acc_ref[...] += lax.dot_general(a, vct_ref[:, cols].astype(BF16), NT_DIMS, preferred_element_type=F32)
        carry_ref[...] = carry
        return jnp.max(carry)

    def logits():
        state["q2"] = _stack_queries(q_ref[...])
        state["suffix"] = _suffix_matrix(CUM_BLOCK)
        state["z_new"] = lax.dot_general(state["q2"], kn_ref[...], NT_DIMS, preferred_element_type=F32)
        state["z_last"] = jnp.dot(state["q2"], kct_ref[:, last_cols].astype(BF16), preferred_element_type=F32)

    def block_sums():
        state["new"] = _stick_block(state["z_new"], _suffix_matrix(DEC_SEQ), visible)
        state["last"] = block_parts(state["z_last"])

    def new_values():
        log2w, carry = state["new"]
        a = jnp.where(visible, jnp.exp2(log2w), 0.0)
        acc_ref[...] = jnp.dot(a.astype(BF16), vn_ref[...], preferred_element_type=F32)
        carry_ref[...] = jnp.broadcast_to(carry, carry_ref.shape)

    def last_values():
        state["top"] = add_values(state["last"], last_cols)

    def earlier():
        def unfinished(loop):
            j, top = loop
            return jnp.logical_and(j >= 0, top > EXIT_LOG2)

        def body(loop):
            j, _ = loop
            cols = pl.ds(pl.multiple_of(j * SAMPLE_SWEEP, SAMPLE_SWEEP), SAMPLE_SWEEP)
            z = jnp.dot(state["q2"], kct_ref[:, cols].astype(BF16), preferred_element_type=F32)
            return j - 1, add_values(block_parts(z), cols)

        lax.while_loop(unfinished, body, (n_sweeps - 2, state["top"]))
        lane = lax.broadcasted_iota(jnp.int32, (DEC_SEQ, HEAD_LANES), 1)
        o_ref[...] = jnp.where(lane < HALF, acc_ref[0:DEC_SEQ, :], acc_ref[DEC_SEQ:2 * DEC_SEQ, :]).astype(BF16)

    return [logits, block_sums, new_values, last_values, earlier]


def kernel(x_prompt, x_sample, cache_diff_k, cache_diff_v, cache_sb_k, cache_sb_v, norm_attn_g, w_in,
           lambda_q1, lambda_k1, lambda_q2, lambda_k2, diff_subln_g, w_out, norm_ffn_g, w_gate, w_up, w_down,
           norm_final_g):
    batch, seq, _ = x_prompt.shape
    dec_batch, dec_seq, _ = x_sample.shape
    assert seq == SEQ and dec_seq == DEC_SEQ and cache_diff_k.shape[2] == PAST_LEN and w_in.shape[0] == 1

    w_in16 = w_in[0].astype(BF16)
    w_sb_t16 = jnp.transpose(w_in[0][:, 4 * GROUP_WIDTH:]).astype(BF16)
    w_out16 = w_out[0].astype(BF16)
    w_gate16 = w_gate[0].astype(BF16)
    w_up16 = w_up[0].astype(BF16)
    w_down16 = w_down[0].astype(BF16)
    g_attn = norm_attn_g[0].reshape(1, D_MODEL)
    g_ffn = norm_ffn_g[0].reshape(1, D_MODEL)
    g_final = norm_final_g.reshape(1, D_MODEL)
    g_subln = diff_subln_g[0].reshape(1, HEAD_LANES)
    lamv = jnp.concatenate([lambda_q1, lambda_k1, lambda_q2, lambda_k2], axis=0).astype(F32)
    slopes = jnp.exp2(-8.0 / DIFF_HEADS * jnp.arange(1, DIFF_HEADS + 1, dtype=F32))

    def ffn(x, od, osb):
        return _ffn(x, od, osb, w_out16, g_ffn, w_gate16, w_up16, w_down16, g_final)

    xs = x_sample.reshape(dec_batch * DEC_SEQ, D_MODEL)
    qd2, qs2, kd2, vd2, ks2, vs2, kd2_16, vd2_16, ks2_16, vs2_16 = _inproj_sample(xs, g_attn, w_in16)
    keys_minor = lambda a: jnp.transpose(a[0], (0, 2, 3, 1)).reshape(dec_batch, GROUP_WIDTH, PAST_LEN)

    xp = x_prompt.reshape(batch * SEQ, D_MODEL)
    qd, kd, vd, kst, vst, kd16, vd16, osb2, osb = _inproj_sb(
        xp, g_attn, w_in16, w_sb_t16, qs2, ks2_16, vs2_16, keys_minor(cache_sb_k), keys_minor(cache_sb_v))
    y_prompt = _diff_ffn(slopes, lamv, g_subln, qd, kd16, vd16, xp, osb, w_out16, g_ffn, w_gate16, w_up16, w_down16,
                         g_final).reshape(batch, SEQ, D_MODEL)

    head_major = lambda a: a[0].reshape(dec_batch, PAST_LEN * DIFF_HEADS, HEAD_LANES)
    od2 = _diff_sample(slopes, lamv, g_subln, qd2, kd2_16, vd2_16, head_major(cache_diff_k), head_major(cache_diff_v))
    y_sample = ffn(xs, od2, osb2).reshape(dec_batch, DEC_SEQ, D_MODEL)

    diff_shape = lambda b, t: (1, b, t, DIFF_HEADS, HEAD_LANES)
    sb_shape = lambda b, t: (1, b, t, 2 * SB_PAIRS, HALF)
    from_keys_minor = lambda a: jnp.transpose(a.reshape(batch, 2 * SB_PAIRS, HALF, SEQ), (0, 3, 1, 2))[None]
    return (y_prompt, y_sample,
            kd.reshape(diff_shape(batch, SEQ)), vd.reshape(diff_shape(batch, SEQ)),
            from_keys_minor(kst), from_keys_minor(vst),
            kd2.reshape(diff_shape(dec_batch, DEC_SEQ)), vd2.reshape(diff_shape(dec_batch, DEC_SEQ)),
            ks2.reshape(sb_shape(dec_batch, DEC_SEQ)), vs2.reshape(sb_shape(dec_batch, DEC_SEQ)))
```
